```python
import jax, jax.numpy as jnp
from jax import lax
import numpy as np

D_MODEL = 4096
BATCH = 4
SEQ = 2048
DEPTH = 1

N_HEADS = 16
HEAD_DIM = 128
N_KV_HEADS = 4
ATT_WIDTH = N_HEADS * HEAD_DIM
KV_WIDTH = N_KV_HEADS * HEAD_DIM
IDX_HEADS = 32
IDX_DIM = 64
TOPK_MAX = 256
Q_BLOCK = 128
GM_WIDTH = 2048
GM_GROUPS = 8
GM_GROUP_W = GM_WIDTH // GM_GROUPS
GM_CHUNK = 128
D_FF = 11008
CONV_W = 3
EPS = 1e-6
NEG_BIG = -1e30
IN_SIZES = (ATT_WIDTH, KV_WIDTH, KV_WIDTH, IDX_HEADS * IDX_DIM, IDX_DIM, IDX_HEADS,
            GM_WIDTH, GM_WIDTH, D_MODEL, D_MODEL)
IN_WIDTH = (ATT_WIDTH + 2 * KV_WIDTH + IDX_HEADS * IDX_DIM + IDX_DIM + IDX_HEADS
            + 2 * GM_WIDTH + 2 * D_MODEL)

kernel_name = 'hybrid_dsa_gmlp_convffn_adaln'


def _split_points():
    return np.cumsum(np.array(IN_SIZES))[:-1].tolist()


def _rmsnorm(x, g):
    xf = x.astype(jnp.float32)
    y = xf * lax.rsqrt(jnp.mean(xf * xf, axis=-1, keepdims=True) + EPS)
    return (y * g.astype(jnp.float32)).astype(x.dtype)


def _alibi_slopes(n):
    return jnp.asarray([2.0 ** (-8.0 * (i + 1) / n) for i in range(n)], dtype=jnp.float32)


def _dsa_attention(q, k, v, q_idx, k_idx, w_idx):
    B, S = q.shape[0], q.shape[1]
    topk = min(TOPK_MAX, S // 4)
    n_blk = S // Q_BLOCK
    rep = N_HEADS // N_KV_HEADS
    slopes = _alibi_slopes(N_HEADS).reshape(N_KV_HEADS, rep)
    key_pos = jnp.arange(S, dtype=jnp.int32)
    scale = HEAD_DIM ** -0.5
    idx_scale = IDX_DIM ** -0.5
    w_scale = IDX_HEADS ** -0.5

    def block(i):
        t0 = i * Q_BLOCK
        qb = lax.dynamic_slice_in_dim(q, t0, Q_BLOCK, axis=1).reshape(B, Q_BLOCK, N_KV_HEADS, rep, HEAD_DIM)
        qib = lax.dynamic_slice_in_dim(q_idx, t0, Q_BLOCK, axis=1)
        wb = lax.dynamic_slice_in_dim(w_idx, t0, Q_BLOCK, axis=1)
        q_pos = t0 + jnp.arange(Q_BLOCK, dtype=jnp.int32)
        causal = key_pos[None, :] <= q_pos[:, None]
        logits = jnp.einsum('bthd,bsd->btsh', qib, k_idx).astype(jnp.float32) * idx_scale
        iscore = jnp.einsum('btsh,bth->bts', jax.nn.relu(logits), wb.astype(jnp.float32) * w_scale)
        iscore = jnp.where(causal[None], iscore, -jnp.inf)
        _, sel = lax.top_k(iscore, topk)
        valid = sel <= q_pos[None, :, None]
        k_sel = jax.vmap(lambda kk, ii: kk[ii])(k, sel)
        v_sel = jax.vmap(lambda vv, ii: vv[ii])(v, sel)
        s = jnp.einsum('btgrd,btkgd->btgrk', qb, k_sel).astype(jnp.float32) * scale
        dist = (q_pos[None, :, None] - sel).astype(jnp.float32)
        s = s - slopes[None, None, :, :, None] * dist[:, :, None, None, :]
        s = jnp.where(valid[:, :, None, None, :], s, NEG_BIG)
        p = jax.nn.softmax(s, axis=-1).astype(v.dtype)
        o = jnp.einsum('btgrk,btkgd->btgrd', p, v_sel)
        return o.reshape(B, Q_BLOCK, ATT_WIDTH)

    out = lax.map(block, jnp.arange(n_blk, dtype=jnp.int32))
    return out.transpose(1, 0, 2, 3).reshape(B, S, ATT_WIDTH)


def _chunked_sgu(u, v, g, w_s, b_s):
    B, S = u.shape[0], u.shape[1]
    n = S // GM_CHUNK
    vn = _rmsnorm(v, g).reshape(B, n, GM_CHUNK, GM_GROUPS, GM_GROUP_W)
    mask = jnp.tril(jnp.ones((GM_CHUNK, GM_CHUNK), dtype=bool))
    w = jnp.where(mask[None], w_s, jnp.zeros_like(w_s))
    f = jnp.einsum('gts,bnsgc->bntgc', w, vn) + b_s.T[None, None, :, :, None]
    return u * f.reshape(B, S, GM_WIDTH)


def _conv_ffn(h, w_up, conv_w, conv_b, w_down):
    S = h.shape[1]
    a = h @ w_up
    ap = jnp.pad(a, ((0, 0), (CONV_W - 1, 0), (0, 0)))
    acc = conv_b
    for j in range(CONV_W):
        acc = acc + ap[:, j:j + S] * conv_w[j]
    gate, val = jnp.split(acc, 2, axis=-1)
    return (jax.nn.silu(gate) * val) @ w_down


def setup_inputs(seed: int = 0) -> dict:
    key = jax.random.key(seed)
    ks = jax.random.split(key, 20)
    f32 = jnp.float32
    nrm = lambda k, shape, s: jax.random.normal(k, shape, f32) * s
    L = DEPTH
    return {
        'x': nrm(ks[0], (BATCH, SEQ, D_MODEL), 1.0),
        'c': nrm(ks[1], (BATCH, D_MODEL), 1.0),
        'ada_w': nrm(ks[2], (L, D_MODEL, 6 * D_MODEL), 0.5 * D_MODEL ** -0.5),
        'ada_b': nrm(ks[3], (L, 6 * D_MODEL), 0.01),
        'norm1_g': 1.0 + nrm(ks[4], (L, D_MODEL), 0.02),
        'w_in': nrm(ks[5], (L, D_MODEL, IN_WIDTH), D_MODEL ** -0.5),
        'q_norm_g': 1.0 + nrm(ks[6], (L, HEAD_DIM), 0.02),
        'k_norm_g': 1.0 + nrm(ks[7], (L, HEAD_DIM), 0.02),
        'sgu_norm_g': 1.0 + nrm(ks[8], (L, GM_WIDTH), 0.02),
        'sgu_w': nrm(ks[9], (L, GM_GROUPS, GM_CHUNK, GM_CHUNK), GM_CHUNK ** -0.5),
        'sgu_b': 1.0 + nrm(ks[10], (L, GM_GROUPS, GM_CHUNK), 0.1),
        'w_branch_a': nrm(ks[11], (L, ATT_WIDTH, D_MODEL), ATT_WIDTH ** -0.5),
        'w_branch_b': nrm(ks[12], (L, GM_WIDTH, D_MODEL), GM_WIDTH ** -0.5),
        'w_out': nrm(ks[13], (L, D_MODEL, D_MODEL), D_MODEL ** -0.5),
        'norm2_g': 1.0 + nrm(ks[14], (L, D_MODEL), 0.02),
        'w_up': nrm(ks[15], (L, D_MODEL, 2 * D_FF), D_MODEL ** -0.5),
        'conv_w': nrm(ks[16], (L, CONV_W, 2 * D_FF), CONV_W ** -0.5),
        'conv_b': nrm(ks[17], (L, 2 * D_FF), 0.01),
        'w_down': nrm(ks[18], (L, D_FF, D_MODEL), D_FF ** -0.5),
    }


def reference(x, c, ada_w, ada_b, norm1_g, w_in, q_norm_g, k_norm_g, sgu_norm_g, sgu_w, sgu_b,
              w_branch_a, w_branch_b, w_out, norm2_g, w_up, conv_w, conv_b, w_down):
    B, S = x.shape[0], x.shape[1]
    cs = jax.nn.silu(c)
    splits = _split_points()
    for l in range(DEPTH):
        mod = cs @ ada_w[l] + ada_b[l]
        sh1, sc1, g1, sh2, sc2, g2 = [m[:, None, :] for m in jnp.split(mod, 6, axis=-1)]
        h = _rmsnorm(x, norm1_g[l]) * (1 + sc1) + sh1
        proj = h @ w_in[l]
        q, k, v, qi, ki, wi, gu, gv, ga, gb = jnp.split(proj, splits, axis=-1)
        q = _rmsnorm(q.reshape(B, S, N_HEADS, HEAD_DIM), q_norm_g[l])
        k = _rmsnorm(k.reshape(B, S, N_KV_HEADS, HEAD_DIM), k_norm_g[l])
        v = v.reshape(B, S, N_KV_HEADS, HEAD_DIM)
        qi = qi.reshape(B, S, IDX_HEADS, IDX_DIM)
        y_a = _dsa_attention(q, k, v, qi, ki, wi)
        y_b = _chunked_sgu(jax.nn.gelu(gu, approximate=False), jax.nn.gelu(gv, approximate=False),
                           sgu_norm_g[l], sgu_w[l], sgu_b[l])
        merged = jax.nn.sigmoid(ga) * (y_a @ w_branch_a[l]) + jax.nn.sigmoid(gb) * (y_b @ w_branch_b[l])
        x = x + g1 * (merged @ w_out[l])
        h2 = _rmsnorm(x, norm2_g[l]) * (1 + sc2) + sh2
        x = x + g2 * _conv_ffn(h2, w_up[l], conv_w[l], conv_b[l], w_down[l])
    return x
```

```python
import functools
import math

import jax
import jax.numpy as jnp
from jax import lax
from jax.experimental import pallas as pl
from jax.experimental.pallas import tpu as pltpu

N_HEADS = 16
HEAD_DIM = 128
N_KV_HEADS = 4
ATT_WIDTH = N_HEADS * HEAD_DIM
KV_WIDTH = N_KV_HEADS * HEAD_DIM
IDX_HEADS = 32
IDX_DIM = 64
TOPK_MAX = 256
GM_WIDTH = 2048
GM_GROUPS = 8
GM_GROUP_W = GM_WIDTH // GM_GROUPS
GM_CHUNK = 128
CONV_W = 3
EPS = 1e-6
NEG_BIG = -1e30

V7X_LANES = 128
V7X_VMEM_BYTES = 64 * 1024 * 1024
VMEM_CAP = V7X_VMEM_BYTES - 8 * 1024 * 1024

BF16 = jnp.bfloat16
F32 = jnp.float32
INT_MIN = -(2 ** 31)


def _vmem_limit(*nbytes):
    return int(min(VMEM_CAP, 2 * sum(nbytes) + (4 << 20)))


def _params(semantics, *nbytes):
    return pltpu.CompilerParams(dimension_semantics=semantics, vmem_limit_bytes=_vmem_limit(*nbytes))


def _pick(n, prefs):
    for p in prefs:
        if n % p == 0:
            return p
    return n


def _ada_kernel(c_ref, w_ref, b_ref, o_ref):
    c = c_ref[...]
    cs = c * jax.nn.sigmoid(c)
    o_ref[...] = jnp.dot(cs.astype(BF16), w_ref[...].astype(BF16),
                         preferred_element_type=F32) + b_ref[...]


def _ada(c_pad, ada_w, ada_b):
    bp, d = c_pad.shape
    n = ada_w.shape[1]
    tn = _pick(n, (512, 256, 128))
    return pl.pallas_call(
        _ada_kernel,
        grid=(n // tn,),
        in_specs=[pl.BlockSpec((bp, d), lambda j: (0, 0)),
                  pl.BlockSpec((d, tn), lambda j: (0, j)),
                  pl.BlockSpec((1, tn), lambda j: (0, j))],
        out_specs=pl.BlockSpec((bp, tn), lambda j: (0, j)),
        out_shape=jax.ShapeDtypeStruct((bp, n), F32),
        compiler_params=_params(("arbitrary",), 2 * d * tn * 4, d * tn * 2),
        name="ada_mod",
    )(c_pad, ada_w, ada_b.reshape(1, n))


def _norm_mod_kernel(x_ref, g_ref, sc_ref, sh_ref, o_ref):
    x = x_ref[...]
    y = x * lax.rsqrt(jnp.mean(x * x, axis=-1, keepdims=True) + EPS) * g_ref[...]
    o_ref[...] = (y * (1.0 + sc_ref[...]) + sh_ref[...]).astype(o_ref.dtype)


def _norm_mod(x2, g, mod3, sc_idx, sh_idx, seq):
    m, d = x2.shape
    tm = _pick(seq, (256, 128, 64, 8))
    per_b = seq // tm
    return pl.pallas_call(
        _norm_mod_kernel,
        grid=(m // tm,),
        in_specs=[pl.BlockSpec((tm, d), lambda i: (i, 0)),
                  pl.BlockSpec((1, d), lambda i: (0, 0)),
                  pl.BlockSpec((None, 1, d), lambda i: (i // per_b, 0, sc_idx)),
                  pl.BlockSpec((None, 1, d), lambda i: (i // per_b, 0, sh_idx))],
        out_specs=pl.BlockSpec((tm, d), lambda i: (i, 0)),
        out_shape=jax.ShapeDtypeStruct((m, d), BF16),
        compiler_params=_params(("arbitrary",), 2 * tm * d * 4, 2 * tm * d * 2),
        name="norm_mod",
    )(x2, g.reshape(1, d), mod3, mod3)


def _gelu_exact(x):
    return 0.5 * x * (1.0 + lax.erf(x * (2.0 ** -0.5)))


def _proj_kernel(a_ref, w_ref, *rest, epilogue, post_scale):
    o_ref = rest[-1]
    acc = jnp.dot(a_ref[...], w_ref[...], preferred_element_type=F32)
    if epilogue == "plain":
        out = acc
    elif epilogue == "gelu":
        out = _gelu_exact(acc)
    elif epilogue == "sigmoid":
        out = jax.nn.sigmoid(acc)
    elif epilogue == "headnorm":
        gain = rest[0][...]
        tn = acc.shape[1]
        parts = []
        for c in range(tn // HEAD_DIM):
            blk = acc[:, c * HEAD_DIM:(c + 1) * HEAD_DIM]
            ms = jnp.mean(blk * blk, axis=-1, keepdims=True)
            parts.append(blk * lax.rsqrt(ms + EPS) * (gain * post_scale))
        out = jnp.concatenate(parts, axis=1) if len(parts) > 1 else parts[0]
    else:
        raise ValueError(epilogue)
    o_ref[...] = out.astype(o_ref.dtype)


def _proj(a, w, col0, n, *, out_dtype, epilogue="plain", gain=None, post_scale=1.0, name):
    m, k = a.shape
    tm = _pick(m, (1024, 512, 256, 128))
    tn = _pick(math.gcd(n, col0) if col0 else n, (512, 256, 128))
    j0 = col0 // tn
    in_specs = [pl.BlockSpec((tm, k), lambda i, j: (i, 0)),
                pl.BlockSpec((k, tn), lambda i, j: (0, j + j0))]
    args = [a, w]
    if epilogue == "headnorm":
        in_specs.append(pl.BlockSpec((1, HEAD_DIM), lambda i, j: (0, 0)))
        args.append(gain.reshape(1, HEAD_DIM))
    osz = jnp.dtype(out_dtype).itemsize
    return pl.pallas_call(
        functools.partial(_proj_kernel, epilogue=epilogue, post_scale=post_scale),
        grid=(m // tm, n // tn),
        in_specs=in_specs,
        out_specs=pl.BlockSpec((tm, tn), lambda i, j: (i, j)),
        out_shape=jax.ShapeDtypeStruct((m, n), out_dtype),
        compiler_params=_params(("arbitrary", "arbitrary"),
                                2 * tm * k * 2, 2 * k * tn * 2, 2 * tm * tn * osz, tm * tn * 4),
        name=name,
    )(*args)


def _alibi_slopes(n):
    return [2.0 ** (-8.0 * (i + 1) / n) for i in range(n)]


def _attn_kernel(q_ref, qi_ref, kwq_ref, k_ref, v_ref, kwa_ref, y_ref,
                 keys_scr, bias_scr, wt_scr, sel_scr, *, tq, seq, topk):
    i = pl.program_id(1)
    n_chunks = i + 1
    t0 = i * tq
    rep = N_HEADS // N_KV_HEADS
    nt = (((1,), (1,)), ((), ()))
    row = lax.broadcasted_iota(jnp.int32, (tq, tq), 0)
    col = lax.broadcasted_iota(jnp.int32, (tq, tq), 1)
    kf = float(topk)

    w_fold = (IDX_HEADS ** -0.5) * (IDX_DIM ** -0.5)
    wt_scr[...] = (kwq_ref[:, 2 * V7X_LANES:3 * V7X_LANES] * w_fold).T

    def score_body(c, carry):
        off = pl.multiple_of(c * tq, tq)
        kraw = kwa_ref[pl.ds(off, tq), 0:2 * V7X_LANES].astype(BF16)
        lhs = jnp.concatenate([kraw[:, :V7X_LANES], kraw[:, V7X_LANES:]], axis=0)
        acc = jnp.zeros((tq, tq), F32)
        for p in range(IDX_HEADS // 2):
            lg = lax.dot_general(lhs, qi_ref[:, p * V7X_LANES:(p + 1) * V7X_LANES], nt,
                                 preferred_element_type=F32)
            acc = acc + wt_scr[2 * p:2 * p + 1, :] * jnp.maximum(lg[:tq], 0.0)
            acc = acc + wt_scr[2 * p + 1:2 * p + 2, :] * jnp.maximum(lg[tq:], 0.0)
        bits = lax.bitcast_convert_type(acc, jnp.int32)
        key = bits ^ ((bits >> 31) & jnp.int32(0x7FFFFFFF))
        keys_scr[c] = jnp.where(off + row <= t0 + col, key, jnp.int32(INT_MIN))
        return carry

    lax.fori_loop(0, n_chunks, score_body, 0)

    def count(pred):
        def body(c, cnt):
            off = c * tq
            return cnt + jnp.sum(jnp.where(pred(keys_scr[c], off + row), 1.0, 0.0),
                                 axis=0, keepdims=True)
        return lax.fori_loop(0, n_chunks, body, jnp.zeros((1, tq), F32))

    def select_topk():
        cnt_nonneg = count(lambda kb, s: kb >= 0)
        prefix0 = jnp.where(cnt_nonneg >= kf, jnp.int32(0), jnp.int32(INT_MIN))

        def bit_body(j, prefix):
            cand = prefix | lax.shift_left(jnp.int32(1), 30 - j)
            cnt = count(lambda kb, s: kb >= cand)
            return jnp.where(cnt >= kf, cand, prefix)

        thr = lax.fori_loop(0, 31, bit_body, prefix0)
        cnt_gt = count(lambda kb, s: kb > thr)
        cnt_ge = count(lambda kb, s: kb >= thr)
        need = kf - cnt_gt
        sel_scr[0:1, :] = thr
        sel_scr[1:2, :] = jnp.full((1, tq), seq, jnp.int32)

        @pl.when(jnp.max(cnt_ge) > kf)
        def _():
            nbits = max(1, (seq - 1).bit_length())

            def idx_body(j, m):
                cand = m | lax.shift_left(jnp.int32(1), nbits - 1 - j)
                cnt = count(lambda kb, s: (kb == thr) & (s < cand))
                return jnp.where(cnt < need, cand, m)

            sel_scr[1:2, :] = lax.fori_loop(0, nbits, idx_body, jnp.zeros((1, tq), jnp.int32))

    if tq <= topk:
        @pl.when(i == 0)
        def _():
            sel_scr[0:1, :] = jnp.full((1, tq), INT_MIN, jnp.int32)
            sel_scr[1:2, :] = jnp.full((1, tq), -1, jnp.int32)

        pl.when(i > 0)(select_topk)
    else:
        select_topk()

    thr = sel_scr[0:1, :]
    m_idx = sel_scr[1:2, :]

    def bias_body(c, carry):
        off = c * tq
        kb = keys_scr[c]
        s_idx = off + row
        take = jnp.where(kb > thr, 1.0, jnp.where((kb == thr) & (s_idx <= m_idx), 1.0, 0.0))
        take = jnp.where(s_idx <= t0 + col, take, 0.0)
        bias_scr[c] = jnp.where(take > 0.5, 0.0, NEG_BIG).T
        return carry

    lax.fori_loop(0, n_chunks, bias_body, 0)

    slopes = _alibi_slopes(N_HEADS)
    d0 = (row - col).astype(F32)

    for g in range(N_KV_HEADS):
        qg = jnp.concatenate(
            [q_ref[:, (g * rep + r) * HEAD_DIM:(g * rep + r + 1) * HEAD_DIM] for r in range(rep)], axis=0)

        def attn_body(c, carry, g=g, qg=qg):
            m_i, l_i, acc = carry
            off = pl.multiple_of(c * tq, tq)
            kc = k_ref[pl.ds(off, tq), g * HEAD_DIM:(g + 1) * HEAD_DIM]
            vc = v_ref[pl.ds(off, tq), g * HEAD_DIM:(g + 1) * HEAD_DIM]
            s = lax.dot_general(qg, kc, nt, preferred_element_type=F32)
            dist = d0 + (t0 - off).astype(F32)
            bias = bias_scr[c]
            s = jnp.concatenate(
                [s[r * tq:(r + 1) * tq] - slopes[g * rep + r] * dist + bias for r in range(rep)], axis=0)
            m_new = jnp.maximum(m_i, jnp.max(s, axis=-1, keepdims=True))
            alpha = jnp.exp(m_i - m_new)
            p = jnp.exp(s - m_new)
            l_new = alpha * l_i + jnp.sum(p, axis=-1, keepdims=True)
            acc_new = alpha * acc + jnp.dot(p.astype(BF16), vc, preferred_element_type=F32)
            return m_new, l_new, acc_new

        init = (jnp.full((rep * tq, 1), NEG_BIG, F32), jnp.zeros((rep * tq, 1), F32),
                jnp.zeros((rep * tq, HEAD_DIM), F32))
        _, l_f, acc_f = lax.fori_loop(0, n_chunks, attn_body, init)
        o = acc_f / l_f
        for r in range(rep):
            h = g * rep + r
            y_ref[:, h * HEAD_DIM:(h + 1) * HEAD_DIM] = o[r * tq:(r + 1) * tq].astype(y_ref.dtype)


def _attention(q, qi, kw, k, v, batch, seq):
    m = q.shape[0]
    topk = min(TOPK_MAX, seq // 4)
    tq = _pick(seq, (256, 128))
    nq = seq // tq
    kww = kw.shape[1]
    return pl.pallas_call(
        functools.partial(_attn_kernel, tq=tq, seq=seq, topk=topk),
        grid=(batch, nq),
        in_specs=[pl.BlockSpec((tq, ATT_WIDTH), lambda b, i: (b * nq + i, 0)),
                  pl.BlockSpec((tq, IDX_HEADS * IDX_DIM), lambda b, i: (b * nq + i, 0)),
                  pl.BlockSpec((tq, kww), lambda b, i: (b * nq + i, 0)),
                  pl.BlockSpec((seq, KV_WIDTH), lambda b, i: (b, 0)),
                  pl.BlockSpec((seq, KV_WIDTH), lambda b, i: (b, 0)),
                  pl.BlockSpec((seq, kww), lambda b, i: (b, 0))],
        out_specs=pl.BlockSpec((tq, ATT_WIDTH), lambda b, i: (b * nq + i, 0)),
        out_shape=jax.ShapeDtypeStruct((m, ATT_WIDTH), BF16),
        scratch_shapes=[pltpu.VMEM((nq, tq, tq), jnp.int32),
                        pltpu.VMEM((nq, tq, tq), F32),
                        pltpu.VMEM((V7X_LANES, tq), F32),
                        pltpu.VMEM((8, tq), jnp.int32)],
        compiler_params=_params(("arbitrary", "arbitrary"),
                                6 * tq * ATT_WIDTH * 2, 4 * seq * KV_WIDTH * 2, 2 * (seq + tq) * kww * 4,
                                2 * seq * tq * 4, 8 * tq * tq * 4),
        name="dsa_attention",
    )(q, qi, kw, k, v, kw)


def _sgu_kernel(u_ref, v_ref, g_ref, w_ref, bt_ref, o_ref, *, rows):
    v = v_ref[...].astype(F32)
    vn = (v * lax.rsqrt(jnp.mean(v * v, axis=-1, keepdims=True) + EPS) * g_ref[...]).astype(BF16)
    r_i = lax.broadcasted_iota(jnp.int32, (GM_CHUNK, GM_CHUNK), 0)
    c_i = lax.broadcasted_iota(jnp.int32, (GM_CHUNK, GM_CHUNK), 1)
    for g in range(GM_GROUPS):
        w = jnp.where(r_i >= c_i, w_ref[g], 0.0).astype(BF16)
        bcol = bt_ref[:, g:g + 1]
        cs = slice(g * GM_GROUP_W, (g + 1) * GM_GROUP_W)
        for n in range(rows // GM_CHUNK):
            rs = slice(n * GM_CHUNK, (n + 1) * GM_CHUNK)
            f = jnp.dot(w, vn[rs, cs], preferred_element_type=F32) + bcol
            o_ref[rs, cs] = (u_ref[rs, cs].astype(F32) * f).astype(o_ref.dtype)


def _sgu(uv, gain, w_s, b_s, seq):
    m = uv.shape[0]
    rows = _pick(seq, (512, 256, 128))
    return pl.pallas_call(
        functools.partial(_sgu_kernel, rows=rows),
        grid=(m // rows,),
        in_specs=[pl.BlockSpec((rows, GM_WIDTH), lambda i: (i, 0)),
                  pl.BlockSpec((rows, GM_WIDTH), lambda i: (i, 1)),
                  pl.BlockSpec((1, GM_WIDTH), lambda i: (0, 0)),
                  pl.BlockSpec((GM_GROUPS, GM_CHUNK, GM_CHUNK), lambda i: (0, 0, 0)),
                  pl.BlockSpec((GM_CHUNK, GM_GROUPS), lambda i: (0, 0))],
        out_specs=pl.BlockSpec((rows, GM_WIDTH), lambda i: (i, 0)),
        out_shape=jax.ShapeDtypeStruct((m, GM_WIDTH), BF16),
        compiler_params=_params(("arbitrary",), 6 * rows * GM_WIDTH * 2, 2 * rows * GM_WIDTH * 4),
        name="sgu",
    )(uv, uv, gain.reshape(1, GM_WIDTH), w_s, b_s.T)


def _merge_kernel(ya_ref, yb_ref, wa_ref, wb_ref, ga_ref, gb_ref, o_ref):
    pa = jnp.dot(ya_ref[...], wa_ref[...], preferred_element_type=F32)
    pb = jnp.dot(yb_ref[...], wb_ref[...], preferred_element_type=F32)
    o_ref[...] = (ga_ref[...].astype(F32) * pa + gb_ref[...].astype(F32) * pb).astype(o_ref.dtype)


def _merge(ya, yb, wa, wb, gates):
    m, ka = ya.shape
    kb = yb.shape[1]
    d = wa.shape[1]
    tm = _pick(m, (1024, 512, 256, 128))
    tn = _pick(d, (512, 256, 128))
    nj = d // tn
    return pl.pallas_call(
        _merge_kernel,
        grid=(m // tm, nj),
        in_specs=[pl.BlockSpec((tm, ka), lambda i, j: (i, 0)),
                  pl.BlockSpec((tm, kb), lambda i, j: (i, 0)),
                  pl.BlockSpec((ka, tn), lambda i, j: (0, j)),
                  pl.BlockSpec((kb, tn), lambda i, j: (0, j)),
                  pl.BlockSpec((tm, tn), lambda i, j: (i, j)),
                  pl.BlockSpec((tm, tn), lambda i, j: (i, j + nj))],
        out_specs=pl.BlockSpec((tm, tn), lambda i, j: (i, j)),
        out_shape=jax.ShapeDtypeStruct((m, d), BF16),
        compiler_params=_params(("arbitrary", "arbitrary"),
                                2 * tm * (ka + kb) * 2, 2 * (ka + kb) * tn * 2, 6 * tm * tn * 2, 2 * tm * tn * 4),
        name="merge",
    )(ya, yb, wa, wb, gates, gates)


def _resid_kernel(a_ref, w_ref, x_ref, g_ref, o_ref):
    acc = jnp.dot(a_ref[...], w_ref[...], preferred_element_type=F32)
    o_ref[...] = x_ref[...] + g_ref[...] * acc


def _resid(a, w, x2, mod3, g_idx, seq, *, tn_prefs, single_buffer_a, name):
    m, k = a.shape
    d = w.shape[1]
    tm = _pick(seq, (1024, 512, 256, 128))
    per_b = seq // tm
    tn = _pick(d, tn_prefs)
    nj = d // tn
    a_bufs = 1 if single_buffer_a else 2
    a_spec = (pl.BlockSpec((tm, k), lambda i, j: (i, 0), pipeline_mode=pl.Buffered(1))
              if single_buffer_a else pl.BlockSpec((tm, k), lambda i, j: (i, 0)))
    return pl.pallas_call(
        _resid_kernel,
        grid=(m // tm, nj),
        in_specs=[a_spec,
                  pl.BlockSpec((k, tn), lambda i, j: (0, j)),
                  pl.BlockSpec((tm, tn), lambda i, j: (i, j)),
                  pl.BlockSpec((None, 1, tn), lambda i, j: (i // per_b, 0, g_idx * nj + j))],
        out_specs=pl.BlockSpec((tm, tn), lambda i, j: (i, j)),
        out_shape=jax.ShapeDtypeStruct((m, d), F32),
        compiler_params=pltpu.CompilerParams(
            dimension_semantics=("arbitrary", "arbitrary"),
            vmem_limit_bytes=int(min(VMEM_CAP, a_bufs * tm * k * 2 + 2 * k * tn * 2
                                     + 6 * tm * tn * 4 + (4 << 20)))),
        name=name,
    )(a, w, x2, mod3)


def _shift_rows(a, k, row):
    return jnp.where(row >= k, pltpu.roll(a, k, 0), 0.0)


def _up_kernel(h_ref, wg_ref, wv_ref, cwg_ref, cwv_ref, cbg_ref, cbv_ref, o_ref):
    h = h_ref[...]
    row = lax.broadcasted_iota(jnp.int32, (h.shape[0], wg_ref.shape[1]), 0)

    def conv(w_ref, cw_ref, cb_ref):
        a = jnp.dot(h, w_ref[...], preferred_element_type=F32)
        acc = cb_ref[...] + _shift_rows(a, 2, row) * cw_ref[0:1, :]
        acc = acc + _shift_rows(a, 1, row) * cw_ref[1:2, :]
        return acc + a * cw_ref[2:3, :]

    gate = conv(wg_ref, cwg_ref, cbg_ref)
    val = conv(wv_ref, cwv_ref, cbv_ref)
    o_ref[...] = (gate * jax.nn.sigmoid(gate) * val).astype(o_ref.dtype)


def _up_conv_gate(h2, w_up, conv_w, conv_b, seq):
    m, k = h2.shape
    f = w_up.shape[1] // 2
    tn = _pick(f, (256, 128))
    nj = f // tn
    return pl.pallas_call(
        _up_kernel,
        grid=(m // seq, nj),
        in_specs=[pl.BlockSpec((seq, k), lambda i, j: (i, 0), pipeline_mode=pl.Buffered(1)),
                  pl.BlockSpec((k, tn), lambda i, j: (0, j)),
                  pl.BlockSpec((k, tn), lambda i, j: (0, j + nj)),
                  pl.BlockSpec((CONV_W, tn), lambda i, j: (0, j)),
                  pl.BlockSpec((CONV_W, tn), lambda i, j: (0, j + nj)),
                  pl.BlockSpec((1, tn), lambda i, j: (0, j)),
                  pl.BlockSpec((1, tn), lambda i, j: (0, j + nj))],
        out_specs=pl.BlockSpec((seq, tn), lambda i, j: (i, j)),
        out_shape=jax.ShapeDtypeStruct((m, f), BF16),
        compiler_params=pltpu.CompilerParams(
            dimension_semantics=("arbitrary", "arbitrary"),
            vmem_limit_bytes=int(min(VMEM_CAP, seq * k * 2 + 4 * k * tn * 2 + 2 * seq * tn * 2
                                     + 8 * seq * tn * 4 + (4 << 20)))),
        name="up_conv_gate",
    )(h2, w_up, w_up, conv_w, conv_w, conv_b.reshape(1, 2 * f), conv_b.reshape(1, 2 * f))


def kernel(x, c, ada_w, ada_b, norm1_g, w_in, q_norm_g, k_norm_g, sgu_norm_g, sgu_w, sgu_b,
           w_branch_a, w_branch_b, w_out, norm2_g, w_up, conv_w, conv_b, w_down):
    batch, seq, d = x.shape
    m = batch * seq
    depth = ada_w.shape[0]
    assert seq % GM_CHUNK == 0 and d % V7X_LANES == 0

    x2 = x.reshape(m, d)
    bp = -(-batch // 8) * 8
    c_pad = jnp.pad(c, ((0, bp - batch), (0, 0)))

    o_q = 0
    o_k = o_q + ATT_WIDTH
    o_v = o_k + KV_WIDTH
    o_qi = o_v + KV_WIDTH
    o_ki = o_qi + IDX_HEADS * IDX_DIM
    o_wi = o_ki + IDX_DIM
    o_gu = o_wi + IDX_HEADS
    o_ga = o_gu + 2 * GM_WIDTH
    o_end = o_ga + 2 * d

    for l in range(depth):
        mod = _ada(c_pad, ada_w[l], ada_b[l])
        mod3 = mod.reshape(bp, 1, 6 * d)

        wl = w_in[l]
        w_att = wl[:, o_q:o_ki].astype(BF16)
        zk = jnp.zeros((d, IDX_DIM), F32)
        w_kw = jnp.concatenate(
            [wl[:, o_ki:o_wi], zk, zk, wl[:, o_ki:o_wi], wl[:, o_wi:o_gu],
             jnp.zeros((d, V7X_LANES - IDX_HEADS), F32)], axis=1).astype(BF16)
        w_uv = wl[:, o_gu:o_ga].astype(BF16)
        w_gt = wl[:, o_ga:o_end].astype(BF16)

        h = _norm_mod(x2, norm1_g[l], mod3, 1, 0, seq)
        q = _proj(h, w_att, o_q, ATT_WIDTH, out_dtype=BF16, epilogue="headnorm", gain=q_norm_g[l],
                  post_scale=HEAD_DIM ** -0.5, name="proj_q")
        k = _proj(h, w_att, o_k, KV_WIDTH, out_dtype=BF16, epilogue="headnorm", gain=k_norm_g[l],
                  name="proj_k")
        v = _proj(h, w_att, o_v, KV_WIDTH, out_dtype=BF16, name="proj_v")
        qi = _proj(h, w_att, o_qi, IDX_HEADS * IDX_DIM, out_dtype=BF16, name="proj_qi")
        kw = _proj(h, w_kw, 0, 3 * V7X_LANES, out_dtype=F32, name="proj_kw")
        uv = _proj(h, w_uv, 0, 2 * GM_WIDTH, out_dtype=BF16, epilogue="gelu", name="proj_uv")
        gates = _proj(h, w_gt, 0, 2 * d, out_dtype=BF16, epilogue="sigmoid", name="proj_gates")

        y_a = _attention(q, qi, kw, k, v, batch, seq)
        y_b = _sgu(uv, sgu_norm_g[l], sgu_w[l], sgu_b[l], seq)
        merged = _merge(y_a, y_b, w_branch_a[l].astype(BF16), w_branch_b[l].astype(BF16), gates)
        x2 = _resid(merged, w_out[l].astype(BF16), x2, mod3, 2, seq,
                    tn_prefs=(512, 256, 128), single_buffer_a=False, name="out_proj_resid")

        h2 = _norm_mod(x2, norm2_g[l], mod3, 4, 3, seq)
        act = _up_conv_gate(h2, w_up[l].astype(BF16), conv_w[l], conv_b[l], seq)
        x2 = _resid(act, w_down[l].astype(BF16), x2, mod3, 5, seq,
                    tn_prefs=(256, 128), single_buffer_a=True, name="down_proj_resid")

    return x2.reshape(batch, seq, d)
```

```python
import functools
import math

import jax
import jax.numpy as jnp
from jax import lax
from jax.experimental import pallas as pl
from jax.experimental.pallas import tpu as pltpu

N_HEADS = 16
HEAD_DIM = 128
N_KV_HEADS = 4
ATT_WIDTH = N_HEADS * HEAD_DIM
KV_WIDTH = N_KV_HEADS * HEAD_DIM
IDX_HEADS = 32
IDX_DIM = 64
TOPK_MAX = 256
GM_WIDTH = 2048
GM_GROUPS = 8
GM_GROUP_W = GM_WIDTH // GM_GROUPS
GM_CHUNK = 128
CONV_W = 3
EPS = 1e-6
NEG_BIG = -1e30

V7X_LANES = 128
V7X_VMEM_BYTES = 64 * 1024 * 1024
VMEM_CAP = V7X_VMEM_BYTES - 8 * 1024 * 1024

BF16 = jnp.bfloat16
F32 = jnp.float32
INT_MIN = -(2 ** 31)


def _vmem_limit(*nbytes):
    return int(min(VMEM_CAP, 2 * sum(nbytes) + (4 << 20)))


def _params(semantics, *nbytes):
    return pltpu.CompilerParams(dimension_semantics=semantics, vmem_limit_bytes=_vmem_limit(*nbytes))


def _pick(n, prefs):
    for p in prefs:
        if n % p == 0:
            return p
    return n


def _ada_kernel(c_ref, w_ref, b_ref, o_ref):
    c = c_ref[...]
    cs = c * jax.nn.sigmoid(c)
    o_ref[...] = jnp.dot(cs.astype(BF16), w_ref[...].astype(BF16),
                         preferred_element_type=F32) + b_ref[...]


def _ada(c_pad, ada_w, ada_b, l):
    bp, d = c_pad.shape
    n = ada_w.shape[2]
    tn = _pick(n, (512, 256, 128))
    return pl.pallas_call(
        _ada_kernel,
        grid=(n // tn,),
        in_specs=[pl.BlockSpec((bp, d), lambda j: (0, 0)),
                  pl.BlockSpec((None, d, tn), lambda j: (l, 0, j)),
                  pl.BlockSpec((None, 1, tn), lambda j: (l, 0, j))],
        out_specs=pl.BlockSpec((bp, tn), lambda j: (0, j)),
        out_shape=jax.ShapeDtypeStruct((bp, n), F32),
        compiler_params=_params(("arbitrary",), 2 * d * tn * 4, d * tn * 2),
        name="ada_mod",
    )(c_pad, ada_w, ada_b.reshape(ada_b.shape[0], 1, n))


def _norm_mod_kernel(x_ref, g_ref, sc_ref, sh_ref, o_ref):
    x = x_ref[...]
    y = x * lax.rsqrt(jnp.mean(x * x, axis=-1, keepdims=True) + EPS) * g_ref[...]
    o_ref[...] = (y * (1.0 + sc_ref[...]) + sh_ref[...]).astype(o_ref.dtype)


def _norm_mod(x2, g, mod3, sc_idx, sh_idx, seq):
    m, d = x2.shape
    tm = _pick(seq, (256, 128, 64, 8))
    per_b = seq // tm
    return pl.pallas_call(
        _norm_mod_kernel,
        grid=(m // tm,),
        in_specs=[pl.BlockSpec((tm, d), lambda i: (i, 0)),
                  pl.BlockSpec((1, d), lambda i: (0, 0)),
                  pl.BlockSpec((None, 1, d), lambda i: (i // per_b, 0, sc_idx)),
                  pl.BlockSpec((None, 1, d), lambda i: (i // per_b, 0, sh_idx))],
        out_specs=pl.BlockSpec((tm, d), lambda i: (i, 0)),
        out_shape=jax.ShapeDtypeStruct((m, d), BF16),
        compiler_params=_params(("arbitrary",), 2 * tm * d * 4, 2 * tm * d * 2),
        name="norm_mod",
    )(x2, g.reshape(1, d), mod3, mod3)


def _gelu_exact(x):
    return 0.5 * x * (1.0 + lax.erf(x * (2.0 ** -0.5)))


MXU_ACC_COLS = 512


def _proj_kernel(a_ref, w_ref, *rest, epilogue, post_scale):
    o_ref = rest[-1]
    tn = o_ref.shape[1]
    nsub = min(tn, MXU_ACC_COLS)
    a = a_ref[...]
    for nb in range(tn // nsub):
        cs = slice(nb * nsub, (nb + 1) * nsub)
        acc = jnp.dot(a, w_ref[:, cs], preferred_element_type=F32)
        if epilogue == "plain":
            out = acc
        elif epilogue == "gelu":
            out = _gelu_exact(acc)
        elif epilogue == "sigmoid":
            out = jax.nn.sigmoid(acc)
        elif epilogue == "headnorm":
            gain = rest[0][...]
            parts = []
            for c in range(nsub // HEAD_DIM):
                blk = acc[:, c * HEAD_DIM:(c + 1) * HEAD_DIM]
                ms = jnp.mean(blk * blk, axis=-1, keepdims=True)
                parts.append(blk * lax.rsqrt(ms + EPS) * (gain * post_scale))
            out = jnp.concatenate(parts, axis=1) if len(parts) > 1 else parts[0]
        else:
            raise ValueError(epilogue)
        o_ref[:, cs] = out.astype(o_ref.dtype)


def _proj(a, w, col0, n, *, out_dtype, epilogue="plain", gain=None, post_scale=1.0, name):
    m, k = a.shape
    tm = _pick(m, (1024, 512, 256, 128))
    tn = _pick(math.gcd(n, col0) if col0 else n, (1024, 512, 256, 128))
    j0 = col0 // tn
    in_specs = [pl.BlockSpec((tm, k), lambda i, j: (i, 0)),
                pl.BlockSpec((k, tn), lambda i, j: (0, j + j0))]
    args = [a, w]
    if epilogue == "headnorm":
        in_specs.append(pl.BlockSpec((1, HEAD_DIM), lambda i, j: (0, 0)))
        args.append(gain.reshape(1, HEAD_DIM))
    osz = jnp.dtype(out_dtype).itemsize
    return pl.pallas_call(
        functools.partial(_proj_kernel, epilogue=epilogue, post_scale=post_scale),
        grid=(m // tm, n // tn),
        in_specs=in_specs,
        out_specs=pl.BlockSpec((tm, tn), lambda i, j: (i, j)),
        out_shape=jax.ShapeDtypeStruct((m, n), out_dtype),
        compiler_params=_params(("arbitrary", "arbitrary"),
                                2 * tm * k * 2, 2 * k * tn * 2, 2 * tm * tn * osz, tm * tn * 4),
        name=name,
    )(*args)


def _alibi_slopes(n):
    return [2.0 ** (-8.0 * (i + 1) / n) for i in range(n)]


def _attn_kernel(q_ref, qi_ref, kwq_ref, k_ref, v_ref, kwa_ref, y_ref,
                 keys_scr, bias_scr, wt_scr, sel_scr, *, tq, seq, topk):
    i = pl.program_id(1)
    n_chunks = i + 1
    t0 = i * tq
    rep = N_HEADS // N_KV_HEADS
    nt = (((1,), (1,)), ((), ()))
    row = lax.broadcasted_iota(jnp.int32, (tq, tq), 0)
    col = lax.broadcasted_iota(jnp.int32, (tq, tq), 1)
    kf = float(topk)

    w_fold = (IDX_HEADS ** -0.5) * (IDX_DIM ** -0.5)
    wt_scr[...] = (kwq_ref[:, 2 * V7X_LANES:3 * V7X_LANES] * w_fold).T

    def score_body(c, carry):
        off = pl.multiple_of(c * tq, tq)
        kraw = kwa_ref[pl.ds(off, tq), 0:2 * V7X_LANES].astype(BF16)
        lhs = jnp.concatenate([kraw[:, :V7X_LANES], kraw[:, V7X_LANES:]], axis=0)
        acc = jnp.zeros((tq, tq), F32)
        for p in range(IDX_HEADS // 2):
            lg = lax.dot_general(lhs, qi_ref[:, p * V7X_LANES:(p + 1) * V7X_LANES], nt,
                                 preferred_element_type=F32)
            acc = acc + wt_scr[2 * p:2 * p + 1, :] * jnp.maximum(lg[:tq], 0.0)
            acc = acc + wt_scr[2 * p + 1:2 * p + 2, :] * jnp.maximum(lg[tq:], 0.0)
        bits = lax.bitcast_convert_type(acc, jnp.int32)
        key = bits ^ ((bits >> 31) & jnp.int32(0x7FFFFFFF))
        keys_scr[c] = jnp.where(off + row <= t0 + col, key, jnp.int32(INT_MIN))
        return carry

    lax.fori_loop(0, n_chunks, score_body, 0)

    def count(pred):
        def body(c, cnt):
            off = c * tq
            return cnt + jnp.sum(jnp.where(pred(keys_scr[c], off + row), 1.0, 0.0),
                                 axis=0, keepdims=True)
        return lax.fori_loop(0, n_chunks, body, jnp.zeros((1, tq), F32))

    def select_topk():
        cnt_nonneg = count(lambda kb, s: kb >= 0)
        prefix0 = jnp.where(cnt_nonneg >= kf, jnp.int32(0), jnp.int32(INT_MIN))

        def bit_body(j, prefix):
            cand = prefix | lax.shift_left(jnp.int32(1), 30 - j)
            cnt = count(lambda kb, s: kb >= cand)
            return jnp.where(cnt >= kf, cand, prefix)

        thr = lax.fori_loop(0, 31, bit_body, prefix0)
        cnt_gt = count(lambda kb, s: kb > thr)
        cnt_ge = count(lambda kb, s: kb >= thr)
        need = kf - cnt_gt
        sel_scr[0:1, :] = thr
        sel_scr[1:2, :] = jnp.full((1, tq), seq, jnp.int32)

        @pl.when(jnp.max(cnt_ge) > kf)
        def _():
            nbits = max(1, (seq - 1).bit_length())

            def idx_body(j, m):
                cand = m | lax.shift_left(jnp.int32(1), nbits - 1 - j)
                cnt = count(lambda kb, s: (kb == thr) & (s < cand))
                return jnp.where(cnt < need, cand, m)

            sel_scr[1:2, :] = lax.fori_loop(0, nbits, idx_body, jnp.zeros((1, tq), jnp.int32))

    if tq <= topk:
        @pl.when(i == 0)
        def _():
            sel_scr[0:1, :] = jnp.full((1, tq), INT_MIN, jnp.int32)
            sel_scr[1:2, :] = jnp.full((1, tq), -1, jnp.int32)

        pl.when(i > 0)(select_topk)
    else:
        select_topk()

    thr = sel_scr[0:1, :]
    m_idx = sel_scr[1:2, :]

    def bias_body(c, carry):
        off = c * tq
        kb = keys_scr[c]
        s_idx = off + row
        take = jnp.where(kb > thr, 1.0, jnp.where((kb == thr) & (s_idx <= m_idx), 1.0, 0.0))
        take = jnp.where(s_idx <= t0 + col, take, 0.0)
        bias_scr[c] = jnp.where(take > 0.5, 0.0, NEG_BIG).T
        return carry

    lax.fori_loop(0, n_chunks, bias_body, 0)

    slopes = _alibi_slopes(N_HEADS)
    d0 = (row - col).astype(F32)

    for g in range(N_KV_HEADS):
        qg = jnp.concatenate(
            [q_ref[:, (g * rep + r) * HEAD_DIM:(g * rep + r + 1) * HEAD_DIM] for r in range(rep)], axis=0)

        def attn_body(c, carry, g=g, qg=qg):
            m_i, l_i, acc = carry
            off = pl.multiple_of(c * tq, tq)
            kc = k_ref[pl.ds(off, tq), g * HEAD_DIM:(g + 1) * HEAD_DIM]
            vc = v_ref[pl.ds(off, tq), g * HEAD_DIM:(g + 1) * HEAD_DIM]
            s = lax.dot_general(qg, kc, nt, preferred_element_type=F32)
            dist = d0 + (t0 - off).astype(F32)
            bias = bias_scr[c]
            s = jnp.concatenate(
                [s[r * tq:(r + 1) * tq] - slopes[g * rep + r] * dist + bias for r in range(rep)], axis=0)
            m_new = jnp.maximum(m_i, jnp.max(s, axis=-1, keepdims=True))
            alpha = jnp.exp(m_i - m_new)
            p = jnp.exp(s - m_new)
            l_new = alpha * l_i + jnp.sum(p, axis=-1, keepdims=True)
            acc_new = alpha * acc + jnp.dot(p.astype(BF16), vc, preferred_element_type=F32)
            return m_new, l_new, acc_new

        init = (jnp.full((rep * tq, 1), NEG_BIG, F32), jnp.zeros((rep * tq, 1), F32),
                jnp.zeros((rep * tq, HEAD_DIM), F32))
        _, l_f, acc_f = lax.fori_loop(0, n_chunks, attn_body, init)
        o = acc_f / l_f
        for r in range(rep):
            h = g * rep + r
            y_ref[:, h * HEAD_DIM:(h + 1) * HEAD_DIM] = o[r * tq:(r + 1) * tq].astype(y_ref.dtype)


def _attention(q, qi, kw, k, v, batch, seq):
    m = q.shape[0]
    topk = min(TOPK_MAX, seq // 4)
    tq = _pick(seq, (256, 128))
    nq = seq // tq
    kww = kw.shape[1]
    return pl.pallas_call(
        functools.partial(_attn_kernel, tq=tq, seq=seq, topk=topk),
        grid=(batch, nq),
        in_specs=[pl.BlockSpec((tq, ATT_WIDTH), lambda b, i: (b * nq + i, 0)),
                  pl.BlockSpec((tq, IDX_HEADS * IDX_DIM), lambda b, i: (b * nq + i, 0)),
                  pl.BlockSpec((tq, kww), lambda b, i: (b * nq + i, 0)),
                  pl.BlockSpec((seq, KV_WIDTH), lambda b, i: (b, 0)),
                  pl.BlockSpec((seq, KV_WIDTH), lambda b, i: (b, 0)),
                  pl.BlockSpec((seq, kww), lambda b, i: (b, 0))],
        out_specs=pl.BlockSpec((tq, ATT_WIDTH), lambda b, i: (b * nq + i, 0)),
        out_shape=jax.ShapeDtypeStruct((m, ATT_WIDTH), BF16),
        scratch_shapes=[pltpu.VMEM((nq, tq, tq), jnp.int32),
                        pltpu.VMEM((nq, tq, tq), F32),
                        pltpu.VMEM((V7X_LANES, tq), F32),
                        pltpu.VMEM((8, tq), jnp.int32)],
        compiler_params=_params(("arbitrary", "arbitrary"),
                                6 * tq * ATT_WIDTH * 2, 4 * seq * KV_WIDTH * 2, 2 * (seq + tq) * kww * 4,
                                2 * seq * tq * 4, 8 * tq * tq * 4),
        name="dsa_attention",
    )(q, qi, kw, k, v, kw)


def _sgu_kernel(u_ref, v_ref, g_ref, w_ref, bt_ref, o_ref, *, rows):
    v = v_ref[...].astype(F32)
    vn = (v * lax.rsqrt(jnp.mean(v * v, axis=-1, keepdims=True) + EPS) * g_ref[...]).astype(BF16)
    r_i = lax.broadcasted_iota(jnp.int32, (GM_CHUNK, GM_CHUNK), 0)
    c_i = lax.broadcasted_iota(jnp.int32, (GM_CHUNK, GM_CHUNK), 1)
    for g in range(GM_GROUPS):
        w = jnp.where(r_i >= c_i, w_ref[g], 0.0).astype(BF16)
        bcol = bt_ref[:, g:g + 1]
        cs = slice(g * GM_GROUP_W, (g + 1) * GM_GROUP_W)
        for n in range(rows // GM_CHUNK):
            rs = slice(n * GM_CHUNK, (n + 1) * GM_CHUNK)
            f = jnp.dot(w, vn[rs, cs], preferred_element_type=F32) + bcol
            o_ref[rs, cs] = (u_ref[rs, cs].astype(F32) * f).astype(o_ref.dtype)


def _sgu(uv, gain, w_s, b_s, seq):
    m = uv.shape[0]
    rows = _pick(seq, (512, 256, 128))
    return pl.pallas_call(
        functools.partial(_sgu_kernel, rows=rows),
        grid=(m // rows,),
        in_specs=[pl.BlockSpec((rows, GM_WIDTH), lambda i: (i, 0)),
                  pl.BlockSpec((rows, GM_WIDTH), lambda i: (i, 1)),
                  pl.BlockSpec((1, GM_WIDTH), lambda i: (0, 0)),
                  pl.BlockSpec((GM_GROUPS, GM_CHUNK, GM_CHUNK), lambda i: (0, 0, 0)),
                  pl.BlockSpec((GM_CHUNK, GM_GROUPS), lambda i: (0, 0))],
        out_specs=pl.BlockSpec((rows, GM_WIDTH), lambda i: (i, 0)),
        out_shape=jax.ShapeDtypeStruct((m, GM_WIDTH), BF16),
        compiler_params=_params(("arbitrary",), 6 * rows * GM_WIDTH * 2, 2 * rows * GM_WIDTH * 4),
        name="sgu",
    )(uv, uv, gain.reshape(1, GM_WIDTH), w_s, b_s.T)


def _w_spec(w, l, k, tn, j0=0):
    if w.ndim == 3:
        return pl.BlockSpec((None, k, tn), lambda i, j: (l, 0, j + j0))
    return pl.BlockSpec((k, tn), lambda i, j: (0, j + j0))


def _merge_kernel(ya_ref, yb_ref, wa_ref, wb_ref, ga_ref, gb_ref, o_ref):
    pa = jnp.dot(ya_ref[...], wa_ref[...].astype(BF16), preferred_element_type=F32)
    pb = jnp.dot(yb_ref[...], wb_ref[...].astype(BF16), preferred_element_type=F32)
    o_ref[...] = (ga_ref[...].astype(F32) * pa + gb_ref[...].astype(F32) * pb).astype(o_ref.dtype)


def _merge(ya, yb, wa, wb, l, gates):
    m, ka = ya.shape
    kb = yb.shape[1]
    d = wa.shape[-1]
    tm = _pick(m, (1024, 512, 256, 128))
    tn = _pick(d, (512, 256, 128))
    nj = d // tn
    wsz = wa.dtype.itemsize
    return pl.pallas_call(
        _merge_kernel,
        grid=(m // tm, nj),
        in_specs=[pl.BlockSpec((tm, ka), lambda i, j: (i, 0)),
                  pl.BlockSpec((tm, kb), lambda i, j: (i, 0)),
                  _w_spec(wa, l, ka, tn),
                  _w_spec(wb, l, kb, tn),
                  pl.BlockSpec((tm, tn), lambda i, j: (i, j)),
                  pl.BlockSpec((tm, tn), lambda i, j: (i, j + nj))],
        out_specs=pl.BlockSpec((tm, tn), lambda i, j: (i, j)),
        out_shape=jax.ShapeDtypeStruct((m, d), BF16),
        compiler_params=pltpu.CompilerParams(
            dimension_semantics=("arbitrary", "arbitrary"),
            vmem_limit_bytes=int(min(VMEM_CAP, 2 * tm * (ka + kb) * 2 + (ka + kb) * tn * (2 * wsz + 2)
                                     + 6 * tm * tn * 2 + 3 * tm * tn * 4 + (4 << 20)))),
        name="merge",
    )(ya, yb, wa, wb, gates, gates)


def _resid_kernel(a_ref, w_ref, x_ref, g_ref, o_ref):
    acc = jnp.dot(a_ref[...], w_ref[...].astype(BF16), preferred_element_type=F32)
    o_ref[...] = x_ref[...] + g_ref[...] * acc


def _resid(a, w, l, x2, mod3, g_idx, seq, *, tn_prefs, single_buffer_a, name):
    m, k = a.shape
    d = w.shape[-1]
    tm = _pick(seq, (1024, 512, 256, 128))
    per_b = seq // tm
    tn = _pick(d, tn_prefs)
    nj = d // tn
    a_bufs = 1 if single_buffer_a else 2
    wsz = w.dtype.itemsize
    a_spec = (pl.BlockSpec((tm, k), lambda i, j: (i, 0), pipeline_mode=pl.Buffered(1))
              if single_buffer_a else pl.BlockSpec((tm, k), lambda i, j: (i, 0)))
    return pl.pallas_call(
        _resid_kernel,
        grid=(m // tm, nj),
        in_specs=[a_spec,
                  _w_spec(w, l, k, tn),
                  pl.BlockSpec((tm, tn), lambda i, j: (i, j)),
                  pl.BlockSpec((None, 1, tn), lambda i, j: (i // per_b, 0, g_idx * nj + j))],
        out_specs=pl.BlockSpec((tm, tn), lambda i, j: (i, j)),
        out_shape=jax.ShapeDtypeStruct((m, d), F32),
        compiler_params=pltpu.CompilerParams(
            dimension_semantics=("arbitrary", "arbitrary"),
            vmem_limit_bytes=int(min(VMEM_CAP, a_bufs * tm * k * 2 + k * tn * (2 * wsz + 2)
                                     + 6 * tm * tn * 4 + (4 << 20)))),
        name=name,
    )(a, w, x2, mod3)


HALO = 8


def _up_kernel(h_ref, wg_ref, wv_ref, cwg_ref, cwv_ref, cbg_ref, cbv_ref, o_ref, *, sub):
    wg = wg_ref[...].astype(BF16)
    wv = wv_ref[...].astype(BF16)
    tn = wg.shape[1]

    def conv(a, halo, cw_ref, cb_ref):
        ext = jnp.concatenate([halo, a], axis=0)
        acc = cb_ref[...] + pltpu.roll(ext, 2, 0)[HALO:] * cw_ref[0:1, :]
        acc = acc + pltpu.roll(ext, 1, 0)[HALO:] * cw_ref[1:2, :]
        return acc + a * cw_ref[2:3, :]

    halo_g = halo_v = jnp.zeros((HALO, tn), F32)
    for s in range(h_ref.shape[0] // sub):
        rs = slice(s * sub, (s + 1) * sub)
        hs = h_ref[rs, :]
        ag = jnp.dot(hs, wg, preferred_element_type=F32)
        av = jnp.dot(hs, wv, preferred_element_type=F32)
        gate = conv(ag, halo_g, cwg_ref, cbg_ref)
        val = conv(av, halo_v, cwv_ref, cbv_ref)
        o_ref[rs, :] = (gate * jax.nn.sigmoid(gate) * val).astype(o_ref.dtype)
        halo_g, halo_v = ag[sub - HALO:], av[sub - HALO:]


def _up_conv_gate(h2, w_up, l, conv_w, conv_b, seq):
    m, k = h2.shape
    f = w_up.shape[-1] // 2
    tn = _pick(f, (256, 128))
    nj = f // tn
    sub = _pick(seq, (512, 256, 128))
    wsz = w_up.dtype.itemsize
    return pl.pallas_call(
        functools.partial(_up_kernel, sub=sub),
        grid=(m // seq, nj),
        in_specs=[pl.BlockSpec((seq, k), lambda i, j: (i, 0), pipeline_mode=pl.Buffered(1)),
                  _w_spec(w_up, l, k, tn),
                  _w_spec(w_up, l, k, tn, nj),
                  pl.BlockSpec((None, CONV_W, tn), lambda i, j: (l, 0, j)),
                  pl.BlockSpec((None, CONV_W, tn), lambda i, j: (l, 0, j + nj)),
                  pl.BlockSpec((None, 1, tn), lambda i, j: (l, 0, j)),
                  pl.BlockSpec((None, 1, tn), lambda i, j: (l, 0, j + nj))],
        out_specs=pl.BlockSpec((seq, tn), lambda i, j: (i, j)),
        out_shape=jax.ShapeDtypeStruct((m, f), BF16),
        compiler_params=pltpu.CompilerParams(
            dimension_semantics=("arbitrary", "arbitrary"),
            vmem_limit_bytes=int(min(VMEM_CAP, seq * k * 2 + 2 * k * tn * (2 * wsz + 2) + 2 * seq * tn * 2
                                     + 12 * sub * tn * 4 + (4 << 20)))),
        name="up_conv_gate",
    )(h2, w_up, w_up, conv_w, conv_w, conv_b.reshape(conv_b.shape[0], 1, 2 * f),
      conv_b.reshape(conv_b.shape[0], 1, 2 * f))


def kernel(x, c, ada_w, ada_b, norm1_g, w_in, q_norm_g, k_norm_g, sgu_norm_g, sgu_w, sgu_b,
           w_branch_a, w_branch_b, w_out, norm2_g, w_up, conv_w, conv_b, w_down):
    batch, seq, d = x.shape
    m = batch * seq
    depth = ada_w.shape[0]
    assert seq % GM_CHUNK == 0 and d % V7X_LANES == 0

    x2 = x.reshape(m, d)
    bp = -(-batch // 8) * 8
    c_pad = jnp.pad(c, ((0, bp - batch), (0, 0)))

    o_q = 0
    o_k = o_q + ATT_WIDTH
    o_v = o_k + KV_WIDTH
    o_qi = o_v + KV_WIDTH
    o_ki = o_qi + IDX_HEADS * IDX_DIM
    o_wi = o_ki + IDX_DIM
    o_gu = o_wi + IDX_HEADS
    o_ga = o_gu + 2 * GM_WIDTH
    o_end = o_ga + 2 * d

    kw_width = 4 * V7X_LANES
    p_kw = o_ki
    p_uv = -(-(p_kw + kw_width) // 1024) * 1024
    p_gt = p_uv + 2 * GM_WIDTH

    for l in range(depth):
        mod = _ada(c_pad, ada_w, ada_b, l)
        mod3 = mod.reshape(bp, 1, 6 * d)

        wl = w_in[l]
        zk = jnp.zeros((d, IDX_DIM), F32)
        w_pk = jnp.concatenate(
            [wl[:, o_q:o_ki],
             wl[:, o_ki:o_wi], zk, zk, wl[:, o_ki:o_wi], wl[:, o_wi:o_gu],
             jnp.zeros((d, p_uv - p_kw - 2 * V7X_LANES - IDX_HEADS), F32),
             wl[:, o_gu:o_end]], axis=1).astype(BF16)

        h = _norm_mod(x2, norm1_g[l], mod3, 1, 0, seq)
        q = _proj(h, w_pk, o_q, ATT_WIDTH, out_dtype=BF16, epilogue="headnorm", gain=q_norm_g[l],
                  post_scale=HEAD_DIM ** -0.5, name="proj_q")
        k = _proj(h, w_pk, o_k, KV_WIDTH, out_dtype=BF16, epilogue="headnorm", gain=k_norm_g[l],
                  name="proj_k")
        v = _proj(h, w_pk, o_v, KV_WIDTH, out_dtype=BF16, name="proj_v")
        qi = _proj(h, w_pk, o_qi, IDX_HEADS * IDX_DIM, out_dtype=BF16, name="proj_qi")
        kw = _proj(h, w_pk, p_kw, kw_width, out_dtype=F32, name="proj_kw")
        uv = _proj(h, w_pk, p_uv, 2 * GM_WIDTH, out_dtype=BF16, epilogue="gelu", name="proj_uv")
        gates = _proj(h, w_pk, p_gt, 2 * d, out_dtype=BF16, epilogue="sigmoid", name="proj_gates")

        y_a = _attention(q, qi, kw, k, v, batch, seq)
        y_b = _sgu(uv, sgu_norm_g[l], sgu_w[l], sgu_b[l], seq)
        merged = _merge(y_a, y_b, w_branch_a, w_branch_b, l, gates)
        x2 = _resid(merged, w_out, l, x2, mod3, 2, seq,
                    tn_prefs=(512, 256, 128), single_buffer_a=False, name="out_proj_resid")

        h2 = _norm_mod(x2, norm2_g[l], mod3, 4, 3, seq)
        act = _up_conv_gate(h2, w_up, l, conv_w, conv_b, seq)
        x2 = _resid(act, w_down[l].astype(BF16), l, x2, mod3, 5, seq,
                    tn_prefs=(256, 128), single_buffer_a=True, name="down_proj_resid")

    return x2.reshape(batch, seq, d)
```

```python
import functools
import math

import jax
import jax.numpy as jnp
from jax import lax
from jax.experimental import pallas as pl
from jax.experimental.pallas import tpu as pltpu

N_HEADS = 16
HEAD_DIM = 128
N_KV_HEADS = 4
ATT_WIDTH = N_HEADS * HEAD_DIM
KV_WIDTH = N_KV_HEADS * HEAD_DIM
IDX_HEADS = 32
IDX_DIM = 64
TOPK_MAX = 256
GM_WIDTH = 2048
GM_GROUPS = 8
GM_GROUP_W = GM_WIDTH // GM_GROUPS
GM_CHUNK = 128
CONV_W = 3
EPS = 1e-6
NEG_BIG = -1e30

V7X_LANES = 128
V7X_VMEM_BYTES = 64 * 1024 * 1024
VMEM_CAP = V7X_VMEM_BYTES - 8 * 1024 * 1024

BF16 = jnp.bfloat16
F32 = jnp.float32
INT_MIN = -(2 ** 31)


def _vmem_limit(*nbytes):
    return int(min(VMEM_CAP, 2 * sum(nbytes) + (4 << 20)))


def _params(semantics, *nbytes):
    return pltpu.CompilerParams(dimension_semantics=semantics, vmem_limit_bytes=_vmem_limit(*nbytes))


def _pick(n, prefs):
    for p in prefs:
        if n % p == 0:
            return p
    return n


def _ada_kernel(c_ref, w_ref, b_ref, o_ref):
    c = c_ref[...]
    cs = c * jax.nn.sigmoid(c)
    o_ref[...] = jnp.dot(cs.astype(BF16), w_ref[...].astype(BF16),
                         preferred_element_type=F32) + b_ref[...]


def _ada(c_pad, ada_w, ada_b, l):
    bp, d = c_pad.shape
    n = ada_w.shape[2]
    tn = _pick(n, (512, 256, 128))
    return pl.pallas_call(
        _ada_kernel,
        grid=(n // tn,),
        in_specs=[pl.BlockSpec((bp, d), lambda j: (0, 0)),
                  pl.BlockSpec((None, d, tn), lambda j: (l, 0, j)),
                  pl.BlockSpec((None, 1, tn), lambda j: (l, 0, j))],
        out_specs=pl.BlockSpec((bp, tn), lambda j: (0, j)),
        out_shape=jax.ShapeDtypeStruct((bp, n), F32),
        compiler_params=_params(("arbitrary",), 2 * d * tn * 4, d * tn * 2),
        name="ada_mod",
    )(c_pad, ada_w, ada_b.reshape(ada_b.shape[0], 1, n))


def _norm_mod_kernel(x_ref, g_ref, sc_ref, sh_ref, o_ref):
    x = x_ref[...]
    y = x * lax.rsqrt(jnp.mean(x * x, axis=-1, keepdims=True) + EPS) * g_ref[...]
    o_ref[...] = (y * (1.0 + sc_ref[...]) + sh_ref[...]).astype(o_ref.dtype)


def _norm_mod(x2, g, mod3, sc_idx, sh_idx, seq):
    m, d = x2.shape
    tm = _pick(seq, (256, 128, 64, 8))
    per_b = seq // tm
    return pl.pallas_call(
        _norm_mod_kernel,
        grid=(m // tm,),
        in_specs=[pl.BlockSpec((tm, d), lambda i: (i, 0)),
                  pl.BlockSpec((1, d), lambda i: (0, 0)),
                  pl.BlockSpec((None, 1, d), lambda i: (i // per_b, 0, sc_idx)),
                  pl.BlockSpec((None, 1, d), lambda i: (i // per_b, 0, sh_idx))],
        out_specs=pl.BlockSpec((tm, d), lambda i: (i, 0)),
        out_shape=jax.ShapeDtypeStruct((m, d), BF16),
        compiler_params=_params(("arbitrary",), 2 * tm * d * 4, 2 * tm * d * 2),
        name="norm_mod",
    )(x2, g.reshape(1, d), mod3, mod3)


def _pack_kernel(w_ref, o_ref, *, o_ki, o_gu, o_end, p_uv):
    rows = w_ref.shape[0]
    lane = lax.broadcasted_iota(jnp.int32, (rows, V7X_LANES), 1)
    o_ref[:, :o_ki] = w_ref[:, :o_ki].astype(BF16)
    s0 = w_ref[:, o_ki:o_ki + V7X_LANES]
    r = pltpu.roll(s0, IDX_DIM, 1)
    o_ref[:, o_ki:o_ki + V7X_LANES] = jnp.where(lane < IDX_DIM, s0, 0.0).astype(BF16)
    o_ref[:, o_ki + V7X_LANES:o_ki + 2 * V7X_LANES] = jnp.where(lane >= IDX_DIM, r, 0.0).astype(BF16)
    o_ref[:, o_ki + 2 * V7X_LANES:o_ki + 3 * V7X_LANES] = jnp.where(lane < IDX_HEADS, r, 0.0).astype(BF16)
    o_ref[:, o_ki + 3 * V7X_LANES:p_uv] = jnp.zeros((rows, p_uv - o_ki - 3 * V7X_LANES), BF16)
    o_ref[:, p_uv:] = w_ref[:, o_gu:o_end].astype(BF16)


def _pack_w_in(w_in, l, *, o_ki, o_gu, o_end, p_uv):
    _, d, width = w_in.shape
    assert o_gu - o_ki == IDX_DIM + IDX_HEADS and 2 * IDX_DIM == V7X_LANES and o_ki % V7X_LANES == 0
    p_width = p_uv + (o_end - o_gu)
    rows = _pick(d, (128, 64, 8))
    return pl.pallas_call(
        functools.partial(_pack_kernel, o_ki=o_ki, o_gu=o_gu, o_end=o_end, p_uv=p_uv),
        grid=(d // rows,),
        in_specs=[pl.BlockSpec((None, rows, width), lambda i: (l, i, 0))],
        out_specs=pl.BlockSpec((rows, p_width), lambda i: (i, 0)),
        out_shape=jax.ShapeDtypeStruct((d, p_width), BF16),
        compiler_params=_params(("arbitrary",), 2 * rows * width * 4, 2 * rows * p_width * 2),
        name="pack_w_in",
    )(w_in)


def _gelu_exact(x):
    return 0.5 * x * (1.0 + lax.erf(x * (2.0 ** -0.5)))


MXU_ACC_COLS = 512


def _proj_kernel(a_ref, w_ref, *rest, epilogue, post_scale):
    o_ref = rest[-1]
    tn = o_ref.shape[1]
    nsub = min(tn, MXU_ACC_COLS)
    a = a_ref[...]
    for nb in range(tn // nsub):
        cs = slice(nb * nsub, (nb + 1) * nsub)
        acc = jnp.dot(a, w_ref[:, cs], preferred_element_type=F32)
        if epilogue == "plain":
            out = acc
        elif epilogue == "gelu":
            out = _gelu_exact(acc)
        elif epilogue == "sigmoid":
            out = jax.nn.sigmoid(acc)
        elif epilogue == "headnorm":
            gain = rest[0][...]
            parts = []
            for c in range(nsub // HEAD_DIM):
                blk = acc[:, c * HEAD_DIM:(c + 1) * HEAD_DIM]
                ms = jnp.mean(blk * blk, axis=-1, keepdims=True)
                parts.append(blk * lax.rsqrt(ms + EPS) * (gain * post_scale))
            out = jnp.concatenate(parts, axis=1) if len(parts) > 1 else parts[0]
        else:
            raise ValueError(epilogue)
        o_ref[:, cs] = out.astype(o_ref.dtype)


def _proj(a, w, col0, n, *, out_dtype, epilogue="plain", gain=None, post_scale=1.0, name):
    m, k = a.shape
    tm = _pick(m, (1024, 512, 256, 128))
    tn = _pick(math.gcd(n, col0) if col0 else n, (1024, 512, 256, 128))
    j0 = col0 // tn
    in_specs = [pl.BlockSpec((tm, k), lambda i, j: (i, 0)),
                pl.BlockSpec((k, tn), lambda i, j: (0, j + j0))]
    args = [a, w]
    if epilogue == "headnorm":
        in_specs.append(pl.BlockSpec((1, HEAD_DIM), lambda i, j: (0, 0)))
        args.append(gain.reshape(1, HEAD_DIM))
    osz = jnp.dtype(out_dtype).itemsize
    return pl.pallas_call(
        functools.partial(_proj_kernel, epilogue=epilogue, post_scale=post_scale),
        grid=(m // tm, n // tn),
        in_specs=in_specs,
        out_specs=pl.BlockSpec((tm, tn), lambda i, j: (i, j)),
        out_shape=jax.ShapeDtypeStruct((m, n), out_dtype),
        compiler_params=_params(("arbitrary", "arbitrary"),
                                2 * tm * k * 2, 2 * k * tn * 2, 2 * tm * tn * osz, tm * tn * 4),
        name=name,
    )(*args)


def _alibi_slopes(n):
    return [2.0 ** (-8.0 * (i + 1) / n) for i in range(n)]


def _attn_kernel(q_ref, qi_ref, kwq_ref, k_ref, v_ref, kwa_ref, y_ref,
                 keys_scr, bias_scr, wt_scr, sel_scr, m_scr, l_scr, acc_scr, *, tq, seq, topk):
    i = pl.program_id(1)
    n_chunks = i + 1
    t0 = i * tq
    rep = N_HEADS // N_KV_HEADS
    nt = (((1,), (1,)), ((), ()))
    row = lax.broadcasted_iota(jnp.int32, (tq, tq), 0)
    col = lax.broadcasted_iota(jnp.int32, (tq, tq), 1)
    kf = float(topk)

    w_fold = (IDX_HEADS ** -0.5) * (IDX_DIM ** -0.5)
    wt_scr[...] = (kwq_ref[:, 2 * V7X_LANES:3 * V7X_LANES] * w_fold).T

    def score_body(c, carry):
        off = pl.multiple_of(c * tq, tq)
        kraw = kwa_ref[pl.ds(off, tq), 0:2 * V7X_LANES].astype(BF16)
        lhs = jnp.concatenate([kraw[:, :V7X_LANES], kraw[:, V7X_LANES:]], axis=0)
        acc = jnp.zeros((tq, tq), F32)
        for p in range(IDX_HEADS // 2):
            lg = lax.dot_general(lhs, qi_ref[:, p * V7X_LANES:(p + 1) * V7X_LANES], nt,
                                 preferred_element_type=F32)
            acc = acc + wt_scr[2 * p:2 * p + 1, :] * jnp.maximum(lg[:tq], 0.0)
            acc = acc + wt_scr[2 * p + 1:2 * p + 2, :] * jnp.maximum(lg[tq:], 0.0)
        bits = lax.bitcast_convert_type(acc, jnp.int32)
        key = bits ^ ((bits >> 31) & jnp.int32(0x7FFFFFFF))
        keys_scr[c] = jnp.where(off + row <= t0 + col, key, jnp.int32(INT_MIN))
        return carry

    lax.fori_loop(0, n_chunks, score_body, 0)

    def count(pred):
        def body(c, cnt):
            off = c * tq
            return cnt + jnp.sum(jnp.where(pred(keys_scr[c], off + row), 1.0, 0.0),
                                 axis=0, keepdims=True)
        return lax.fori_loop(0, n_chunks, body, jnp.zeros((1, tq), F32))

    def select_topk():
        cnt_nonneg = count(lambda kb, s: kb >= 0)
        prefix0 = jnp.where(cnt_nonneg >= kf, jnp.int32(0), jnp.int32(INT_MIN))

        def bit_body(j, prefix):
            cand = prefix | lax.shift_left(jnp.int32(1), 30 - j)
            cnt = count(lambda kb, s: kb >= cand)
            return jnp.where(cnt >= kf, cand, prefix)

        thr = lax.fori_loop(0, 31, bit_body, prefix0)
        cnt_gt = count(lambda kb, s: kb > thr)
        cnt_ge = count(lambda kb, s: kb >= thr)
        need = kf - cnt_gt
        sel_scr[0:1, :] = thr
        sel_scr[1:2, :] = jnp.full((1, tq), seq, jnp.int32)

        @pl.when(jnp.max(cnt_ge) > kf)
        def _():
            nbits = max(1, (seq - 1).bit_length())

            def idx_body(j, m):
                cand = m | lax.shift_left(jnp.int32(1), nbits - 1 - j)
                cnt = count(lambda kb, s: (kb == thr) & (s < cand))
                return jnp.where(cnt < need, cand, m)

            sel_scr[1:2, :] = lax.fori_loop(0, nbits, idx_body, jnp.zeros((1, tq), jnp.int32))

    if tq <= topk:
        @pl.when(i == 0)
        def _():
            sel_scr[0:1, :] = jnp.full((1, tq), INT_MIN, jnp.int32)
            sel_scr[1:2, :] = jnp.full((1, tq), -1, jnp.int32)

        pl.when(i > 0)(select_topk)
    else:
        select_topk()

    thr = sel_scr[0:1, :]
    m_idx = sel_scr[1:2, :]

    def bias_body(c, carry):
        off = c * tq
        kb = keys_scr[c]
        s_idx = off + row
        take = jnp.where(kb > thr, 1.0, jnp.where((kb == thr) & (s_idx <= m_idx), 1.0, 0.0))
        take = jnp.where(s_idx <= t0 + col, take, 0.0)
        bias_scr[c] = jnp.where(take > 0.5, 0.0, NEG_BIG).T
        return carry

    lax.fori_loop(0, n_chunks, bias_body, 0)

    slopes = _alibi_slopes(N_HEADS)
    d0 = (row - col).astype(F32)

    for g in range(N_KV_HEADS):
        qg = jnp.concatenate(
            [q_ref[:, (g * rep + r) * HEAD_DIM:(g * rep + r + 1) * HEAD_DIM] for r in range(rep)], axis=0)

        m_scr[...] = jnp.full(m_scr.shape, NEG_BIG, F32)
        l_scr[...] = jnp.zeros(l_scr.shape, F32)
        acc_scr[...] = jnp.zeros(acc_scr.shape, F32)

        def attn_body(c, carry, g=g, qg=qg):
            off = pl.multiple_of(c * tq, tq)
            kc = k_ref[pl.ds(off, tq), g * HEAD_DIM:(g + 1) * HEAD_DIM]
            vc = v_ref[pl.ds(off, tq), g * HEAD_DIM:(g + 1) * HEAD_DIM]
            s = lax.dot_general(qg, kc, nt, preferred_element_type=F32)
            dist = d0 + (t0 - off).astype(F32)
            bias = bias_scr[c]
            for r in range(rep):
                rows = slice(r * tq, (r + 1) * tq)
                s_r = s[rows] - slopes[g * rep + r] * dist + bias
                m_prev = m_scr[rows]
                m_new = jnp.maximum(m_prev, jnp.max(s_r, axis=-1, keepdims=True))
                alpha = jnp.exp(m_prev - m_new)
                p = jnp.exp(s_r - pltpu.repeat(m_new, tq // V7X_LANES, axis=1))
                l_scr[rows] = alpha * l_scr[rows] + jnp.sum(p, axis=-1, keepdims=True)
                m_scr[rows] = m_new
                acc_scr[rows] = alpha * acc_scr[rows] + jnp.dot(p.astype(BF16), vc,
                                                                preferred_element_type=F32)
            return carry

        lax.fori_loop(0, n_chunks, attn_body, 0)
        for r in range(rep):
            h = g * rep + r
            rows = slice(r * tq, (r + 1) * tq)
            y_ref[:, h * HEAD_DIM:(h + 1) * HEAD_DIM] = (acc_scr[rows] / l_scr[rows]).astype(y_ref.dtype)


def _attention(q, qi, kw, k, v, batch, seq):
    m = q.shape[0]
    topk = min(TOPK_MAX, seq // 4)
    tq = _pick(seq, (256, 128))
    nq = seq // tq
    kww = kw.shape[1]
    rep = N_HEADS // N_KV_HEADS
    assert HEAD_DIM == V7X_LANES and tq % V7X_LANES == 0
    return pl.pallas_call(
        functools.partial(_attn_kernel, tq=tq, seq=seq, topk=topk),
        grid=(batch, nq),
        in_specs=[pl.BlockSpec((tq, ATT_WIDTH), lambda b, i: (b * nq + i, 0)),
                  pl.BlockSpec((tq, IDX_HEADS * IDX_DIM), lambda b, i: (b * nq + i, 0)),
                  pl.BlockSpec((tq, kww), lambda b, i: (b * nq + i, 0)),
                  pl.BlockSpec((seq, KV_WIDTH), lambda b, i: (b, 0)),
                  pl.BlockSpec((seq, KV_WIDTH), lambda b, i: (b, 0)),
                  pl.BlockSpec((seq, kww), lambda b, i: (b, 0))],
        out_specs=pl.BlockSpec((tq, ATT_WIDTH), lambda b, i: (b * nq + i, 0)),
        out_shape=jax.ShapeDtypeStruct((m, ATT_WIDTH), BF16),
        scratch_shapes=[pltpu.VMEM((nq, tq, tq), jnp.int32),
                        pltpu.VMEM((nq, tq, tq), F32),
                        pltpu.VMEM((V7X_LANES, tq), F32),
                        pltpu.VMEM((8, tq), jnp.int32),
                        pltpu.VMEM((rep * tq, V7X_LANES), F32),
                        pltpu.VMEM((rep * tq, V7X_LANES), F32),
                        pltpu.VMEM((rep * tq, HEAD_DIM), F32)],
        compiler_params=_params(("arbitrary", "arbitrary"),
                                6 * tq * ATT_WIDTH * 2, 4 * seq * KV_WIDTH * 2, 2 * (seq + tq) * kww * 4,
                                2 * seq * tq * 4, 8 * tq * tq * 4, 3 * rep * tq * V7X_LANES * 4),
        name="dsa_attention",
    )(q, qi, kw, k, v, kw)


def _sgu_kernel(u_ref, v_ref, g_ref, w_ref, bt_ref, o_ref, *, rows):
    v = v_ref[...].astype(F32)
    vn = (v * lax.rsqrt(jnp.mean(v * v, axis=-1, keepdims=True) + EPS) * g_ref[...]).astype(BF16)
    r_i = lax.broadcasted_iota(jnp.int32, (GM_CHUNK, GM_CHUNK), 0)
    c_i = lax.broadcasted_iota(jnp.int32, (GM_CHUNK, GM_CHUNK), 1)
    for g in range(GM_GROUPS):
        w = jnp.where(r_i >= c_i, w_ref[g], 0.0).astype(BF16)
        bcol = bt_ref[:, g:g + 1]
        cs = slice(g * GM_GROUP_W, (g + 1) * GM_GROUP_W)
        for n in range(rows // GM_CHUNK):
            rs = slice(n * GM_CHUNK, (n + 1) * GM_CHUNK)
            f = jnp.dot(w, vn[rs, cs], preferred_element_type=F32) + bcol
            o_ref[rs, cs] = (u_ref[rs, cs].astype(F32) * f).astype(o_ref.dtype)


def _sgu(uv, gain, w_s, b_s, seq):
    m = uv.shape[0]
    rows = _pick(seq, (512, 256, 128))
    return pl.pallas_call(
        functools.partial(_sgu_kernel, rows=rows),
        grid=(m // rows,),
        in_specs=[pl.BlockSpec((rows, GM_WIDTH), lambda i: (i, 0)),
                  pl.BlockSpec((rows, GM_WIDTH), lambda i: (i, 1)),
                  pl.BlockSpec((1, GM_WIDTH), lambda i: (0, 0)),
                  pl.BlockSpec((GM_GROUPS, GM_CHUNK, GM_CHUNK), lambda i: (0, 0, 0)),
                  pl.BlockSpec((GM_CHUNK, GM_GROUPS), lambda i: (0, 0))],
        out_specs=pl.BlockSpec((rows, GM_WIDTH), lambda i: (i, 0)),
        out_shape=jax.ShapeDtypeStruct((m, GM_WIDTH), BF16),
        compiler_params=_params(("arbitrary",), 6 * rows * GM_WIDTH * 2, 2 * rows * GM_WIDTH * 4),
        name="sgu",
    )(uv, uv, gain.reshape(1, GM_WIDTH), w_s, b_s.T)


def _w_spec(w, l, k, tn, j0=0):
    if w.ndim == 3:
        return pl.BlockSpec((None, k, tn), lambda i, j: (l, 0, j + j0))
    return pl.BlockSpec((k, tn), lambda i, j: (0, j + j0))


def _merge_kernel(ya_ref, yb_ref, wa_ref, wb_ref, ga_ref, gb_ref, o_ref):
    pa = jnp.dot(ya_ref[...], wa_ref[...].astype(BF16), preferred_element_type=F32)
    pb = jnp.dot(yb_ref[...], wb_ref[...].astype(BF16), preferred_element_type=F32)
    o_ref[...] = (ga_ref[...].astype(F32) * pa + gb_ref[...].astype(F32) * pb).astype(o_ref.dtype)


def _merge(ya, yb, wa, wb, l, gates):
    m, ka = ya.shape
    kb = yb.shape[1]
    d = wa.shape[-1]
    tm = _pick(m, (1024, 512, 256, 128))
    tn = _pick(d, (512, 256, 128))
    nj = d // tn
    wsz = wa.dtype.itemsize
    return pl.pallas_call(
        _merge_kernel,
        grid=(m // tm, nj),
        in_specs=[pl.BlockSpec((tm, ka), lambda i, j: (i, 0)),
                  pl.BlockSpec((tm, kb), lambda i, j: (i, 0)),
                  _w_spec(wa, l, ka, tn),
                  _w_spec(wb, l, kb, tn),
                  pl.BlockSpec((tm, tn), lambda i, j: (i, j)),
                  pl.BlockSpec((tm, tn), lambda i, j: (i, j + nj))],
        out_specs=pl.BlockSpec((tm, tn), lambda i, j: (i, j)),
        out_shape=jax.ShapeDtypeStruct((m, d), BF16),
        compiler_params=pltpu.CompilerParams(
            dimension_semantics=("arbitrary", "arbitrary"),
            vmem_limit_bytes=int(min(VMEM_CAP, 2 * tm * (ka + kb) * 2 + (ka + kb) * tn * (2 * wsz + 2)
                                     + 6 * tm * tn * 2 + 3 * tm * tn * 4 + (4 << 20)))),
        name="merge",
    )(ya, yb, wa, wb, gates, gates)


def _resid_kernel(a_ref, w_ref, x_ref, g_ref, o_ref):
    acc = jnp.dot(a_ref[...], w_ref[...].astype(BF16), preferred_element_type=F32)
    o_ref[...] = x_ref[...] + g_ref[...] * acc


def _resid(a, w, l, x2, mod3, g_idx, seq, *, tn_prefs, single_buffer_a, name):
    m, k = a.shape
    d = w.shape[-1]
    tm = _pick(seq, (1024, 512, 256, 128))
    per_b = seq // tm
    tn = _pick(d, tn_prefs)
    nj = d // tn
    a_bufs = 1 if single_buffer_a else 2
    wsz = w.dtype.itemsize
    a_spec = (pl.BlockSpec((tm, k), lambda i, j: (i, 0), pipeline_mode=pl.Buffered(1))
              if single_buffer_a else pl.BlockSpec((tm, k), lambda i, j: (i, 0)))
    return pl.pallas_call(
        _resid_kernel,
        grid=(m // tm, nj),
        in_specs=[a_spec,
                  _w_spec(w, l, k, tn),
                  pl.BlockSpec((tm, tn), lambda i, j: (i, j)),
                  pl.BlockSpec((None, 1, tn), lambda i, j: (i // per_b, 0, g_idx * nj + j))],
        out_specs=pl.BlockSpec((tm, tn), lambda i, j: (i, j)),
        out_shape=jax.ShapeDtypeStruct((m, d), F32),
        compiler_params=pltpu.CompilerParams(
            dimension_semantics=("arbitrary", "arbitrary"),
            vmem_limit_bytes=int(min(VMEM_CAP, a_bufs * tm * k * 2 + k * tn * (2 * wsz + 2)
                                     + 6 * tm * tn * 4 + (4 << 20)))),
        name=name,
    )(a, w, x2, mod3)


HALO = 8


def _up_kernel(h_ref, wg_ref, wv_ref, cwg_ref, cwv_ref, cbg_ref, cbv_ref, o_ref, *, sub):
    wg = wg_ref[...].astype(BF16)
    wv = wv_ref[...].astype(BF16)
    tn = wg.shape[1]

    def conv(a, halo, cw_ref, cb_ref):
        ext = jnp.concatenate([halo, a], axis=0)
        acc = cb_ref[...] + pltpu.roll(ext, 2, 0)[HALO:] * cw_ref[0:1, :]
        acc = acc + pltpu.roll(ext, 1, 0)[HALO:] * cw_ref[1:2, :]
        return acc + a * cw_ref[2:3, :]

    halo_g = halo_v = jnp.zeros((HALO, tn), F32)
    for s in range(h_ref.shape[0] // sub):
        rs = slice(s * sub, (s + 1) * sub)
        hs = h_ref[rs, :]
        ag = jnp.dot(hs, wg, preferred_element_type=F32)
        av = jnp.dot(hs, wv, preferred_element_type=F32)
        gate = conv(ag, halo_g, cwg_ref, cbg_ref)
        val = conv(av, halo_v, cwv_ref, cbv_ref)
        o_ref[rs, :] = (gate * jax.nn.sigmoid(gate) * val).astype(o_ref.dtype)
        halo_g, halo_v = ag[sub - HALO:], av[sub - HALO:]


def _up_conv_gate(h2, w_up, l, conv_w, conv_b, seq):
    m, k = h2.shape
    f = w_up.shape[-1] // 2
    tn = _pick(f, (256, 128))
    nj = f // tn
    sub = _pick(seq, (512, 256, 128))
    wsz = w_up.dtype.itemsize
    return pl.pallas_call(
        functools.partial(_up_kernel, sub=sub),
        grid=(m // seq, nj),
        in_specs=[pl.BlockSpec((seq, k), lambda i, j: (i, 0), pipeline_mode=pl.Buffered(1)),
                  _w_spec(w_up, l, k, tn),
                  _w_spec(w_up, l, k, tn, nj),
                  pl.BlockSpec((None, CONV_W, tn), lambda i, j: (l, 0, j)),
                  pl.BlockSpec((None, CONV_W, tn), lambda i, j: (l, 0, j + nj)),
                  pl.BlockSpec((None, 1, tn), lambda i, j: (l, 0, j)),
                  pl.BlockSpec((None, 1, tn), lambda i, j: (l, 0, j + nj))],
        out_specs=pl.BlockSpec((seq, tn), lambda i, j: (i, j)),
        out_shape=jax.ShapeDtypeStruct((m, f), BF16),
        compiler_params=pltpu.CompilerParams(
            dimension_semantics=("arbitrary", "arbitrary"),
            vmem_limit_bytes=int(min(VMEM_CAP, seq * k * 2 + 2 * k * tn * (2 * wsz + 2) + 2 * seq * tn * 2
                                     + 12 * sub * tn * 4 + (4 << 20)))),
        name="up_conv_gate",
    )(h2, w_up, w_up, conv_w, conv_w, conv_b.reshape(conv_b.shape[0], 1, 2 * f),
      conv_b.reshape(conv_b.shape[0], 1, 2 * f))


def kernel(x, c, ada_w, ada_b, norm1_g, w_in, q_norm_g, k_norm_g, sgu_norm_g, sgu_w, sgu_b,
           w_branch_a, w_branch_b, w_out, norm2_g, w_up, conv_w, conv_b, w_down):
    batch, seq, d = x.shape
    m = batch * seq
    depth = ada_w.shape[0]
    assert seq % GM_CHUNK == 0 and d % V7X_LANES == 0

    x2 = x.reshape(m, d)
    bp = -(-batch // 8) * 8
    c_pad = jnp.pad(c, ((0, bp - batch), (0, 0)))

    o_q = 0
    o_k = o_q + ATT_WIDTH
    o_v = o_k + KV_WIDTH
    o_qi = o_v + KV_WIDTH
    o_ki = o_qi + IDX_HEADS * IDX_DIM
    o_wi = o_ki + IDX_DIM
    o_gu = o_wi + IDX_HEADS
    o_ga = o_gu + 2 * GM_WIDTH
    o_end = o_ga + 2 * d

    kw_width = 4 * V7X_LANES
    p_kw = o_ki
    p_uv = -(-(p_kw + kw_width) // 1024) * 1024
    p_gt = p_uv + 2 * GM_WIDTH

    for l in range(depth):
        mod = _ada(c_pad, ada_w, ada_b, l)
        mod3 = mod.reshape(bp, 1, 6 * d)

        w_pk = _pack_w_in(w_in, l, o_ki=o_ki, o_gu=o_gu, o_end=o_end, p_uv=p_uv)

        h = _norm_mod(x2, norm1_g[l], mod3, 1, 0, seq)
        q = _proj(h, w_pk, o_q, ATT_WIDTH, out_dtype=BF16, epilogue="headnorm", gain=q_norm_g[l],
                  post_scale=HEAD_DIM ** -0.5, name="proj_q")
        k = _proj(h, w_pk, o_k, KV_WIDTH, out_dtype=BF16, epilogue="headnorm", gain=k_norm_g[l],
                  name="proj_k")
        v = _proj(h, w_pk, o_v, KV_WIDTH, out_dtype=BF16, name="proj_v")
        qi = _proj(h, w_pk, o_qi, IDX_HEADS * IDX_DIM, out_dtype=BF16, name="proj_qi")
        kw = _proj(h, w_pk, p_kw, kw_width, out_dtype=F32, name="proj_kw")
        uv = _proj(h, w_pk, p_uv, 2 * GM_WIDTH, out_dtype=BF16, epilogue="gelu", name="proj_uv")
        gates = _proj(h, w_pk, p_gt, 2 * d, out_dtype=BF16, epilogue="sigmoid", name="proj_gates")

        y_a = _attention(q, qi, kw, k, v, batch, seq)
        y_b = _sgu(uv, sgu_norm_g[l], sgu_w[l], sgu_b[l], seq)
        merged = _merge(y_a, y_b, w_branch_a, w_branch_b, l, gates)
        x2 = _resid(merged, w_out, l, x2, mod3, 2, seq,
                    tn_prefs=(512, 256, 128), single_buffer_a=False, name="out_proj_resid")

        h2 = _norm_mod(x2, norm2_g[l], mod3, 4, 3, seq)
        act = _up_conv_gate(h2, w_up, l, conv_w, conv_b, seq)
        x2 = _resid(act, w_down[l].astype(BF16), l, x2, mod3, 5, seq,
                    tn_prefs=(256, 128), single_buffer_a=True, name="down_proj_resid")

    return x2.reshape(batch, seq, d)
```

```python
import functools

import jax
import jax.numpy as jnp
from jax import lax
from jax.experimental import pallas as pl
from jax.experimental.pallas import tpu as pltpu

N_HEADS = 16
HEAD_DIM = 128
N_KV_HEADS = 4
ATT_WIDTH = N_HEADS * HEAD_DIM
KV_WIDTH = N_KV_HEADS * HEAD_DIM
IDX_HEADS = 32
IDX_DIM = 64
TOPK_MAX = 256
GM_WIDTH = 2048
GM_GROUPS = 8
GM_GROUP_W = GM_WIDTH // GM_GROUPS
GM_CHUNK = 128
CONV_W = 3
EPS = 1e-6
NEG_BIG = -1e30

V7X_LANES = 128
V7X_VMEM_BYTES = 64 * 1024 * 1024
VMEM_CAP = V7X_VMEM_BYTES - 8 * 1024 * 1024

BF16 = jnp.bfloat16
F32 = jnp.float32
INT_MIN = -(2 ** 31)


def _vmem_limit(*nbytes):
    return int(min(VMEM_CAP, 2 * sum(nbytes) + (4 << 20)))


def _params(semantics, *nbytes):
    return pltpu.CompilerParams(dimension_semantics=semantics, vmem_limit_bytes=_vmem_limit(*nbytes))


def _pick(n, prefs):
    for p in prefs:
        if n % p == 0:
            return p
    return n


def _ada_kernel(c_ref, w_ref, b_ref, o_ref):
    c = c_ref[...]
    cs = c * jax.nn.sigmoid(c)
    o_ref[...] = jnp.dot(cs.astype(BF16), w_ref[...].astype(BF16),
                         preferred_element_type=F32) + b_ref[...]


def _ada(c_pad, ada_w, ada_b, l):
    bp, d = c_pad.shape
    n = ada_w.shape[2]
    tn = _pick(n, (512, 256, 128))
    return pl.pallas_call(
        _ada_kernel,
        grid=(n // tn,),
        in_specs=[pl.BlockSpec((bp, d), lambda j: (0, 0)),
                  pl.BlockSpec((None, d, tn), lambda j: (l, 0, j)),
                  pl.BlockSpec((None, 1, tn), lambda j: (l, 0, j))],
        out_specs=pl.BlockSpec((bp, tn), lambda j: (0, j)),
        out_shape=jax.ShapeDtypeStruct((bp, n), F32),
        compiler_params=_params(("arbitrary",), 2 * d * tn * 4, d * tn * 2),
        name="ada_mod",
    )(c_pad, ada_w, ada_b.reshape(ada_b.shape[0], 1, n))


def _norm_mod_kernel(x_ref, g_ref, sc_ref, sh_ref, o_ref):
    x = x_ref[...]
    y = x * lax.rsqrt(jnp.mean(x * x, axis=-1, keepdims=True) + EPS) * g_ref[...]
    o_ref[...] = (y * (1.0 + sc_ref[...]) + sh_ref[...]).astype(o_ref.dtype)


def _norm_mod(x2, g, mod3, sc_idx, sh_idx, seq):
    m, d = x2.shape
    tm = _pick(seq, (256, 128, 64, 8))
    per_b = seq // tm
    return pl.pallas_call(
        _norm_mod_kernel,
        grid=(m // tm,),
        in_specs=[pl.BlockSpec((tm, d), lambda i: (i, 0)),
                  pl.BlockSpec((1, d), lambda i: (0, 0)),
                  pl.BlockSpec((None, 1, d), lambda i: (i // per_b, 0, sc_idx)),
                  pl.BlockSpec((None, 1, d), lambda i: (i // per_b, 0, sh_idx))],
        out_specs=pl.BlockSpec((tm, d), lambda i: (i, 0)),
        out_shape=jax.ShapeDtypeStruct((m, d), BF16),
        compiler_params=_params(("arbitrary",), 2 * tm * d * 4, 2 * tm * d * 2),
        name="norm_mod",
    )(x2, g.reshape(1, d), mod3, mod3)


def _gelu_exact(x):
    return 0.5 * x * (1.0 + lax.erf(x * (2.0 ** -0.5)))


MXU_ACC_COLS = 512


_NT = (((1,), (1,)), ((), ()))


def _proj_kernel(a_ref, w_ref, *rest, epilogue, post_scale):
    o_ref = rest[-1]
    tn = o_ref.shape[1]
    nsub = min(tn, MXU_ACC_COLS)
    a = a_ref[...]
    for nb in range(tn // nsub):
        cs = slice(nb * nsub, (nb + 1) * nsub)
        acc = lax.dot_general(a, w_ref[cs, :].astype(BF16), _NT, preferred_element_type=F32)
        if epilogue == "plain":
            out = acc
        elif epilogue == "gelu":
            out = _gelu_exact(acc)
        elif epilogue == "sigmoid":
            out = jax.nn.sigmoid(acc)
        elif epilogue == "headnorm":
            gain = rest[0][...]
            parts = []
            for c in range(nsub // HEAD_DIM):
                blk = acc[:, c * HEAD_DIM:(c + 1) * HEAD_DIM]
                ms = jnp.mean(blk * blk, axis=-1, keepdims=True)
                parts.append(blk * lax.rsqrt(ms + EPS) * (gain * post_scale))
            out = jnp.concatenate(parts, axis=1) if len(parts) > 1 else parts[0]
        else:
            raise ValueError(epilogue)
        o_ref[:, cs] = out.astype(o_ref.dtype)


def _proj(a, w_t, l, row0, n, *, out_dtype, epilogue="plain", gain=None, post_scale=1.0, name):
    m, k = a.shape
    assert row0 % 8 == 0 and w_t.shape[2] == k
    tm = _pick(m, (1024, 512, 256, 128))
    tn = _pick(n, (512, 256, 128))
    in_specs = [pl.BlockSpec((tm, k), lambda i, j: (i, 0)),
                pl.BlockSpec((None, pl.Element(tn), pl.Element(k)),
                             lambda i, j: (l, pl.multiple_of(row0 + j * tn, 8), 0))]
    args = [a, w_t]
    if epilogue == "headnorm":
        in_specs.append(pl.BlockSpec((1, HEAD_DIM), lambda i, j: (0, 0)))
        args.append(gain.reshape(1, HEAD_DIM))
    osz = jnp.dtype(out_dtype).itemsize
    return pl.pallas_call(
        functools.partial(_proj_kernel, epilogue=epilogue, post_scale=post_scale),
        grid=(m // tm, n // tn),
        in_specs=in_specs,
        out_specs=pl.BlockSpec((tm, tn), lambda i, j: (i, j)),
        out_shape=jax.ShapeDtypeStruct((m, n), out_dtype),
        compiler_params=pltpu.CompilerParams(
            dimension_semantics=("arbitrary", "arbitrary"),
            vmem_limit_bytes=int(min(VMEM_CAP, 2 * tm * k * 2 + tn * k * (2 * 4 + 2) + 2 * tm * tn * osz
                                     + 3 * tm * tn * 4 + (4 << 20)))),
        name=name,
    )(*args)


def _alibi_slopes(n):
    return [2.0 ** (-8.0 * (i + 1) / n) for i in range(n)]


def _attn_kernel(q_ref, qi_ref, kwq_ref, k_ref, v_ref, kwa_ref, y_ref,
                 keys_scr, bias_scr, wt_scr, sel_scr, m_scr, l_scr, acc_scr, *, tq, seq, topk):
    i = pl.program_id(1)
    n_chunks = i + 1
    t0 = i * tq
    rep = N_HEADS // N_KV_HEADS
    nt = (((1,), (1,)), ((), ()))
    row = lax.broadcasted_iota(jnp.int32, (tq, tq), 0)
    col = lax.broadcasted_iota(jnp.int32, (tq, tq), 1)
    kf = float(topk)

    w_fold = (IDX_HEADS ** -0.5) * (IDX_DIM ** -0.5)
    wt_scr[...] = (kwq_ref[...] * w_fold).T
    lane = lax.broadcasted_iota(jnp.int32, (tq, V7X_LANES), 1)

    def score_body(c, carry):
        off = pl.multiple_of(c * tq, tq)
        kraw = kwa_ref[pl.ds(off, tq), :]
        k_even = jnp.where(lane < IDX_DIM, kraw, 0.0)
        k_odd = jnp.where(lane >= IDX_DIM, pltpu.roll(kraw, IDX_DIM, 1), 0.0)
        lhs = jnp.concatenate([k_even, k_odd], axis=0).astype(BF16)
        acc = jnp.zeros((tq, tq), F32)
        for p in range(IDX_HEADS // 2):
            lg = lax.dot_general(lhs, qi_ref[:, p * V7X_LANES:(p + 1) * V7X_LANES], nt,
                                 preferred_element_type=F32)
            h0 = IDX_DIM + 2 * p
            acc = acc + wt_scr[h0:h0 + 1, :] * jnp.maximum(lg[:tq], 0.0)
            acc = acc + wt_scr[h0 + 1:h0 + 2, :] * jnp.maximum(lg[tq:], 0.0)
        bits = lax.bitcast_convert_type(acc, jnp.int32)
        key = bits ^ ((bits >> 31) & jnp.int32(0x7FFFFFFF))
        keys_scr[c] = jnp.where(off + row <= t0 + col, key, jnp.int32(INT_MIN))
        return carry

    lax.fori_loop(0, n_chunks, score_body, 0)

    def count(pred):
        def body(c, cnt):
            off = c * tq
            return cnt + jnp.sum(jnp.where(pred(keys_scr[c], off + row), 1.0, 0.0),
                                 axis=0, keepdims=True)
        return lax.fori_loop(0, n_chunks, body, jnp.zeros((1, tq), F32))

    def select_topk():
        cnt_nonneg = count(lambda kb, s: kb >= 0)
        prefix0 = jnp.where(cnt_nonneg >= kf, jnp.int32(0), jnp.int32(INT_MIN))

        def bit_body(j, prefix):
            cand = prefix | lax.shift_left(jnp.int32(1), 30 - j)
            cnt = count(lambda kb, s: kb >= cand)
            return jnp.where(cnt >= kf, cand, prefix)

        thr = lax.fori_loop(0, 31, bit_body, prefix0)
        cnt_gt = count(lambda kb, s: kb > thr)
        cnt_ge = count(lambda kb, s: kb >= thr)
        need = kf - cnt_gt
        sel_scr[0:1, :] = thr
        sel_scr[1:2, :] = jnp.full((1, tq), seq, jnp.int32)

        @pl.when(jnp.max(cnt_ge) > kf)
        def _():
            nbits = max(1, (seq - 1).bit_length())

            def idx_body(j, m):
                cand = m | lax.shift_left(jnp.int32(1), nbits - 1 - j)
                cnt = count(lambda kb, s: (kb == thr) & (s < cand))
                return jnp.where(cnt < need, cand, m)

            sel_scr[1:2, :] = lax.fori_loop(0, nbits, idx_body, jnp.zeros((1, tq), jnp.int32))

    if tq <= topk:
        @pl.when(i == 0)
        def _():
            sel_scr[0:1, :] = jnp.full((1, tq), INT_MIN, jnp.int32)
            sel_scr[1:2, :] = jnp.full((1, tq), -1, jnp.int32)

        pl.when(i > 0)(select_topk)
    else:
        select_topk()

    thr = sel_scr[0:1, :]
    m_idx = sel_scr[1:2, :]

    def bias_body(c, carry):
        off = c * tq
        kb = keys_scr[c]
        s_idx = off + row
        take = jnp.where(kb > thr, 1.0, jnp.where((kb == thr) & (s_idx <= m_idx), 1.0, 0.0))
        take = jnp.where(s_idx <= t0 + col, take, 0.0)
        bias_scr[c] = jnp.where(take > 0.5, 0.0, NEG_BIG).T
        return carry

    lax.fori_loop(0, n_chunks, bias_body, 0)

    slopes = _alibi_slopes(N_HEADS)
    d0 = (row - col).astype(F32)

    for g in range(N_KV_HEADS):
        qg = jnp.concatenate(
            [q_ref[:, (g * rep + r) * HEAD_DIM:(g * rep + r + 1) * HEAD_DIM] for r in range(rep)], axis=0)

        m_scr[...] = jnp.full(m_scr.shape, NEG_BIG, F32)
        l_scr[...] = jnp.zeros(l_scr.shape, F32)
        acc_scr[...] = jnp.zeros(acc_scr.shape, F32)

        def attn_body(c, carry, g=g, qg=qg):
            off = pl.multiple_of(c * tq, tq)
            kc = k_ref[pl.ds(off, tq), g * HEAD_DIM:(g + 1) * HEAD_DIM]
            vc = v_ref[pl.ds(off, tq), g * HEAD_DIM:(g + 1) * HEAD_DIM]
            s = lax.dot_general(qg, kc, nt, preferred_element_type=F32)
            dist = d0 + (t0 - off).astype(F32)
            bias = bias_scr[c]
            for r in range(rep):
                rows = slice(r * tq, (r + 1) * tq)
                s_r = s[rows] - slopes[g * rep + r] * dist + bias
                m_prev = m_scr[rows]
                m_new = jnp.maximum(m_prev, jnp.max(s_r, axis=-1, keepdims=True))
                alpha = jnp.exp(m_prev - m_new)
                p = jnp.exp(s_r - pltpu.repeat(m_new, tq // V7X_LANES, axis=1))
                l_scr[rows] = alpha * l_scr[rows] + jnp.sum(p, axis=-1, keepdims=True)
                m_scr[rows] = m_new
                acc_scr[rows] = alpha * acc_scr[rows] + jnp.dot(p.astype(BF16), vc,
                                                                preferred_element_type=F32)
            return carry

        lax.fori_loop(0, n_chunks, attn_body, 0)
        for r in range(rep):
            h = g * rep + r
            rows = slice(r * tq, (r + 1) * tq)
            y_ref[:, h * HEAD_DIM:(h + 1) * HEAD_DIM] = (acc_scr[rows] / l_scr[rows]).astype(y_ref.dtype)


def _attention(q, qi, kw, k, v, batch, seq):
    m = q.shape[0]
    topk = min(TOPK_MAX, seq // 4)
    tq = _pick(seq, (256, 128))
    nq = seq // tq
    kww = kw.shape[1]
    rep = N_HEADS // N_KV_HEADS
    assert HEAD_DIM == V7X_LANES and tq % V7X_LANES == 0
    assert kww == V7X_LANES == 2 * IDX_DIM and IDX_DIM + IDX_HEADS <= V7X_LANES
    return pl.pallas_call(
        functools.partial(_attn_kernel, tq=tq, seq=seq, topk=topk),
        grid=(batch, nq),
        in_specs=[pl.BlockSpec((tq, ATT_WIDTH), lambda b, i: (b * nq + i, 0)),
                  pl.BlockSpec((tq, IDX_HEADS * IDX_DIM), lambda b, i: (b * nq + i, 0)),
                  pl.BlockSpec((tq, kww), lambda b, i: (b * nq + i, 0)),
                  pl.BlockSpec((seq, KV_WIDTH), lambda b, i: (b, 0)),
                  pl.BlockSpec((seq, KV_WIDTH), lambda b, i: (b, 0)),
                  pl.BlockSpec((seq, kww), lambda b, i: (b, 0))],
        out_specs=pl.BlockSpec((tq, ATT_WIDTH), lambda b, i: (b * nq + i, 0)),
        out_shape=jax.ShapeDtypeStruct((m, ATT_WIDTH), BF16),
        scratch_shapes=[pltpu.VMEM((nq, tq, tq), jnp.int32),
                        pltpu.VMEM((nq, tq, tq), F32),
                        pltpu.VMEM((V7X_LANES, tq), F32),
                        pltpu.VMEM((8, tq), jnp.int32),
                        pltpu.VMEM((rep * tq, V7X_LANES), F32),
                        pltpu.VMEM((rep * tq, V7X_LANES), F32),
                        pltpu.VMEM((rep * tq, HEAD_DIM), F32)],
        compiler_params=_params(("arbitrary", "arbitrary"),
                                6 * tq * ATT_WIDTH * 2, 4 * seq * KV_WIDTH * 2, 2 * (seq + tq) * kww * 4,
                                2 * seq * tq * 4, 8 * tq * tq * 4, 3 * rep * tq * V7X_LANES * 4),
        name="dsa_attention",
    )(q, qi, kw, k, v, kw)


def _sgu_kernel(u_ref, v_ref, g_ref, w_ref, bt_ref, o_ref, *, rows):
    v = v_ref[...].astype(F32)
    vn = (v * lax.rsqrt(jnp.mean(v * v, axis=-1, keepdims=True) + EPS) * g_ref[...]).astype(BF16)
    r_i = lax.broadcasted_iota(jnp.int32, (GM_CHUNK, GM_CHUNK), 0)
    c_i = lax.broadcasted_iota(jnp.int32, (GM_CHUNK, GM_CHUNK), 1)
    for g in range(GM_GROUPS):
        w = jnp.where(r_i >= c_i, w_ref[g], 0.0).astype(BF16)
        bcol = bt_ref[:, g:g + 1]
        cs = slice(g * GM_GROUP_W, (g + 1) * GM_GROUP_W)
        for n in range(rows // GM_CHUNK):
            rs = slice(n * GM_CHUNK, (n + 1) * GM_CHUNK)
            f = jnp.dot(w, vn[rs, cs], preferred_element_type=F32) + bcol
            o_ref[rs, cs] = (u_ref[rs, cs].astype(F32) * f).astype(o_ref.dtype)


def _sgu(uv, gain, w_s, b_s, seq):
    m = uv.shape[0]
    rows = _pick(seq, (512, 256, 128))
    return pl.pallas_call(
        functools.partial(_sgu_kernel, rows=rows),
        grid=(m // rows,),
        in_specs=[pl.BlockSpec((rows, GM_WIDTH), lambda i: (i, 0)),
                  pl.BlockSpec((rows, GM_WIDTH), lambda i: (i, 1)),
                  pl.BlockSpec((1, GM_WIDTH), lambda i: (0, 0)),
                  pl.BlockSpec((GM_GROUPS, GM_CHUNK, GM_CHUNK), lambda i: (0, 0, 0)),
                  pl.BlockSpec((GM_CHUNK, GM_GROUPS), lambda i: (0, 0))],
        out_specs=pl.BlockSpec((rows, GM_WIDTH), lambda i: (i, 0)),
        out_shape=jax.ShapeDtypeStruct((m, GM_WIDTH), BF16),
        compiler_params=_params(("arbitrary",), 6 * rows * GM_WIDTH * 2, 2 * rows * GM_WIDTH * 4),
        name="sgu",
    )(uv, uv, gain.reshape(1, GM_WIDTH), w_s, b_s.T)


def _w_spec(w, l, k, tn, j0=0):
    if w.ndim == 3:
        return pl.BlockSpec((None, k, tn), lambda i, j: (l, 0, j + j0))
    return pl.BlockSpec((k, tn), lambda i, j: (0, j + j0))


def _merge_kernel(ya_ref, yb_ref, wa_ref, wb_ref, ga_ref, gb_ref, o_ref):
    pa = jnp.dot(ya_ref[...], wa_ref[...].astype(BF16), preferred_element_type=F32)
    pb = jnp.dot(yb_ref[...], wb_ref[...].astype(BF16), preferred_element_type=F32)
    o_ref[...] = (ga_ref[...].astype(F32) * pa + gb_ref[...].astype(F32) * pb).astype(o_ref.dtype)


def _merge(ya, yb, wa, wb, l, gates):
    m, ka = ya.shape
    kb = yb.shape[1]
    d = wa.shape[-1]
    tm = _pick(m, (1024, 512, 256, 128))
    tn = _pick(d, (512, 256, 128))
    nj = d // tn
    wsz = wa.dtype.itemsize
    return pl.pallas_call(
        _merge_kernel,
        grid=(m // tm, nj),
        in_specs=[pl.BlockSpec((tm, ka), lambda i, j: (i, 0)),
                  pl.BlockSpec((tm, kb), lambda i, j: (i, 0)),
                  _w_spec(wa, l, ka, tn),
                  _w_spec(wb, l, kb, tn),
                  pl.BlockSpec((tm, tn), lambda i, j: (i, j)),
                  pl.BlockSpec((tm, tn), lambda i, j: (i, j + nj))],
        out_specs=pl.BlockSpec((tm, tn), lambda i, j: (i, j)),
        out_shape=jax.ShapeDtypeStruct((m, d), BF16),
        compiler_params=pltpu.CompilerParams(
            dimension_semantics=("arbitrary", "arbitrary"),
            vmem_limit_bytes=int(min(VMEM_CAP, 2 * tm * (ka + kb) * 2 + (ka + kb) * tn * (2 * wsz + 2)
                                     + 6 * tm * tn * 2 + 3 * tm * tn * 4 + (4 << 20)))),
        name="merge",
    )(ya, yb, wa, wb, gates, gates)


def _resid_kernel(a_ref, w_ref, x_ref, g_ref, o_ref):
    acc = jnp.dot(a_ref[...], w_ref[...].astype(BF16), preferred_element_type=F32)
    o_ref[...] = x_ref[...] + g_ref[...] * acc


def _resid(a, w, l, x2, mod3, g_idx, seq, *, tn_prefs, single_buffer_a, name):
    m, k = a.shape
    d = w.shape[-1]
    tm = _pick(seq, (1024, 512, 256, 128))
    per_b = seq // tm
    tn = _pick(d, tn_prefs)
    nj = d // tn
    a_bufs = 1 if single_buffer_a else 2
    wsz = w.dtype.itemsize
    a_spec = (pl.BlockSpec((tm, k), lambda i, j: (i, 0), pipeline_mode=pl.Buffered(1))
              if single_buffer_a else pl.BlockSpec((tm, k), lambda i, j: (i, 0)))
    return pl.pallas_call(
        _resid_kernel,
        grid=(m // tm, nj),
        in_specs=[a_spec,
                  _w_spec(w, l, k, tn),
                  pl.BlockSpec((tm, tn), lambda i, j: (i, j)),
                  pl.BlockSpec((None, 1, tn), lambda i, j: (i // per_b, 0, g_idx * nj + j))],
        out_specs=pl.BlockSpec((tm, tn), lambda i, j: (i, j)),
        out_shape=jax.ShapeDtypeStruct((m, d), F32),
        compiler_params=pltpu.CompilerParams(
            dimension_semantics=("arbitrary", "arbitrary"),
            vmem_limit_bytes=int(min(VMEM_CAP, a_bufs * tm * k * 2 + k * tn * (2 * wsz + 2)
                                     + 6 * tm * tn * 4 + (4 << 20)))),
        name=name,
    )(a, w, x2, mod3)


HALO = 8


def _up_kernel(h_ref, wg_ref, wv_ref, cwg_ref, cwv_ref, cbg_ref, cbv_ref, o_ref, *, sub):
    wg = wg_ref[...].astype(BF16)
    wv = wv_ref[...].astype(BF16)
    tn = wg.shape[1]

    def conv(a, halo, cw_ref, cb_ref):
        ext = jnp.concatenate([halo, a], axis=0)
        acc = cb_ref[...] + pltpu.roll(ext, 2, 0)[HALO:] * cw_ref[0:1, :]
        acc = acc + pltpu.roll(ext, 1, 0)[HALO:] * cw_ref[1:2, :]
        return acc + a * cw_ref[2:3, :]

    halo_g = halo_v = jnp.zeros((HALO, tn), F32)
    for s in range(h_ref.shape[0] // sub):
        rs = slice(s * sub, (s + 1) * sub)
        hs = h_ref[rs, :]
        ag = jnp.dot(hs, wg, preferred_element_type=F32)
        av = jnp.dot(hs, wv, preferred_element_type=F32)
        gate = conv(ag, halo_g, cwg_ref, cbg_ref)
        val = conv(av, halo_v, cwv_ref, cbv_ref)
        o_ref[rs, :] = (gate * jax.nn.sigmoid(gate) * val).astype(o_ref.dtype)
        halo_g, halo_v = ag[sub - HALO:], av[sub - HALO:]


def _up_conv_gate(h2, w_up, l, conv_w, conv_b, seq):
    m, k = h2.shape
    f = w_up.shape[-1] // 2
    tn = _pick(f, (256, 128))
    nj = f // tn
    sub = _pick(seq, (512, 256, 128))
    wsz = w_up.dtype.itemsize
    return pl.pallas_call(
        functools.partial(_up_kernel, sub=sub),
        grid=(m // seq, nj),
        in_specs=[pl.BlockSpec((seq, k), lambda i, j: (i, 0), pipeline_mode=pl.Buffered(1)),
                  _w_spec(w_up, l, k, tn),
                  _w_spec(w_up, l, k, tn, nj),
                  pl.BlockSpec((None, CONV_W, tn), lambda i, j: (l, 0, j)),
                  pl.BlockSpec((None, CONV_W, tn), lambda i, j: (l, 0, j + nj)),
                  pl.BlockSpec((None, 1, tn), lambda i, j: (l, 0, j)),
                  pl.BlockSpec((None, 1, tn), lambda i, j: (l, 0, j + nj))],
        out_specs=pl.BlockSpec((seq, tn), lambda i, j: (i, j)),
        out_shape=jax.ShapeDtypeStruct((m, f), BF16),
        compiler_params=pltpu.CompilerParams(
            dimension_semantics=("arbitrary", "arbitrary"),
            vmem_limit_bytes=int(min(VMEM_CAP, seq * k * 2 + 2 * k * tn * (2 * wsz + 2) + 2 * seq * tn * 2
                                     + 12 * sub * tn * 4 + (4 << 20)))),
        name="up_conv_gate",
    )(h2, w_up, w_up, conv_w, conv_w, conv_b.reshape(conv_b.shape[0], 1, 2 * f),
      conv_b.reshape(conv_b.shape[0], 1, 2 * f))


def kernel(x, c, ada_w, ada_b, norm1_g, w_in, q_norm_g, k_norm_g, sgu_norm_g, sgu_w, sgu_b,
           w_branch_a, w_branch_b, w_out, norm2_g, w_up, conv_w, conv_b, w_down):
    batch, seq, d = x.shape
    m = batch * seq
    depth = ada_w.shape[0]
    assert seq % GM_CHUNK == 0 and d % V7X_LANES == 0

    x2 = x.reshape(m, d)
    bp = -(-batch // 8) * 8
    c_pad = jnp.pad(c, ((0, bp - batch), (0, 0)))

    o_q = 0
    o_k = o_q + ATT_WIDTH
    o_v = o_k + KV_WIDTH
    o_qi = o_v + KV_WIDTH
    o_ki = o_qi + IDX_HEADS * IDX_DIM
    o_wi = o_ki + IDX_DIM
    o_gu = o_wi + IDX_HEADS
    o_ga = o_gu + 2 * GM_WIDTH
    o_end = o_ga + 2 * d

    assert o_gu - o_ki <= V7X_LANES and o_end == w_in.shape[2]
    w_in_t = jnp.swapaxes(w_in, 1, 2)

    for l in range(depth):
        mod = _ada(c_pad, ada_w, ada_b, l)
        mod3 = mod.reshape(bp, 1, 6 * d)

        h = _norm_mod(x2, norm1_g[l], mod3, 1, 0, seq)
        q = _proj(h, w_in_t, l, o_q, ATT_WIDTH, out_dtype=BF16, epilogue="headnorm", gain=q_norm_g[l],
                  post_scale=HEAD_DIM ** -0.5, name="proj_q")
        k = _proj(h, w_in_t, l, o_k, KV_WIDTH, out_dtype=BF16, epilogue="headnorm", gain=k_norm_g[l],
                  name="proj_k")
        v = _proj(h, w_in_t, l, o_v, KV_WIDTH, out_dtype=BF16, name="proj_v")
        qi = _proj(h, w_in_t, l, o_qi, IDX_HEADS * IDX_DIM, out_dtype=BF16, name="proj_qi")
        kw = _proj(h, w_in_t, l, o_ki, V7X_LANES, out_dtype=F32, name="proj_kw")
        uv = _proj(h, w_in_t, l, o_gu, 2 * GM_WIDTH, out_dtype=BF16, epilogue="gelu", name="proj_uv")
        gates = _proj(h, w_in_t, l, o_ga, 2 * d, out_dtype=BF16, epilogue="sigmoid", name="proj_gates")

        y_a = _attention(q, qi, kw, k, v, batch, seq)
        y_b = _sgu(uv, sgu_norm_g[l], sgu_w[l], sgu_b[l], seq)
        merged = _merge(y_a, y_b, w_branch_a, w_branch_b, l, gates)
        x2 = _resid(merged, w_out, l, x2, mod3, 2, seq,
                    tn_prefs=(512, 256, 128), single_buffer_a=False, name="out_proj_resid")

        h2 = _norm_mod(x2, norm2_g[l], mod3, 4, 3, seq)
        act = _up_conv_gate(h2, w_up, l, conv_w, conv_b, seq)
        x2 = _resid(act, w_down[l].astype(BF16), l, x2, mod3, 5, seq,
                    tn_prefs=(256, 128), single_buffer_a=True, name="down_proj_resid")

    return x2.reshape(batch, seq, d)
```

```python
import functools

import numpy as np
import jax
import jax.numpy as jnp
from jax import lax
from jax.experimental import pallas as pl
from jax.experimental.pallas import tpu as pltpu

N_HEADS = 16
HEAD_DIM = 128
N_KV_HEADS = 4
ATT_WIDTH = N_HEADS * HEAD_DIM
KV_WIDTH = N_KV_HEADS * HEAD_DIM
IDX_HEADS = 32
IDX_DIM = 64
TOPK_MAX = 256
GM_WIDTH = 2048
GM_GROUPS = 8
GM_GROUP_W = GM_WIDTH // GM_GROUPS
GM_CHUNK = 128
CONV_W = 3
EPS = 1e-6
NEG_BIG = -1e30

V7X_LANES = 128
V7X_VMEM_BYTES = 64 * 1024 * 1024
VMEM_CAP = V7X_VMEM_BYTES - 8 * 1024 * 1024

BF16 = jnp.bfloat16
F32 = jnp.float32
INT_MIN = -(2 ** 31)


def _vmem_limit(*nbytes):
    return int(min(VMEM_CAP, 2 * sum(nbytes) + (4 << 20)))


def _params(semantics, *nbytes):
    return pltpu.CompilerParams(dimension_semantics=semantics, vmem_limit_bytes=_vmem_limit(*nbytes))


def _pick(n, prefs):
    for p in prefs:
        if n % p == 0:
            return p
    return n


def _ada_kernel(c_ref, w_ref, b_ref, o_ref):
    c = c_ref[...]
    cs = c * jax.nn.sigmoid(c)
    o_ref[...] = jnp.dot(cs.astype(BF16), w_ref[...].astype(BF16),
                         preferred_element_type=F32) + b_ref[...]


def _ada(c_pad, ada_w, ada_b, l):
    bp, d = c_pad.shape
    n = ada_w.shape[2]
    tn = _pick(n, (512, 256, 128))
    return pl.pallas_call(
        _ada_kernel,
        grid=(n // tn,),
        in_specs=[pl.BlockSpec((bp, d), lambda j: (0, 0)),
                  pl.BlockSpec((None, d, tn), lambda j: (l, 0, j)),
                  pl.BlockSpec((None, 1, tn), lambda j: (l, 0, j))],
        out_specs=pl.BlockSpec((bp, tn), lambda j: (0, j)),
        out_shape=jax.ShapeDtypeStruct((bp, n), F32),
        compiler_params=_params(("arbitrary",), 2 * d * tn * 4, d * tn * 2),
        name="ada_mod",
    )(c_pad, ada_w, ada_b.reshape(ada_b.shape[0], 1, n))


def _norm_mod_kernel(x_ref, g_ref, sc_ref, sh_ref, o_ref):
    x = x_ref[...]
    y = x * lax.rsqrt(jnp.mean(x * x, axis=-1, keepdims=True) + EPS) * g_ref[...]
    o_ref[...] = (y * (1.0 + sc_ref[...]) + sh_ref[...]).astype(o_ref.dtype)


def _norm_mod(x2, g, mod3, sc_idx, sh_idx, seq):
    m, d = x2.shape
    tm = _pick(seq, (256, 128, 64, 8))
    per_b = seq // tm
    return pl.pallas_call(
        _norm_mod_kernel,
        grid=(m // tm,),
        in_specs=[pl.BlockSpec((tm, d), lambda i: (i, 0)),
                  pl.BlockSpec((1, d), lambda i: (0, 0)),
                  pl.BlockSpec((None, 1, d), lambda i: (i // per_b, 0, sc_idx)),
                  pl.BlockSpec((None, 1, d), lambda i: (i // per_b, 0, sh_idx))],
        out_specs=pl.BlockSpec((tm, d), lambda i: (i, 0)),
        out_shape=jax.ShapeDtypeStruct((m, d), BF16),
        compiler_params=_params(("arbitrary",), 2 * tm * d * 4, 2 * tm * d * 2),
        name="norm_mod",
    )(x2, g.reshape(1, d), mod3, mod3)


def _gelu_exact(x):
    return 0.5 * x * (1.0 + lax.erf(x * (2.0 ** -0.5)))


MXU_ACC_ROWS = 512


_NT = (((1,), (1,)), ((), ()))


def _proj_kernel(a_ref, w_ref, *rest, epilogue, post_scale):
    o_ref = rest[-1]
    tm, tn = o_ref.shape
    rsub = min(tm, MXU_ACC_ROWS)
    w = w_ref[...].astype(BF16)
    for rb in range(tm // rsub):
        rs = slice(rb * rsub, (rb + 1) * rsub)
        acc = lax.dot_general(a_ref[rs, :], w, _NT, preferred_element_type=F32)
        if epilogue == "plain":
            out = acc
        elif epilogue == "gelu":
            out = _gelu_exact(acc)
        elif epilogue == "sigmoid":
            out = jax.nn.sigmoid(acc)
        elif epilogue == "headnorm":
            gain = rest[0][...]
            parts = []
            for c in range(tn // HEAD_DIM):
                blk = acc[:, c * HEAD_DIM:(c + 1) * HEAD_DIM]
                ms = jnp.mean(blk * blk, axis=-1, keepdims=True)
                parts.append(blk * lax.rsqrt(ms + EPS) * (gain * post_scale))
            out = jnp.concatenate(parts, axis=1) if len(parts) > 1 else parts[0]
        else:
            raise ValueError(epilogue)
        o_ref[rs, :] = out.astype(o_ref.dtype)


def _proj(a, w_t, l, row0, n, *, out_dtype, epilogue="plain", gain=None, post_scale=1.0, name):
    m, k = a.shape
    assert row0 % 8 == 0 and w_t.shape[2] == k
    tm = _pick(m, (1024, 512, 256, 128))
    tn = _pick(n, (512, 256, 128))
    in_specs = [pl.BlockSpec((tm, k), lambda i, j: (i, 0)),
                pl.BlockSpec((None, pl.Element(tn), pl.Element(k)),
                             lambda i, j: (l, pl.multiple_of(row0 + j * tn, 8), 0))]
    args = [a, w_t]
    if epilogue == "headnorm":
        in_specs.append(pl.BlockSpec((1, HEAD_DIM), lambda i, j: (0, 0)))
        args.append(gain.reshape(1, HEAD_DIM))
    osz = jnp.dtype(out_dtype).itemsize
    return pl.pallas_call(
        functools.partial(_proj_kernel, epilogue=epilogue, post_scale=post_scale),
        grid=(m // tm, n // tn),
        in_specs=in_specs,
        out_specs=pl.BlockSpec((tm, tn), lambda i, j: (i, j)),
        out_shape=jax.ShapeDtypeStruct((m, n), out_dtype),
        compiler_params=pltpu.CompilerParams(
            dimension_semantics=("arbitrary", "arbitrary"),
            vmem_limit_bytes=int(min(VMEM_CAP, 2 * tm * k * 2 + tn * k * (2 * 4 + 2) + 2 * tm * tn * osz
                                     + 3 * tm * tn * 4 + (4 << 20)))),
        name=name,
    )(*args)


def _alibi_slopes(n):
    return [2.0 ** (-8.0 * (i + 1) / n) for i in range(n)]


LOG2E = 1.4426950408889634
ALIBI_PIECES = 4
POS_SPLIT = 256


def _alibi_tables(seq):
    assert seq <= POS_SPLIT * POS_SPLIT and 2 * ALIBI_PIECES <= V7X_LANES
    slope_tab = np.zeros((N_HEADS, V7X_LANES), np.float32)
    for h, s in enumerate(_alibi_slopes(N_HEADS)):
        rest = float(np.float32(s)) * LOG2E
        for p in range(ALIBI_PIECES):
            piece = float(np.asarray(rest, dtype=BF16))
            slope_tab[h, 2 * p:2 * p + 2] = piece
            rest -= piece
    pos = np.arange(seq)
    pos_tab = np.zeros((seq, V7X_LANES), np.float32)
    pos_tab[:, 0:2 * ALIBI_PIECES:2] = ((pos // POS_SPLIT) * POS_SPLIT)[:, None]
    pos_tab[:, 1:2 * ALIBI_PIECES:2] = (pos % POS_SPLIT)[:, None]
    return jnp.asarray(slope_tab), jnp.asarray(pos_tab, dtype=BF16)


def _attn_kernel(q_ref, qi_ref, kwq_ref, k_ref, v_ref, kwa_ref, slope_ref, pos_ref, y_ref,
                 keys_scr, bias_scr, wt_scr, sel_scr, vt_scr, qa_scr, ml_scr, acc_scr, *, tq, seq, topk):
    i = pl.program_id(1)
    n_chunks = i + 1
    t0 = i * tq
    rep = N_HEADS // N_KV_HEADS
    nt = (((1,), (1,)), ((), ()))
    row = lax.broadcasted_iota(jnp.int32, (tq, tq), 0)
    col = lax.broadcasted_iota(jnp.int32, (tq, tq), 1)
    kf = float(topk)

    w_fold = (IDX_HEADS ** -0.5) * (IDX_DIM ** -0.5)
    wt_scr[...] = (kwq_ref[...] * w_fold).T
    lane = lax.broadcasted_iota(jnp.int32, (tq, V7X_LANES), 1)

    def score_body(c, carry):
        off = pl.multiple_of(c * tq, tq)
        kraw = kwa_ref[pl.ds(off, tq), :]
        k_even = jnp.where(lane < IDX_DIM, kraw, 0.0)
        k_odd = jnp.where(lane >= IDX_DIM, pltpu.roll(kraw, IDX_DIM, 1), 0.0)
        lhs = jnp.concatenate([k_even, k_odd], axis=0).astype(BF16)
        acc = jnp.zeros((tq, tq), F32)
        for p in range(IDX_HEADS // 2):
            lg = lax.dot_general(lhs, qi_ref[:, p * V7X_LANES:(p + 1) * V7X_LANES], nt,
                                 preferred_element_type=F32)
            h0 = IDX_DIM + 2 * p
            acc = acc + wt_scr[h0:h0 + 1, :] * jnp.maximum(lg[:tq], 0.0)
            acc = acc + wt_scr[h0 + 1:h0 + 2, :] * jnp.maximum(lg[tq:], 0.0)
        bits = lax.bitcast_convert_type(acc, jnp.int32)
        key = bits ^ ((bits >> 31) & jnp.int32(0x7FFFFFFF))
        keys_scr[c] = jnp.where(off + row <= t0 + col, key, jnp.int32(INT_MIN))
        return carry

    lax.fori_loop(0, n_chunks, score_body, 0)

    def count(pred):
        def body(c, cnt):
            off = c * tq
            return cnt + jnp.sum(jnp.where(pred(keys_scr[c], off + row), 1.0, 0.0),
                                 axis=0, keepdims=True)
        return lax.fori_loop(0, n_chunks, body, jnp.zeros((1, tq), F32))

    def select_topk():
        cnt_nonneg = count(lambda kb, s: kb >= 0)
        prefix0 = jnp.where(cnt_nonneg >= kf, jnp.int32(0), jnp.int32(INT_MIN))

        def bit_body(j, prefix):
            cand = prefix | lax.shift_left(jnp.int32(1), 30 - j)
            cnt = count(lambda kb, s: kb >= cand)
            return jnp.where(cnt >= kf, cand, prefix)

        thr = lax.fori_loop(0, 31, bit_body, prefix0)
        cnt_gt = count(lambda kb, s: kb > thr)
        cnt_ge = count(lambda kb, s: kb >= thr)
        need = kf - cnt_gt
        sel_scr[0:1, :] = thr
        sel_scr[1:2, :] = jnp.full((1, tq), seq, jnp.int32)

        @pl.when(jnp.max(cnt_ge) > kf)
        def _():
            nbits = max(1, (seq - 1).bit_length())

            def idx_body(j, m):
                cand = m | lax.shift_left(jnp.int32(1), nbits - 1 - j)
                cnt = count(lambda kb, s: (kb == thr) & (s < cand))
                return jnp.where(cnt < need, cand, m)

            sel_scr[1:2, :] = lax.fori_loop(0, nbits, idx_body, jnp.zeros((1, tq), jnp.int32))

    if tq <= topk:
        @pl.when(i == 0)
        def _():
            sel_scr[0:1, :] = jnp.full((1, tq), INT_MIN, jnp.int32)
            sel_scr[1:2, :] = jnp.full((1, tq), -1, jnp.int32)

        pl.when(i > 0)(select_topk)
    else:
        select_topk()

    thr = sel_scr[0:1, :]
    m_idx = sel_scr[1:2, :]

    def bias_body(c, carry):
        off = c * tq
        kb = keys_scr[c]
        s_idx = off + row
        take = jnp.where(kb > thr, 1.0, jnp.where((kb == thr) & (s_idx <= m_idx), 1.0, 0.0))
        take = jnp.where(s_idx <= t0 + col, take, 0.0)
        bias_scr[c] = jnp.where(take > 0.5, 0.0, NEG_BIG)
        return carry

    lax.fori_loop(0, n_chunks, bias_body, 0)

    @pl.when(i == 0)
    def _():
        for c in range(seq // tq):
            for g in range(N_KV_HEADS):
                blk = v_ref[c * tq:(c + 1) * tq, g * HEAD_DIM:(g + 1) * HEAD_DIM].astype(F32)
                vt_scr[c, g * HEAD_DIM:(g + 1) * HEAD_DIM, :] = blk.T.astype(BF16)

    for h in range(N_HEADS):
        g, r = divmod(h, rep)
        qa_scr[g, r * tq:(r + 1) * tq, :HEAD_DIM] = q_ref[:, h * HEAD_DIM:(h + 1) * HEAD_DIM]
        qa_scr[g, r * tq:(r + 1) * tq, HEAD_DIM:] = jnp.broadcast_to(
            slope_ref[h:h + 1, :], (tq, V7X_LANES)).astype(BF16)
    acc_scr[...] = jnp.zeros(acc_scr.shape, F32)
    for g in range(N_KV_HEADS):
        ml_scr[2 * g:2 * g + 1, :] = jnp.full((1, rep * tq), NEG_BIG, F32)
        ml_scr[2 * g + 1:2 * g + 2, :] = jnp.zeros((1, rep * tq), F32)

    def attn_body(c, carry):
        off = pl.multiple_of(c * tq, tq)
        pos_c = pos_ref[pl.ds(off, tq), :]
        bias = jnp.concatenate([bias_scr[c]] * rep, axis=1)
        for g in range(N_KV_HEADS):
            kc = jnp.concatenate([k_ref[pl.ds(off, tq), g * HEAD_DIM:(g + 1) * HEAD_DIM], pos_c],
                                 axis=1)
            vt = vt_scr[c, g * HEAD_DIM:(g + 1) * HEAD_DIM, :]
            st = lax.dot_general(kc, qa_scr[g], nt, preferred_element_type=F32) + bias
            m_prev = ml_scr[2 * g:2 * g + 1, :]
            m_new = jnp.maximum(m_prev, jnp.max(st, axis=0, keepdims=True))
            alpha = jnp.exp2(m_prev - m_new)
            pt = jnp.exp2(st - m_new)
            ml_scr[2 * g:2 * g + 1, :] = m_new
            ml_scr[2 * g + 1:2 * g + 2, :] = (alpha * ml_scr[2 * g + 1:2 * g + 2, :]
                                              + jnp.sum(pt, axis=0, keepdims=True))
            acc_scr[g] = alpha * acc_scr[g] + jnp.dot(vt, pt.astype(BF16), preferred_element_type=F32)
        return carry

    lax.fori_loop(0, n_chunks, attn_body, 0)
    for g in range(N_KV_HEADS):
        o_t = acc_scr[g] / ml_scr[2 * g + 1:2 * g + 2, :]
        for r in range(rep):
            h = g * rep + r
            y_ref[:, h * HEAD_DIM:(h + 1) * HEAD_DIM] = o_t[:, r * tq:(r + 1) * tq].T.astype(y_ref.dtype)


def _attention(q, qi, kw, k, v, batch, seq):
    m = q.shape[0]
    topk = min(TOPK_MAX, seq // 4)
    tq = _pick(seq, (256, 128))
    nq = seq // tq
    kww = kw.shape[1]
    rep = N_HEADS // N_KV_HEADS
    assert HEAD_DIM == V7X_LANES and tq % V7X_LANES == 0
    assert kww == V7X_LANES == 2 * IDX_DIM and IDX_DIM + IDX_HEADS <= V7X_LANES
    return pl.pallas_call(
        functools.partial(_attn_kernel, tq=tq, seq=seq, topk=topk),
        grid=(batch, nq),
        in_specs=[pl.BlockSpec((tq, ATT_WIDTH), lambda b, i: (b * nq + i, 0)),
                  pl.BlockSpec((tq, IDX_HEADS * IDX_DIM), lambda b, i: (b * nq + i, 0)),
                  pl.BlockSpec((tq, kww), lambda b, i: (b * nq + i, 0)),
                  pl.BlockSpec((seq, KV_WIDTH), lambda b, i: (b, 0)),
                  pl.BlockSpec((seq, KV_WIDTH), lambda b, i: (b, 0)),
                  pl.BlockSpec((seq, kww), lambda b, i: (b, 0)),
                  pl.BlockSpec((N_HEADS, V7X_LANES), lambda b, i: (0, 0)),
                  pl.BlockSpec((seq, V7X_LANES), lambda b, i: (0, 0))],
        out_specs=pl.BlockSpec((tq, ATT_WIDTH), lambda b, i: (b * nq + i, 0)),
        out_shape=jax.ShapeDtypeStruct((m, ATT_WIDTH), BF16),
        scratch_shapes=[pltpu.VMEM((nq, tq, tq), jnp.int32),
                        pltpu.VMEM((nq, tq, tq), F32),
                        pltpu.VMEM((V7X_LANES, tq), F32),
                        pltpu.VMEM((8, tq), jnp.int32),
                        pltpu.VMEM((nq, KV_WIDTH, tq), BF16),
                        pltpu.VMEM((N_KV_HEADS, rep * tq, 2 * HEAD_DIM), BF16),
                        pltpu.VMEM((2 * N_KV_HEADS, rep * tq), F32),
                        pltpu.VMEM((N_KV_HEADS, HEAD_DIM, rep * tq), F32)],
        compiler_params=_params(("arbitrary", "arbitrary"),
                                6 * tq * ATT_WIDTH * 2, 4 * seq * KV_WIDTH * 2, 2 * (seq + tq) * kww * 4,
                                2 * seq * tq * 4, 8 * tq * tq * 4, seq * KV_WIDTH * 2, rep * HEAD_DIM * tq * 4),
        name="dsa_attention",
    )(q, qi, kw, k, v, kw, *_alibi_tables(seq))


def _sgu_kernel(u_ref, v_ref, g_ref, w_ref, bt_ref, o_ref, *, rows):
    v = v_ref[...].astype(F32)
    vn = (v * lax.rsqrt(jnp.mean(v * v, axis=-1, keepdims=True) + EPS) * g_ref[...]).astype(BF16)
    r_i = lax.broadcasted_iota(jnp.int32, (GM_CHUNK, GM_CHUNK), 0)
    c_i = lax.broadcasted_iota(jnp.int32, (GM_CHUNK, GM_CHUNK), 1)
    for g in range(GM_GROUPS):
        w = jnp.where(r_i >= c_i, w_ref[g], 0.0).astype(BF16)
        bcol = bt_ref[:, g:g + 1]
        cs = slice(g * GM_GROUP_W, (g + 1) * GM_GROUP_W)
        for n in range(rows // GM_CHUNK):
            rs = slice(n * GM_CHUNK, (n + 1) * GM_CHUNK)
            f = jnp.dot(w, vn[rs, cs], preferred_element_type=F32) + bcol
            o_ref[rs, cs] = (u_ref[rs, cs].astype(F32) * f).astype(o_ref.dtype)


def _sgu(uv, gain, w_s, b_s, seq):
    m = uv.shape[0]
    rows = _pick(seq, (512, 256, 128))
    return pl.pallas_call(
        functools.partial(_sgu_kernel, rows=rows),
        grid=(m // rows,),
        in_specs=[pl.BlockSpec((rows, GM_WIDTH), lambda i: (i, 0)),
                  pl.BlockSpec((rows, GM_WIDTH), lambda i: (i, 1)),
                  pl.BlockSpec((1, GM_WIDTH), lambda i: (0, 0)),
                  pl.BlockSpec((GM_GROUPS, GM_CHUNK, GM_CHUNK), lambda i: (0, 0, 0)),
                  pl.BlockSpec((GM_CHUNK, GM_GROUPS), lambda i: (0, 0))],
        out_specs=pl.BlockSpec((rows, GM_WIDTH), lambda i: (i, 0)),
        out_shape=jax.ShapeDtypeStruct((m, GM_WIDTH), BF16),
        compiler_params=_params(("arbitrary",), 6 * rows * GM_WIDTH * 2, 2 * rows * GM_WIDTH * 4),
        name="sgu",
    )(uv, uv, gain.reshape(1, GM_WIDTH), w_s, b_s.T)


def _w_spec(w, l, k, tn, j0=0):
    if w.ndim == 3:
        return pl.BlockSpec((None, k, tn), lambda i, j: (l, 0, j + j0))
    return pl.BlockSpec((k, tn), lambda i, j: (0, j + j0))


def _merge_kernel(ya_ref, yb_ref, wa_ref, wb_ref, ga_ref, gb_ref, o_ref):
    pa = jnp.dot(ya_ref[...], wa_ref[...].astype(BF16), preferred_element_type=F32)
    pb = jnp.dot(yb_ref[...], wb_ref[...].astype(BF16), preferred_element_type=F32)
    o_ref[...] = (ga_ref[...].astype(F32) * pa + gb_ref[...].astype(F32) * pb).astype(o_ref.dtype)


def _merge(ya, yb, wa, wb, l, gates):
    m, ka = ya.shape
    kb = yb.shape[1]
    d = wa.shape[-1]
    tm = _pick(m, (1024, 512, 256, 128))
    tn = _pick(d, (512, 256, 128))
    nj = d // tn
    wsz = wa.dtype.itemsize
    return pl.pallas_call(
        _merge_kernel,
        grid=(m // tm, nj),
        in_specs=[pl.BlockSpec((tm, ka), lambda i, j: (i, 0)),
                  pl.BlockSpec((tm, kb), lambda i, j: (i, 0)),
                  _w_spec(wa, l, ka, tn),
                  _w_spec(wb, l, kb, tn),
                  pl.BlockSpec((tm, tn), lambda i, j: (i, j)),
                  pl.BlockSpec((tm, tn), lambda i, j: (i, j + nj))],
        out_specs=pl.BlockSpec((tm, tn), lambda i, j: (i, j)),
        out_shape=jax.ShapeDtypeStruct((m, d), BF16),
        compiler_params=pltpu.CompilerParams(
            dimension_semantics=("arbitrary", "arbitrary"),
            vmem_limit_bytes=int(min(VMEM_CAP, 2 * tm * (ka + kb) * 2 + (ka + kb) * tn * (2 * wsz + 2)
                                     + 6 * tm * tn * 2 + 3 * tm * tn * 4 + (4 << 20)))),
        name="merge",
    )(ya, yb, wa, wb, gates, gates)


def _resid_kernel(a_ref, w_ref, x_ref, g_ref, o_ref):
    acc = jnp.dot(a_ref[...], w_ref[...].astype(BF16), preferred_element_type=F32)
    o_ref[...] = x_ref[...] + g_ref[...] * acc


def _resid(a, w, l, x2, mod3, g_idx, seq, *, tn_prefs, single_buffer_a, name):
    m, k = a.shape
    d = w.shape[-1]
    tm = _pick(seq, (1024, 512, 256, 128))
    per_b = seq // tm
    tn = _pick(d, tn_prefs)
    nj = d // tn
    a_bufs = 1 if single_buffer_a else 2
    wsz = w.dtype.itemsize
    a_spec = (pl.BlockSpec((tm, k), lambda i, j: (i, 0), pipeline_mode=pl.Buffered(1))
              if single_buffer_a else pl.BlockSpec((tm, k), lambda i, j: (i, 0)))
    return pl.pallas_call(
        _resid_kernel,
        grid=(m // tm, nj),
        in_specs=[a_spec,
                  _w_spec(w, l, k, tn),
                  pl.BlockSpec((tm, tn), lambda i, j: (i, j)),
                  pl.BlockSpec((None, 1, tn), lambda i, j: (i // per_b, 0, g_idx * nj + j))],
        out_specs=pl.BlockSpec((tm, tn), lambda i, j: (i, j)),
        out_shape=jax.ShapeDtypeStruct((m, d), F32),
        compiler_params=pltpu.CompilerParams(
            dimension_semantics=("arbitrary", "arbitrary"),
            vmem_limit_bytes=int(min(VMEM_CAP, a_bufs * tm * k * 2 + k * tn * (2 * wsz + 2)
                                     + 6 * tm * tn * 4 + (4 << 20)))),
        name=name,
    )(a, w, x2, mod3)


HALO = 8


def _up_kernel(h_ref, wg_ref, wv_ref, cwg_ref, cwv_ref, cbg_ref, cbv_ref, o_ref, *, sub):
    wg = wg_ref[...].astype(BF16)
    wv = wv_ref[...].astype(BF16)
    tn = wg.shape[1]

    def conv(a, halo, cw_ref, cb_ref):
        ext = jnp.concatenate([halo, a], axis=0)
        acc = cb_ref[...] + pltpu.roll(ext, 2, 0)[HALO:] * cw_ref[0:1, :]
        acc = acc + pltpu.roll(ext, 1, 0)[HALO:] * cw_ref[1:2, :]
        return acc + a * cw_ref[2:3, :]

    halo_g = halo_v = jnp.zeros((HALO, tn), F32)
    for s in range(h_ref.shape[0] // sub):
        rs = slice(s * sub, (s + 1) * sub)
        hs = h_ref[rs, :]
        ag = jnp.dot(hs, wg, preferred_element_type=F32)
        av = jnp.dot(hs, wv, preferred_element_type=F32)
        gate = conv(ag, halo_g, cwg_ref, cbg_ref)
        val = conv(av, halo_v, cwv_ref, cbv_ref)
        o_ref[rs, :] = (gate * jax.nn.sigmoid(gate) * val).astype(o_ref.dtype)
        halo_g, halo_v = ag[sub - HALO:], av[sub - HALO:]


def _up_conv_gate(h2, w_up, l, conv_w, conv_b, seq):
    m, k = h2.shape
    f = w_up.shape[-1] // 2
    tn = _pick(f, (256, 128))
    nj = f // tn
    sub = _pick(seq, (512, 256, 128))
    wsz = w_up.dtype.itemsize
    return pl.pallas_call(
        functools.partial(_up_kernel, sub=sub),
        grid=(m // seq, nj),
        in_specs=[pl.BlockSpec((seq, k), lambda i, j: (i, 0), pipeline_mode=pl.Buffered(1)),
                  _w_spec(w_up, l, k, tn),
                  _w_spec(w_up, l, k, tn, nj),
                  pl.BlockSpec((None, CONV_W, tn), lambda i, j: (l, 0, j)),
                  pl.BlockSpec((None, CONV_W, tn), lambda i, j: (l, 0, j + nj)),
                  pl.BlockSpec((None, 1, tn), lambda i, j: (l, 0, j)),
                  pl.BlockSpec((None, 1, tn), lambda i, j: (l, 0, j + nj))],
        out_specs=pl.BlockSpec((seq, tn), lambda i, j: (i, j)),
        out_shape=jax.ShapeDtypeStruct((m, f), BF16),
        compiler_params=pltpu.CompilerParams(
            dimension_semantics=("arbitrary", "arbitrary"),
            vmem_limit_bytes=int(min(VMEM_CAP, seq * k * 2 + 2 * k * tn * (2 * wsz + 2) + 2 * seq * tn * 2
                                     + 12 * sub * tn * 4 + (4 << 20)))),
        name="up_conv_gate",
    )(h2, w_up, w_up, conv_w, conv_w, conv_b.reshape(conv_b.shape[0], 1, 2 * f),
      conv_b.reshape(conv_b.shape[0], 1, 2 * f))


def kernel(x, c, ada_w, ada_b, norm1_g, w_in, q_norm_g, k_norm_g, sgu_norm_g, sgu_w, sgu_b,
           w_branch_a, w_branch_b, w_out, norm2_g, w_up, conv_w, conv_b, w_down):
    batch, seq, d = x.shape
    m = batch * seq
    depth = ada_w.shape[0]
    assert seq % GM_CHUNK == 0 and d % V7X_LANES == 0

    x2 = x.reshape(m, d)
    bp = -(-batch // 8) * 8
    c_pad = jnp.pad(c, ((0, bp - batch), (0, 0)))

    o_q = 0
    o_k = o_q + ATT_WIDTH
    o_v = o_k + KV_WIDTH
    o_qi = o_v + KV_WIDTH
    o_ki = o_qi + IDX_HEADS * IDX_DIM
    o_wi = o_ki + IDX_DIM
    o_gu = o_wi + IDX_HEADS
    o_ga = o_gu + 2 * GM_WIDTH
    o_end = o_ga + 2 * d

    assert o_gu - o_ki <= V7X_LANES and o_end == w_in.shape[2]
    w_in_t = jnp.swapaxes(w_in, 1, 2)

    for l in range(depth):
        mod = _ada(c_pad, ada_w, ada_b, l)
        mod3 = mod.reshape(bp, 1, 6 * d)

        h = _norm_mod(x2, norm1_g[l], mod3, 1, 0, seq)
        q = _proj(h, w_in_t, l, o_q, ATT_WIDTH, out_dtype=BF16, epilogue="headnorm", gain=q_norm_g[l],
                  post_scale=HEAD_DIM ** -0.5 * LOG2E, name="proj_q")
        k = _proj(h, w_in_t, l, o_k, KV_WIDTH, out_dtype=BF16, epilogue="headnorm", gain=k_norm_g[l],
                  name="proj_k")
        v = _proj(h, w_in_t, l, o_v, KV_WIDTH, out_dtype=BF16, name="proj_v")
        qi = _proj(h, w_in_t, l, o_qi, IDX_HEADS * IDX_DIM, out_dtype=BF16, name="proj_qi")
        kw = _proj(h, w_in_t, l, o_ki, V7X_LANES, out_dtype=F32, name="proj_kw")
        uv = _proj(h, w_in_t, l, o_gu, 2 * GM_WIDTH, out_dtype=BF16, epilogue="gelu", name="proj_uv")
        gates = _proj(h, w_in_t, l, o_ga, 2 * d, out_dtype=BF16, epilogue="sigmoid", name="proj_gates")

        y_a = _attention(q, qi, kw, k, v, batch, seq)
        y_b = _sgu(uv, sgu_norm_g[l], sgu_w[l], sgu_b[l], seq)
        merged = _merge(y_a, y_b, w_branch_a, w_branch_b, l, gates)
        x2 = _resid(merged, w_out, l, x2, mod3, 2, seq,
                    tn_prefs=(512, 256, 128), single_buffer_a=False, name="out_proj_resid")

        h2 = _norm_mod(x2, norm2_g[l], mod3, 4, 3, seq)
        act = _up_conv_gate(h2, w_up, l, conv_w, conv_b, seq)
        x2 = _resid(act, w_down[l].astype(BF16), l, x2, mod3, 5, seq,
                    tn_prefs=(256, 128), single_buffer_a=True, name="down_proj_resid")

    return x2.reshape(batch, seq, d)
```

```python
import functools

import numpy as np
import jax
import jax.numpy as jnp
from jax import lax
from jax.experimental import pallas as pl
from jax.experimental.pallas import tpu as pltpu

N_HEADS = 16
HEAD_DIM = 128
N_KV_HEADS = 4
ATT_WIDTH = N_HEADS * HEAD_DIM
KV_WIDTH = N_KV_HEADS * HEAD_DIM
IDX_HEADS = 32
IDX_DIM = 64
TOPK_MAX = 256
GM_WIDTH = 2048
GM_GROUPS = 8
GM_GROUP_W = GM_WIDTH // GM_GROUPS
GM_CHUNK = 128
CONV_W = 3
EPS = 1e-6
NEG_BIG = -1e30

V7X_LANES = 128
V7X_VMEM_BYTES = 64 * 1024 * 1024
VMEM_CAP = V7X_VMEM_BYTES - 8 * 1024 * 1024

BF16 = jnp.bfloat16
F32 = jnp.float32
INT_MIN = -(2 ** 31)


def _vmem_limit(*nbytes):
    return int(min(VMEM_CAP, 2 * sum(nbytes) + (4 << 20)))


def _params(semantics, *nbytes):
    return pltpu.CompilerParams(dimension_semantics=semantics, vmem_limit_bytes=_vmem_limit(*nbytes))


def _pick(n, prefs):
    for p in prefs:
        if n % p == 0:
            return p
    return n


def _ada_kernel(c_ref, w_ref, b_ref, o_ref):
    c = c_ref[...]
    cs = c * jax.nn.sigmoid(c)
    o_ref[...] = jnp.dot(cs.astype(BF16), w_ref[...].astype(BF16),
                         preferred_element_type=F32) + b_ref[...]


def _ada(c_pad, ada_w, ada_b, l):
    bp, d = c_pad.shape
    n = ada_w.shape[2]
    tn = _pick(n, (512, 256, 128))
    return pl.pallas_call(
        _ada_kernel,
        grid=(n // tn,),
        in_specs=[pl.BlockSpec((bp, d), lambda j: (0, 0)),
                  pl.BlockSpec((None, d, tn), lambda j: (l, 0, j)),
                  pl.BlockSpec((None, 1, tn), lambda j: (l, 0, j))],
        out_specs=pl.BlockSpec((bp, tn), lambda j: (0, j)),
        out_shape=jax.ShapeDtypeStruct((bp, n), F32),
        compiler_params=_params(("arbitrary",), 2 * d * tn * 4, d * tn * 2),
        name="ada_mod",
    )(c_pad, ada_w, ada_b.reshape(ada_b.shape[0], 1, n))


def _norm_mod_kernel(x_ref, g_ref, sc_ref, sh_ref, o_ref):
    x = x_ref[...]
    y = x * lax.rsqrt(jnp.mean(x * x, axis=-1, keepdims=True) + EPS) * g_ref[...]
    o_ref[...] = (y * (1.0 + sc_ref[...]) + sh_ref[...]).astype(o_ref.dtype)


def _norm_mod(x2, g, mod3, sc_idx, sh_idx, seq):
    m, d = x2.shape
    tm = _pick(seq, (512, 256, 128, 64, 8))
    per_b = seq // tm
    return pl.pallas_call(
        _norm_mod_kernel,
        grid=(m // tm,),
        in_specs=[pl.BlockSpec((tm, d), lambda i: (i, 0)),
                  pl.BlockSpec((1, d), lambda i: (0, 0)),
                  pl.BlockSpec((None, 1, d), lambda i: (i // per_b, 0, sc_idx)),
                  pl.BlockSpec((None, 1, d), lambda i: (i // per_b, 0, sh_idx))],
        out_specs=pl.BlockSpec((tm, d), lambda i: (i, 0)),
        out_shape=jax.ShapeDtypeStruct((m, d), BF16),
        compiler_params=_params(("arbitrary",), 2 * tm * d * 4, 2 * tm * d * 2),
        name="norm_mod",
    )(x2, g.reshape(1, d), mod3, mod3)


def _gelu_exact(x):
    return 0.5 * x * (1.0 + lax.erf(x * (2.0 ** -0.5)))


MXU_ACC_ROWS = 512


_NT = (((1,), (1,)), ((), ()))


def _proj_kernel(a_ref, w_ref, *rest, epilogue, post_scale):
    o_ref = rest[-1]
    tm, tn = o_ref.shape
    rsub = min(tm, MXU_ACC_ROWS)
    w = w_ref[...].astype(BF16)
    for rb in range(tm // rsub):
        rs = slice(rb * rsub, (rb + 1) * rsub)
        acc = lax.dot_general(a_ref[rs, :], w, _NT, preferred_element_type=F32)
        if epilogue == "plain":
            out = acc
        elif epilogue == "gelu":
            out = _gelu_exact(acc)
        elif epilogue == "sigmoid":
            out = jax.nn.sigmoid(acc)
        elif epilogue == "headnorm":
            gain = rest[0][...]
            parts = []
            for c in range(tn // HEAD_DIM):
                blk = acc[:, c * HEAD_DIM:(c + 1) * HEAD_DIM]
                ms = jnp.mean(blk * blk, axis=-1, keepdims=True)
                parts.append(blk * lax.rsqrt(ms + EPS) * (gain * post_scale))
            out = jnp.concatenate(parts, axis=1) if len(parts) > 1 else parts[0]
        else:
            raise ValueError(epilogue)
        o_ref[rs, :] = out.astype(o_ref.dtype)


def _proj(a, w_t, l, row0, n, *, out_dtype, epilogue="plain", gain=None, post_scale=1.0, name):
    m, k = a.shape
    assert row0 % 8 == 0 and w_t.shape[2] == k
    tm = _pick(m, (1024, 512, 256, 128))
    tn = _pick(n, (512, 256, 128))
    in_specs = [pl.BlockSpec((tm, k), lambda i, j: (i, 0)),
                pl.BlockSpec((None, pl.Element(tn), pl.Element(k)),
                             lambda i, j: (l, pl.multiple_of(row0 + j * tn, 8), 0))]
    args = [a, w_t]
    if epilogue == "headnorm":
        in_specs.append(pl.BlockSpec((1, HEAD_DIM), lambda i, j: (0, 0)))
        args.append(gain.reshape(1, HEAD_DIM))
    osz = jnp.dtype(out_dtype).itemsize
    return pl.pallas_call(
        functools.partial(_proj_kernel, epilogue=epilogue, post_scale=post_scale),
        grid=(m // tm, n // tn),
        in_specs=in_specs,
        out_specs=pl.BlockSpec((tm, tn), lambda i, j: (i, j)),
        out_shape=jax.ShapeDtypeStruct((m, n), out_dtype),
        compiler_params=pltpu.CompilerParams(
            dimension_semantics=("arbitrary", "arbitrary"),
            vmem_limit_bytes=int(min(VMEM_CAP, 2 * tm * k * 2 + tn * k * (2 * 4 + 2) + 2 * tm * tn * osz
                                     + 3 * tm * tn * 4 + (4 << 20)))),
        name=name,
    )(*args)


def _alibi_slopes(n):
    return [2.0 ** (-8.0 * (i + 1) / n) for i in range(n)]


LOG2E = 1.4426950408889634
ALIBI_PIECES = 4
POS_SPLIT = 256


def _alibi_tables(seq):
    assert seq <= POS_SPLIT * POS_SPLIT and 2 * ALIBI_PIECES <= V7X_LANES
    slope_tab = np.zeros((N_HEADS, V7X_LANES), np.float32)
    for h, s in enumerate(_alibi_slopes(N_HEADS)):
        rest = float(np.float32(s)) * LOG2E
        for p in range(ALIBI_PIECES):
            piece = float(np.asarray(rest, dtype=BF16))
            slope_tab[h, 2 * p:2 * p + 2] = piece
            rest -= piece
    pos = np.arange(seq)
    pos_tab = np.zeros((seq, V7X_LANES), np.float32)
    pos_tab[:, 0:2 * ALIBI_PIECES:2] = ((pos // POS_SPLIT) * POS_SPLIT)[:, None]
    pos_tab[:, 1:2 * ALIBI_PIECES:2] = (pos % POS_SPLIT)[:, None]
    return jnp.asarray(slope_tab), jnp.asarray(pos_tab, dtype=BF16)


def _attn_kernel(q_ref, qi_ref, kwq_ref, k_ref, v_ref, kwa_ref, slope_ref, pos_ref, y_ref,
                 keys_scr, bias_scr, wt_scr, sel_scr, vt_scr, qa_scr, ml_scr, acc_scr, *, tq, seq, topk):
    i = pl.program_id(1)
    n_chunks = i + 1
    t0 = i * tq
    rep = N_HEADS // N_KV_HEADS
    nt = (((1,), (1,)), ((), ()))
    row = lax.broadcasted_iota(jnp.int32, (tq, tq), 0)
    col = lax.broadcasted_iota(jnp.int32, (tq, tq), 1)
    kf = float(topk)

    w_fold = (IDX_HEADS ** -0.5) * (IDX_DIM ** -0.5)
    wt_scr[...] = (kwq_ref[...] * w_fold).T
    lane = lax.broadcasted_iota(jnp.int32, (tq, V7X_LANES), 1)

    def score_body(c, carry):
        off = pl.multiple_of(c * tq, tq)
        kraw = kwa_ref[pl.ds(off, tq), :]
        k_even = jnp.where(lane < IDX_DIM, kraw, 0.0)
        k_odd = jnp.where(lane >= IDX_DIM, pltpu.roll(kraw, IDX_DIM, 1), 0.0)
        lhs = jnp.concatenate([k_even, k_odd], axis=0).astype(BF16)
        acc = jnp.zeros((tq, tq), F32)
        for p in range(IDX_HEADS // 2):
            lg = lax.dot_general(lhs, qi_ref[:, p * V7X_LANES:(p + 1) * V7X_LANES], nt,
                                 preferred_element_type=F32)
            h0 = IDX_DIM + 2 * p
            acc = acc + wt_scr[h0:h0 + 1, :] * jnp.maximum(lg[:tq], 0.0)
            acc = acc + wt_scr[h0 + 1:h0 + 2, :] * jnp.maximum(lg[tq:], 0.0)
        bits = lax.bitcast_convert_type(acc, jnp.int32)
        key = bits ^ ((bits >> 31) & jnp.int32(0x7FFFFFFF))
        keys_scr[c] = jnp.where(off + row <= t0 + col, key, jnp.int32(INT_MIN))
        return carry

    lax.fori_loop(0, n_chunks, score_body, 0)

    def count(pred):
        def body(c, cnt):
            off = c * tq
            return cnt + jnp.sum(jnp.where(pred(keys_scr[c], off + row), 1.0, 0.0),
                                 axis=0, keepdims=True)
        return lax.fori_loop(0, n_chunks, body, jnp.zeros((1, tq), F32))

    def select_topk():
        cnt_nonneg = count(lambda kb, s: kb >= 0)
        prefix0 = jnp.where(cnt_nonneg >= kf, jnp.int32(0), jnp.int32(INT_MIN))

        def bit_body(j, prefix):
            cand = prefix | lax.shift_left(jnp.int32(1), 30 - j)
            cnt = count(lambda kb, s: kb >= cand)
            return jnp.where(cnt >= kf, cand, prefix)

        thr = lax.fori_loop(0, 31, bit_body, prefix0)
        cnt_gt = count(lambda kb, s: kb > thr)
        cnt_ge = count(lambda kb, s: kb >= thr)
        need = kf - cnt_gt
        sel_scr[0:1, :] = thr
        sel_scr[1:2, :] = jnp.full((1, tq), seq, jnp.int32)

        @pl.when(jnp.max(cnt_ge) > kf)
        def _():
            nbits = max(1, (seq - 1).bit_length())

            def idx_body(j, m):
                cand = m | lax.shift_left(jnp.int32(1), nbits - 1 - j)
                cnt = count(lambda kb, s: (kb == thr) & (s < cand))
                return jnp.where(cnt < need, cand, m)

            sel_scr[1:2, :] = lax.fori_loop(0, nbits, idx_body, jnp.zeros((1, tq), jnp.int32))

    if tq <= topk:
        @pl.when(i == 0)
        def _():
            sel_scr[0:1, :] = jnp.full((1, tq), INT_MIN, jnp.int32)
            sel_scr[1:2, :] = jnp.full((1, tq), -1, jnp.int32)

        pl.when(i > 0)(select_topk)
    else:
        select_topk()

    thr = sel_scr[0:1, :]
    m_idx = sel_scr[1:2, :]

    def bias_body(c, carry):
        off = c * tq
        kb = keys_scr[c]
        s_idx = off + row
        take = jnp.where(kb > thr, 1.0, jnp.where((kb == thr) & (s_idx <= m_idx), 1.0, 0.0))
        take = jnp.where(s_idx <= t0 + col, take, 0.0)
        bias_scr[c] = jnp.where(take > 0.5, 0.0, NEG_BIG)
        return carry

    lax.fori_loop(0, n_chunks, bias_body, 0)

    @pl.when(i == 0)
    def _():
        for c in range(seq // tq):
            for g in range(N_KV_HEADS):
                blk = v_ref[c * tq:(c + 1) * tq, g * HEAD_DIM:(g + 1) * HEAD_DIM].astype(F32)
                vt_scr[c, g * HEAD_DIM:(g + 1) * HEAD_DIM, :] = blk.T.astype(BF16)

    for h in range(N_HEADS):
        g, r = divmod(h, rep)
        qa_scr[g, r * tq:(r + 1) * tq, :HEAD_DIM] = q_ref[:, h * HEAD_DIM:(h + 1) * HEAD_DIM]
        qa_scr[g, r * tq:(r + 1) * tq, HEAD_DIM:] = jnp.broadcast_to(
            slope_ref[h:h + 1, :], (tq, V7X_LANES)).astype(BF16)
    acc_scr[...] = jnp.zeros(acc_scr.shape, F32)
    for g in range(N_KV_HEADS):
        ml_scr[2 * g:2 * g + 1, :] = jnp.full((1, rep * tq), NEG_BIG, F32)
        ml_scr[2 * g + 1:2 * g + 2, :] = jnp.zeros((1, rep * tq), F32)

    def attn_body(c, carry):
        off = pl.multiple_of(c * tq, tq)
        pos_c = pos_ref[pl.ds(off, tq), :]
        bias = jnp.concatenate([bias_scr[c]] * rep, axis=1)
        def scores(g):
            kc = jnp.concatenate([k_ref[pl.ds(off, tq), g * HEAD_DIM:(g + 1) * HEAD_DIM], pos_c],
                                 axis=1)
            return lax.dot_general(kc, qa_scr[g], nt, preferred_element_type=F32) + bias

        lead = 2
        sts = [scores(g) for g in range(lead)]
        for g in range(N_KV_HEADS):
            st = sts[g]
            if g + lead < N_KV_HEADS:
                sts.append(scores(g + lead))
            vt = vt_scr[c, g * HEAD_DIM:(g + 1) * HEAD_DIM, :]
            m_prev = ml_scr[2 * g:2 * g + 1, :]
            m_new = jnp.maximum(m_prev, jnp.max(st, axis=0, keepdims=True))
            alpha = jnp.exp2(m_prev - m_new)
            pt = jnp.exp2(st - m_new)
            ml_scr[2 * g:2 * g + 1, :] = m_new
            ml_scr[2 * g + 1:2 * g + 2, :] = (alpha * ml_scr[2 * g + 1:2 * g + 2, :]
                                              + jnp.sum(pt, axis=0, keepdims=True))
            acc_scr[g] = alpha * acc_scr[g] + jnp.dot(vt, pt.astype(BF16), preferred_element_type=F32)
        return carry

    lax.fori_loop(0, n_chunks, attn_body, 0)
    for g in range(N_KV_HEADS):
        o_t = acc_scr[g] / ml_scr[2 * g + 1:2 * g + 2, :]
        for r in range(rep):
            h = g * rep + r
            y_ref[:, h * HEAD_DIM:(h + 1) * HEAD_DIM] = o_t[:, r * tq:(r + 1) * tq].T.astype(y_ref.dtype)


def _attention(q, qi, kw, k, v, batch, seq):
    m = q.shape[0]
    topk = min(TOPK_MAX, seq // 4)
    tq = _pick(seq, (256, 128))
    nq = seq // tq
    kww = kw.shape[1]
    rep = N_HEADS // N_KV_HEADS
    assert HEAD_DIM == V7X_LANES and tq % V7X_LANES == 0
    assert kww == V7X_LANES == 2 * IDX_DIM and IDX_DIM + IDX_HEADS <= V7X_LANES
    return pl.pallas_call(
        functools.partial(_attn_kernel, tq=tq, seq=seq, topk=topk),
        grid=(batch, nq),
        in_specs=[pl.BlockSpec((tq, ATT_WIDTH), lambda b, i: (b * nq + i, 0)),
                  pl.BlockSpec((tq, IDX_HEADS * IDX_DIM), lambda b, i: (b * nq + i, 0)),
                  pl.BlockSpec((tq, kww), lambda b, i: (b * nq + i, 0)),
                  pl.BlockSpec((seq, KV_WIDTH), lambda b, i: (b, 0)),
                  pl.BlockSpec((seq, KV_WIDTH), lambda b, i: (b, 0)),
                  pl.BlockSpec((seq, kww), lambda b, i: (b, 0)),
                  pl.BlockSpec((N_HEADS, V7X_LANES), lambda b, i: (0, 0)),
                  pl.BlockSpec((seq, V7X_LANES), lambda b, i: (0, 0))],
        out_specs=pl.BlockSpec((tq, ATT_WIDTH), lambda b, i: (b * nq + i, 0)),
        out_shape=jax.ShapeDtypeStruct((m, ATT_WIDTH), BF16),
        scratch_shapes=[pltpu.VMEM((nq, tq, tq), jnp.int32),
                        pltpu.VMEM((nq, tq, tq), F32),
                        pltpu.VMEM((V7X_LANES, tq), F32),
                        pltpu.VMEM((8, tq), jnp.int32),
                        pltpu.VMEM((nq, KV_WIDTH, tq), BF16),
                        pltpu.VMEM((N_KV_HEADS, rep * tq, 2 * HEAD_DIM), BF16),
                        pltpu.VMEM((2 * N_KV_HEADS, rep * tq), F32),
                        pltpu.VMEM((N_KV_HEADS, HEAD_DIM, rep * tq), F32)],
        compiler_params=_params(("arbitrary", "arbitrary"),
                                6 * tq * ATT_WIDTH * 2, 4 * seq * KV_WIDTH * 2, 2 * (seq + tq) * kww * 4,
                                2 * seq * tq * 4, 8 * tq * tq * 4, seq * KV_WIDTH * 2, rep * HEAD_DIM * tq * 4),
        name="dsa_attention",
    )(q, qi, kw, k, v, kw, *_alibi_tables(seq))


def _sgu_kernel(u_ref, v_ref, g_ref, w_ref, bt_ref, o_ref, *, rows):
    v = v_ref[...].astype(F32)
    vn = (v * lax.rsqrt(jnp.mean(v * v, axis=-1, keepdims=True) + EPS) * g_ref[...]).astype(BF16)
    r_i = lax.broadcasted_iota(jnp.int32, (GM_CHUNK, GM_CHUNK), 0)
    c_i = lax.broadcasted_iota(jnp.int32, (GM_CHUNK, GM_CHUNK), 1)
    for g in range(GM_GROUPS):
        w = jnp.where(r_i >= c_i, w_ref[g], 0.0).astype(BF16)
        bcol = bt_ref[:, g:g + 1]
        cs = slice(g * GM_GROUP_W, (g + 1) * GM_GROUP_W)
        for n in range(rows // GM_CHUNK):
            rs = slice(n * GM_CHUNK, (n + 1) * GM_CHUNK)
            f = jnp.dot(w, vn[rs, cs], preferred_element_type=F32) + bcol
            o_ref[rs, cs] = (u_ref[rs, cs].astype(F32) * f).astype(o_ref.dtype)


def _sgu(uv, gain, w_s, b_s, seq):
    m = uv.shape[0]
    rows = _pick(seq, (512, 256, 128))
    return pl.pallas_call(
        functools.partial(_sgu_kernel, rows=rows),
        grid=(m // rows,),
        in_specs=[pl.BlockSpec((rows, GM_WIDTH), lambda i: (i, 0)),
                  pl.BlockSpec((rows, GM_WIDTH), lambda i: (i, 1)),
                  pl.BlockSpec((1, GM_WIDTH), lambda i: (0, 0)),
                  pl.BlockSpec((GM_GROUPS, GM_CHUNK, GM_CHUNK), lambda i: (0, 0, 0)),
                  pl.BlockSpec((GM_CHUNK, GM_GROUPS), lambda i: (0, 0))],
        out_specs=pl.BlockSpec((rows, GM_WIDTH), lambda i: (i, 0)),
        out_shape=jax.ShapeDtypeStruct((m, GM_WIDTH), BF16),
        compiler_params=_params(("arbitrary",), 6 * rows * GM_WIDTH * 2, 2 * rows * GM_WIDTH * 4),
        name="sgu",
    )(uv, uv, gain.reshape(1, GM_WIDTH), w_s, b_s.T)


def _w_spec(w, l, k, tn, j0=0):
    if w.ndim == 3:
        return pl.BlockSpec((None, k, tn), lambda i, j: (l, 0, j + j0))
    return pl.BlockSpec((k, tn), lambda i, j: (0, j + j0))


def _merge_kernel(ya_ref, yb_ref, wa_ref, wb_ref, ga_ref, gb_ref, o_ref):
    pa = jnp.dot(ya_ref[...], wa_ref[...].astype(BF16), preferred_element_type=F32)
    pb = jnp.dot(yb_ref[...], wb_ref[...].astype(BF16), preferred_element_type=F32)
    o_ref[...] = (ga_ref[...].astype(F32) * pa + gb_ref[...].astype(F32) * pb).astype(o_ref.dtype)


def _merge(ya, yb, wa, wb, l, gates):
    m, ka = ya.shape
    kb = yb.shape[1]
    d = wa.shape[-1]
    tm = _pick(m, (1024, 512, 256, 128))
    tn = _pick(d, (512, 256, 128))
    nj = d // tn
    wsz = wa.dtype.itemsize
    return pl.pallas_call(
        _merge_kernel,
        grid=(m // tm, nj),
        in_specs=[pl.BlockSpec((tm, ka), lambda i, j: (i, 0)),
                  pl.BlockSpec((tm, kb), lambda i, j: (i, 0)),
                  _w_spec(wa, l, ka, tn),
                  _w_spec(wb, l, kb, tn),
                  pl.BlockSpec((tm, tn), lambda i, j: (i, j)),
                  pl.BlockSpec((tm, tn), lambda i, j: (i, j + nj))],
        out_specs=pl.BlockSpec((tm, tn), lambda i, j: (i, j)),
        out_shape=jax.ShapeDtypeStruct((m, d), BF16),
        compiler_params=pltpu.CompilerParams(
            dimension_semantics=("arbitrary", "arbitrary"),
            vmem_limit_bytes=int(min(VMEM_CAP, 2 * tm * (ka + kb) * 2 + (ka + kb) * tn * (2 * wsz + 2)
                                     + 6 * tm * tn * 2 + 3 * tm * tn * 4 + (4 << 20)))),
        name="merge",
    )(ya, yb, wa, wb, gates, gates)


def _resid_kernel(a_ref, w_ref, x_ref, g_ref, o_ref):
    acc = jnp.dot(a_ref[...], w_ref[...].astype(BF16), preferred_element_type=F32)
    o_ref[...] = x_ref[...] + g_ref[...] * acc


def _resid(a, w, l, x2, mod3, g_idx, seq, *, tn_prefs, single_buffer_a, name):
    m, k = a.shape
    d = w.shape[-1]
    tm = _pick(seq, (1024, 512, 256, 128))
    per_b = seq // tm
    tn = _pick(d, tn_prefs)
    nj = d // tn
    a_bufs = 1 if single_buffer_a else 2
    wsz = w.dtype.itemsize
    a_spec = (pl.BlockSpec((tm, k), lambda i, j: (i, 0), pipeline_mode=pl.Buffered(1))
              if single_buffer_a else pl.BlockSpec((tm, k), lambda i, j: (i, 0)))
    return pl.pallas_call(
        _resid_kernel,
        grid=(m // tm, nj),
        in_specs=[a_spec,
                  _w_spec(w, l, k, tn),
                  pl.BlockSpec((tm, tn), lambda i, j: (i, j)),
                  pl.BlockSpec((None, 1, tn), lambda i, j: (i // per_b, 0, g_idx * nj + j))],
        out_specs=pl.BlockSpec((tm, tn), lambda i, j: (i, j)),
        out_shape=jax.ShapeDtypeStruct((m, d), F32),
        compiler_params=pltpu.CompilerParams(
            dimension_semantics=("arbitrary", "arbitrary"),
            vmem_limit_bytes=int(min(VMEM_CAP, a_bufs * tm * k * 2 + k * tn * (2 * wsz + 2)
                                     + 6 * tm * tn * 4 + (4 << 20)))),
        name=name,
    )(a, w, x2, mod3)


HALO = 8


def _up_kernel(h_ref, wg_ref, wv_ref, cwg_ref, cwv_ref, cbg_ref, cbv_ref, wd_ref, o_ref, wd_o_ref, *, sub):
    wd_o_ref[...] = wd_ref[...].astype(BF16)
    wg = wg_ref[...].astype(BF16)
    wv = wv_ref[...].astype(BF16)
    tn = wg.shape[1]

    def conv(a, halo, cw_ref, cb_ref):
        ext = jnp.concatenate([halo, a], axis=0)
        acc = cb_ref[...] + pltpu.roll(ext, 2, 0)[HALO:] * cw_ref[0:1, :]
        acc = acc + pltpu.roll(ext, 1, 0)[HALO:] * cw_ref[1:2, :]
        return acc + a * cw_ref[2:3, :]

    halo_g = halo_v = jnp.zeros((HALO, tn), F32)
    for s in range(h_ref.shape[0] // sub):
        rs = slice(s * sub, (s + 1) * sub)
        hs = h_ref[rs, :]
        ag = jnp.dot(hs, wg, preferred_element_type=F32)
        av = jnp.dot(hs, wv, preferred_element_type=F32)
        gate = conv(ag, halo_g, cwg_ref, cbg_ref)
        val = conv(av, halo_v, cwv_ref, cbv_ref)
        o_ref[rs, :] = (gate * jax.nn.sigmoid(gate) * val).astype(o_ref.dtype)
        halo_g, halo_v = ag[sub - HALO:], av[sub - HALO:]


BF16_SUBLANES = 16


def _up_conv_gate(h2, w_up, l, conv_w, conv_b, w_down, seq):
    m, k = h2.shape
    f = w_up.shape[-1] // 2
    d_out = w_down.shape[-1]
    tn = _pick(f, (256, 128))
    nj = f // tn
    n_steps = (m // seq) * nj
    assert f % (n_steps * BF16_SUBLANES) == 0
    slab = f // n_steps
    sub = _pick(seq, (512, 256, 128))
    wsz = w_up.dtype.itemsize
    return pl.pallas_call(
        functools.partial(_up_kernel, sub=sub),
        grid=(m // seq, nj),
        in_specs=[pl.BlockSpec((seq, k), lambda i, j: (i, 0), pipeline_mode=pl.Buffered(1)),
                  _w_spec(w_up, l, k, tn),
                  _w_spec(w_up, l, k, tn, nj),
                  pl.BlockSpec((None, CONV_W, tn), lambda i, j: (l, 0, j)),
                  pl.BlockSpec((None, CONV_W, tn), lambda i, j: (l, 0, j + nj)),
                  pl.BlockSpec((None, 1, tn), lambda i, j: (l, 0, j)),
                  pl.BlockSpec((None, 1, tn), lambda i, j: (l, 0, j + nj)),
                  pl.BlockSpec((None, slab, d_out), lambda i, j: (l, i * nj + j, 0))],
        out_specs=[pl.BlockSpec((seq, tn), lambda i, j: (i, j)),
                   pl.BlockSpec((slab, d_out), lambda i, j: (i * nj + j, 0))],
        out_shape=[jax.ShapeDtypeStruct((m, f), BF16),
                   jax.ShapeDtypeStruct((f, d_out), BF16)],
        compiler_params=pltpu.CompilerParams(
            dimension_semantics=("arbitrary", "arbitrary"),
            vmem_limit_bytes=int(min(VMEM_CAP, seq * k * 2 + 2 * k * tn * (2 * wsz + 2) + 2 * seq * tn * 2
                                     + 12 * sub * tn * 4 + 2 * slab * d_out * 6 + (4 << 20)))),
        name="up_conv_gate",
    )(h2, w_up, w_up, conv_w, conv_w, conv_b.reshape(conv_b.shape[0], 1, 2 * f),
      conv_b.reshape(conv_b.shape[0], 1, 2 * f), w_down)


def kernel(x, c, ada_w, ada_b, norm1_g, w_in, q_norm_g, k_norm_g, sgu_norm_g, sgu_w, sgu_b,
           w_branch_a, w_branch_b, w_out, norm2_g, w_up, conv_w, conv_b, w_down):
    batch, seq, d = x.shape
    m = batch * seq
    depth = ada_w.shape[0]
    assert seq % GM_CHUNK == 0 and d % V7X_LANES == 0

    x2 = x.reshape(m, d)
    bp = -(-batch // 8) * 8
    c_pad = jnp.pad(c, ((0, bp - batch), (0, 0)))

    o_q = 0
    o_k = o_q + ATT_WIDTH
    o_v = o_k + KV_WIDTH
    o_qi = o_v + KV_WIDTH
    o_ki = o_qi + IDX_HEADS * IDX_DIM
    o_wi = o_ki + IDX_DIM
    o_gu = o_wi + IDX_HEADS
    o_ga = o_gu + 2 * GM_WIDTH
    o_end = o_ga + 2 * d

    assert o_gu - o_ki <= V7X_LANES and o_end == w_in.shape[2]
    w_in_t = jnp.swapaxes(w_in, 1, 2)

    for l in range(depth):
        mod = _ada(c_pad, ada_w, ada_b, l)
        mod3 = mod.reshape(bp, 1, 6 * d)

        h = _norm_mod(x2, norm1_g[l], mod3, 1, 0, seq)
        q = _proj(h, w_in_t, l, o_q, ATT_WIDTH, out_dtype=BF16, epilogue="headnorm", gain=q_norm_g[l],
                  post_scale=HEAD_DIM ** -0.5 * LOG2E, name="proj_q")
        k = _proj(h, w_in_t, l, o_k, KV_WIDTH, out_dtype=BF16, epilogue="headnorm", gain=k_norm_g[l],
                  name="proj_k")
        v = _proj(h, w_in_t, l, o_v, KV_WIDTH, out_dtype=BF16, name="proj_v")
        qi = _proj(h, w_in_t, l, o_qi, IDX_HEADS * IDX_DIM, out_dtype=BF16, name="proj_qi")
        kw = _proj(h, w_in_t, l, o_ki, V7X_LANES, out_dtype=F32, name="proj_kw")
        uv = _proj(h, w_in_t, l, o_gu, 2 * GM_WIDTH, out_dtype=BF16, epilogue="gelu", name="proj_uv")
        gates = _proj(h, w_in_t, l, o_ga, 2 * d, out_dtype=BF16, epilogue="sigmoid", name="proj_gates")

        y_a = _attention(q, qi, kw, k, v, batch, seq)
        y_b = _sgu(uv, sgu_norm_g[l], sgu_w[l], sgu_b[l], seq)
        merged = _merge(y_a, y_b, w_branch_a, w_branch_b, l, gates)
        x2 = _resid(merged, w_out, l, x2, mod3, 2, seq,
                    tn_prefs=(512, 256, 128), single_buffer_a=False, name="out_proj_resid")

        h2 = _norm_mod(x2, norm2_g[l], mod3, 4, 3, seq)
        act, w_down_bf = _up_conv_gate(h2, w_up, l, conv_w, conv_b, w_down, seq)
        x2 = _resid(act, w_down_bf, l, x2, mod3, 5, seq,
                    tn_prefs=(256, 128), single_buffer_a=True, name="down_proj_resid")

    return x2.reshape(batch, seq, d)
```

```python
import functools

import numpy as np
import jax
import jax.numpy as jnp
from jax import lax
from jax.experimental import pallas as pl
from jax.experimental.pallas import tpu as pltpu

N_HEADS = 16
HEAD_DIM = 128
N_KV_HEADS = 4
ATT_WIDTH = N_HEADS * HEAD_DIM
KV_WIDTH = N_KV_HEADS * HEAD_DIM
IDX_HEADS = 32
IDX_DIM = 64
TOPK_MAX = 256
GM_WIDTH = 2048
GM_GROUPS = 8
GM_GROUP_W = GM_WIDTH // GM_GROUPS
GM_CHUNK = 128
CONV_W = 3
EPS = 1e-6
NEG_BIG = -1e30

V7X_LANES = 128
V7X_VMEM_BYTES = 64 * 1024 * 1024
VMEM_CAP = V7X_VMEM_BYTES - 8 * 1024 * 1024

BF16 = jnp.bfloat16
F32 = jnp.float32
INT_MIN = -(2 ** 31)


def _vmem_limit(*nbytes):
    return int(min(VMEM_CAP, 2 * sum(nbytes) + (4 << 20)))


def _params(semantics, *nbytes):
    return pltpu.CompilerParams(dimension_semantics=semantics, vmem_limit_bytes=_vmem_limit(*nbytes))


def _pick(n, prefs):
    for p in prefs:
        if n % p == 0:
            return p
    return n


def _ada_kernel(c_ref, w_ref, b_ref, o_ref):
    c = c_ref[...]
    cs = c * jax.nn.sigmoid(c)
    o_ref[...] = jnp.dot(cs.astype(BF16), w_ref[...].astype(BF16),
                         preferred_element_type=F32) + b_ref[...]


def _ada(c_pad, ada_w, ada_b, l):
    bp, d = c_pad.shape
    n = ada_w.shape[2]
    tn = _pick(n, (512, 256, 128))
    return pl.pallas_call(
        _ada_kernel,
        grid=(n // tn,),
        in_specs=[pl.BlockSpec((bp, d), lambda j: (0, 0)),
                  pl.BlockSpec((None, d, tn), lambda j: (l, 0, j)),
                  pl.BlockSpec((None, 1, tn), lambda j: (l, 0, j))],
        out_specs=pl.BlockSpec((bp, tn), lambda j: (0, j)),
        out_shape=jax.ShapeDtypeStruct((bp, n), F32),
        compiler_params=_params(("arbitrary",), 2 * d * tn * 4, d * tn * 2),
        name="ada_mod",
    )(c_pad, ada_w, ada_b.reshape(ada_b.shape[0], 1, n))


def _norm_mod_kernel(x_ref, g_ref, sc_ref, sh_ref, o_ref):
    x = x_ref[...]
    y = x * lax.rsqrt(jnp.mean(x * x, axis=-1, keepdims=True) + EPS) * g_ref[...]
    o_ref[...] = (y * (1.0 + sc_ref[...]) + sh_ref[...]).astype(o_ref.dtype)


def _norm_mod(x2, g, mod3, sc_idx, sh_idx, seq):
    m, d = x2.shape
    tm = _pick(seq, (512, 256, 128, 64, 8))
    per_b = seq // tm
    return pl.pallas_call(
        _norm_mod_kernel,
        grid=(m // tm,),
        in_specs=[pl.BlockSpec((tm, d), lambda i: (i, 0)),
                  pl.BlockSpec((1, d), lambda i: (0, 0)),
                  pl.BlockSpec((None, 1, d), lambda i: (i // per_b, 0, sc_idx)),
                  pl.BlockSpec((None, 1, d), lambda i: (i // per_b, 0, sh_idx))],
        out_specs=pl.BlockSpec((tm, d), lambda i: (i, 0)),
        out_shape=jax.ShapeDtypeStruct((m, d), BF16),
        compiler_params=_params(("arbitrary",), 2 * tm * d * 4, 2 * tm * d * 2),
        name="norm_mod",
    )(x2, g.reshape(1, d), mod3, mod3)


def _gelu_exact(x):
    return 0.5 * x * (1.0 + lax.erf(x * (2.0 ** -0.5)))


MXU_ACC_ROWS = 512


_NT = (((1,), (1,)), ((), ()))


def _proj_kernel(a_ref, w_ref, *rest, epilogue, post_scale):
    o_ref = rest[-1]
    tm, tn = o_ref.shape
    rsub = min(tm, MXU_ACC_ROWS)
    w = w_ref[...].astype(BF16)
    for rb in range(tm // rsub):
        rs = slice(rb * rsub, (rb + 1) * rsub)
        acc = lax.dot_general(a_ref[rs, :], w, _NT, preferred_element_type=F32)
        if epilogue == "plain":
            out = acc
        elif epilogue == "gelu":
            out = _gelu_exact(acc)
        elif epilogue == "sigmoid":
            out = jax.nn.sigmoid(acc)
        elif epilogue == "headnorm":
            gain = rest[0][...]
            parts = []
            for c in range(tn // HEAD_DIM):
                blk = acc[:, c * HEAD_DIM:(c + 1) * HEAD_DIM]
                ms = jnp.mean(blk * blk, axis=-1, keepdims=True)
                parts.append(blk * lax.rsqrt(ms + EPS) * (gain * post_scale))
            out = jnp.concatenate(parts, axis=1) if len(parts) > 1 else parts[0]
        else:
            raise ValueError(epilogue)
        o_ref[rs, :] = out.astype(o_ref.dtype)


def _proj(a, w_t, l, row0, n, *, out_dtype, epilogue="plain", gain=None, post_scale=1.0, name):
    m, k = a.shape
    assert row0 % 8 == 0 and w_t.shape[2] == k
    tm = _pick(m, (1024, 512, 256, 128))
    tn = _pick(n, (512, 256, 128))
    in_specs = [pl.BlockSpec((tm, k), lambda i, j: (i, 0)),
                pl.BlockSpec((None, pl.Element(tn), pl.Element(k)),
                             lambda i, j: (l, pl.multiple_of(row0 + j * tn, 8), 0))]
    args = [a, w_t]
    if epilogue == "headnorm":
        in_specs.append(pl.BlockSpec((1, HEAD_DIM), lambda i, j: (0, 0)))
        args.append(gain.reshape(1, HEAD_DIM))
    osz = jnp.dtype(out_dtype).itemsize
    return pl.pallas_call(
        functools.partial(_proj_kernel, epilogue=epilogue, post_scale=post_scale),
        grid=(m // tm, n // tn),
        in_specs=in_specs,
        out_specs=pl.BlockSpec((tm, tn), lambda i, j: (i, j)),
        out_shape=jax.ShapeDtypeStruct((m, n), out_dtype),
        compiler_params=pltpu.CompilerParams(
            dimension_semantics=("arbitrary", "arbitrary"),
            vmem_limit_bytes=int(min(VMEM_CAP, 2 * tm * k * 2 + tn * k * (2 * 4 + 2) + 2 * tm * tn * osz
                                     + 3 * tm * tn * 4 + (4 << 20)))),
        name=name,
    )(*args)


def _alibi_slopes(n):
    return [2.0 ** (-8.0 * (i + 1) / n) for i in range(n)]


F32_SUBLANES = 8


def _fold_rows(x, op):
    rows = x.shape[0]
    while rows > F32_SUBLANES and rows % (2 * F32_SUBLANES) == 0:
        rows //= 2
        x = op(x[:rows], x[rows:])
    return x


def _reduce_rows(x, op):
    return (jnp.max if op is jnp.maximum else jnp.sum)(_fold_rows(x, op), axis=0, keepdims=True)


LOG2E = 1.4426950408889634
ALIBI_PIECES = 4
POS_SPLIT = 256


def _alibi_tables(seq):
    assert seq <= POS_SPLIT * POS_SPLIT and 2 * ALIBI_PIECES <= V7X_LANES
    slope_tab = np.zeros((N_HEADS, V7X_LANES), np.float32)
    for h, s in enumerate(_alibi_slopes(N_HEADS)):
        rest = float(np.float32(s)) * LOG2E
        for p in range(ALIBI_PIECES):
            piece = float(np.asarray(rest, dtype=BF16))
            slope_tab[h, 2 * p:2 * p + 2] = piece
            rest -= piece
    pos = np.arange(seq)
    pos_tab = np.zeros((seq, V7X_LANES), np.float32)
    pos_tab[:, 0:2 * ALIBI_PIECES:2] = ((pos // POS_SPLIT) * POS_SPLIT)[:, None]
    pos_tab[:, 1:2 * ALIBI_PIECES:2] = (pos % POS_SPLIT)[:, None]
    return jnp.asarray(slope_tab), jnp.asarray(pos_tab, dtype=BF16)


def _attn_kernel(q_ref, qi_ref, kwq_ref, k_ref, v_ref, kwa_ref, slope_ref, pos_ref, y_ref,
                 keys_scr, bias_scr, wt_scr, sel_scr, vt_scr, qa_scr, ml_scr, acc_scr, *, tq, seq, topk):
    i = pl.program_id(1)
    n_chunks = i + 1
    t0 = i * tq
    rep = N_HEADS // N_KV_HEADS
    nt = (((1,), (1,)), ((), ()))
    row = lax.broadcasted_iota(jnp.int32, (tq, tq), 0)
    col = lax.broadcasted_iota(jnp.int32, (tq, tq), 1)
    kf = float(topk)

    w_fold = (IDX_HEADS ** -0.5) * (IDX_DIM ** -0.5)
    wt_scr[...] = (kwq_ref[...] * w_fold).T
    lane = lax.broadcasted_iota(jnp.int32, (tq, V7X_LANES), 1)

    def score_body(c, carry):
        off = pl.multiple_of(c * tq, tq)
        kraw = kwa_ref[pl.ds(off, tq), :]
        k_even = jnp.where(lane < IDX_DIM, kraw, 0.0)
        k_odd = jnp.where(lane >= IDX_DIM, pltpu.roll(kraw, IDX_DIM, 1), 0.0)
        lhs = jnp.concatenate([k_even, k_odd], axis=0).astype(BF16)
        acc = jnp.zeros((tq, tq), F32)
        for p in range(IDX_HEADS // 2):
            lg = lax.dot_general(lhs, qi_ref[:, p * V7X_LANES:(p + 1) * V7X_LANES], nt,
                                 preferred_element_type=F32)
            h0 = IDX_DIM + 2 * p
            acc = acc + wt_scr[h0:h0 + 1, :] * jnp.maximum(lg[:tq], 0.0)
            acc = acc + wt_scr[h0 + 1:h0 + 2, :] * jnp.maximum(lg[tq:], 0.0)
        bits = lax.bitcast_convert_type(acc, jnp.int32)
        key = bits ^ ((bits >> 31) & jnp.int32(0x7FFFFFFF))
        keys_scr[c] = jnp.where(off + row <= t0 + col, key, jnp.int32(INT_MIN))
        return carry

    lax.fori_loop(0, n_chunks, score_body, 0)

    def count(pred):
        def body(c, part):
            off = c * tq
            return part + _fold_rows(jnp.where(pred(keys_scr[c], off + row), 1.0, 0.0), jnp.add)
        part = lax.fori_loop(0, n_chunks, body, jnp.zeros((F32_SUBLANES, tq), F32))
        return jnp.sum(part, axis=0, keepdims=True)

    def select_topk():
        cnt_nonneg = count(lambda kb, s: kb >= 0)
        nonneg = cnt_nonneg >= kf
        prefix0 = jnp.where(nonneg, jnp.int32(0), jnp.int32(INT_MIN))
        cnt0 = jnp.where(nonneg, cnt_nonneg, (n_chunks * tq).astype(F32))

        def bit_body(j, carry):
            prefix, cnt_prefix = carry
            cand = prefix | lax.shift_left(jnp.int32(1), 30 - j)
            cnt = count(lambda kb, s: kb >= cand)
            keep = cnt >= kf
            return jnp.where(keep, cand, prefix), jnp.where(keep, cnt, cnt_prefix)

        thr, cnt_ge = lax.fori_loop(0, 31, bit_body, (prefix0, cnt0))
        sel_scr[0:1, :] = thr
        sel_scr[1:2, :] = jnp.full((1, tq), seq, jnp.int32)

        @pl.when(jnp.max(cnt_ge) > kf)
        def _():
            nbits = max(1, (seq - 1).bit_length())
            need = kf - count(lambda kb, s: kb > thr)

            def idx_body(j, m):
                cand = m | lax.shift_left(jnp.int32(1), nbits - 1 - j)
                cnt = count(lambda kb, s: (kb == thr) & (s < cand))
                return jnp.where(cnt < need, cand, m)

            sel_scr[1:2, :] = lax.fori_loop(0, nbits, idx_body, jnp.zeros((1, tq), jnp.int32))

    if tq <= topk:
        @pl.when(i == 0)
        def _():
            sel_scr[0:1, :] = jnp.full((1, tq), INT_MIN, jnp.int32)
            sel_scr[1:2, :] = jnp.full((1, tq), -1, jnp.int32)

        pl.when(i > 0)(select_topk)
    else:
        select_topk()

    thr = sel_scr[0:1, :]
    m_idx = sel_scr[1:2, :]

    def bias_body(c, carry):
        off = c * tq
        kb = keys_scr[c]
        s_idx = off + row
        take = jnp.where(kb > thr, 1.0, jnp.where((kb == thr) & (s_idx <= m_idx), 1.0, 0.0))
        take = jnp.where(s_idx <= t0 + col, take, 0.0)
        bias_scr[c] = jnp.where(take > 0.5, 0.0, NEG_BIG)
        return carry

    lax.fori_loop(0, n_chunks, bias_body, 0)

    @pl.when(i == 0)
    def _():
        for c in range(seq // tq):
            for g in range(N_KV_HEADS):
                blk = v_ref[c * tq:(c + 1) * tq, g * HEAD_DIM:(g + 1) * HEAD_DIM].astype(F32)
                vt_scr[c, g * HEAD_DIM:(g + 1) * HEAD_DIM, :] = blk.T.astype(BF16)

    for h in range(N_HEADS):
        g, r = divmod(h, rep)
        qa_scr[g, r * tq:(r + 1) * tq, :HEAD_DIM] = q_ref[:, h * HEAD_DIM:(h + 1) * HEAD_DIM]
        qa_scr[g, r * tq:(r + 1) * tq, HEAD_DIM:] = jnp.broadcast_to(
            slope_ref[h:h + 1, :], (tq, V7X_LANES)).astype(BF16)
    acc_scr[...] = jnp.zeros(acc_scr.shape, F32)
    for g in range(N_KV_HEADS):
        ml_scr[2 * g:2 * g + 1, :] = jnp.full((1, rep * tq), NEG_BIG, F32)
        ml_scr[2 * g + 1:2 * g + 2, :] = jnp.zeros((1, rep * tq), F32)

    def attn_body(c, carry):
        off = pl.multiple_of(c * tq, tq)
        pos_c = pos_ref[pl.ds(off, tq), :]
        bias = jnp.concatenate([bias_scr[c]] * rep, axis=1)
        def scores(g):
            kc = jnp.concatenate([k_ref[pl.ds(off, tq), g * HEAD_DIM:(g + 1) * HEAD_DIM], pos_c],
                                 axis=1)
            return lax.dot_general(kc, qa_scr[g], nt, preferred_element_type=F32) + bias

        lead = 2
        sts = [scores(g) for g in range(lead)]
        for g in range(N_KV_HEADS):
            st = sts[g]
            if g + lead < N_KV_HEADS:
                sts.append(scores(g + lead))
            vt = vt_scr[c, g * HEAD_DIM:(g + 1) * HEAD_DIM, :]
            m_prev = ml_scr[2 * g:2 * g + 1, :]
            m_new = jnp.maximum(m_prev, _reduce_rows(st, jnp.maximum))
            alpha = jnp.exp2(m_prev - m_new)
            pt = jnp.exp2(st - m_new)
            ml_scr[2 * g:2 * g + 1, :] = m_new
            ml_scr[2 * g + 1:2 * g + 2, :] = (alpha * ml_scr[2 * g + 1:2 * g + 2, :]
                                              + _reduce_rows(pt, jnp.add))
            acc_scr[g] = alpha * acc_scr[g] + jnp.dot(vt, pt.astype(BF16), preferred_element_type=F32)
        return carry

    lax.fori_loop(0, n_chunks, attn_body, 0)
    for g in range(N_KV_HEADS):
        o_t = acc_scr[g] / ml_scr[2 * g + 1:2 * g + 2, :]
        for r in range(rep):
            h = g * rep + r
            y_ref[:, h * HEAD_DIM:(h + 1) * HEAD_DIM] = o_t[:, r * tq:(r + 1) * tq].T.astype(y_ref.dtype)


def _attention(q, qi, kw, k, v, batch, seq):
    m = q.shape[0]
    topk = min(TOPK_MAX, seq // 4)
    tq = _pick(seq, (256, 128))
    nq = seq // tq
    kww = kw.shape[1]
    rep = N_HEADS // N_KV_HEADS
    assert HEAD_DIM == V7X_LANES and tq % V7X_LANES == 0
    assert kww == V7X_LANES == 2 * IDX_DIM and IDX_DIM + IDX_HEADS <= V7X_LANES
    return pl.pallas_call(
        functools.partial(_attn_kernel, tq=tq, seq=seq, topk=topk),
        grid=(batch, nq),
        in_specs=[pl.BlockSpec((tq, ATT_WIDTH), lambda b, i: (b * nq + i, 0)),
                  pl.BlockSpec((tq, IDX_HEADS * IDX_DIM), lambda b, i: (b * nq + i, 0)),
                  pl.BlockSpec((tq, kww), lambda b, i: (b * nq + i, 0)),
                  pl.BlockSpec((seq, KV_WIDTH), lambda b, i: (b, 0)),
                  pl.BlockSpec((seq, KV_WIDTH), lambda b, i: (b, 0)),
                  pl.BlockSpec((seq, kww), lambda b, i: (b, 0)),
                  pl.BlockSpec((N_HEADS, V7X_LANES), lambda b, i: (0, 0)),
                  pl.BlockSpec((seq, V7X_LANES), lambda b, i: (0, 0))],
        out_specs=pl.BlockSpec((tq, ATT_WIDTH), lambda b, i: (b * nq + i, 0)),
        out_shape=jax.ShapeDtypeStruct((m, ATT_WIDTH), BF16),
        scratch_shapes=[pltpu.VMEM((nq, tq, tq), jnp.int32),
                        pltpu.VMEM((nq, tq, tq), F32),
                        pltpu.VMEM((V7X_LANES, tq), F32),
                        pltpu.VMEM((8, tq), jnp.int32),
                        pltpu.VMEM((nq, KV_WIDTH, tq), BF16),
                        pltpu.VMEM((N_KV_HEADS, rep * tq, 2 * HEAD_DIM), BF16),
                        pltpu.VMEM((2 * N_KV_HEADS, rep * tq), F32),
                        pltpu.VMEM((N_KV_HEADS, HEAD_DIM, rep * tq), F32)],
        compiler_params=_params(("arbitrary", "arbitrary"),
                                6 * tq * ATT_WIDTH * 2, 4 * seq * KV_WIDTH * 2, 2 * (seq + tq) * kww * 4,
                                2 * seq * tq * 4, 8 * tq * tq * 4, seq * KV_WIDTH * 2, rep * HEAD_DIM * tq * 4),
        name="dsa_attention",
    )(q, qi, kw, k, v, kw, *_alibi_tables(seq))


def _sgu_kernel(u_ref, v_ref, g_ref, w_ref, bt_ref, o_ref, *, rows):
    v = v_ref[...].astype(F32)
    vn = (v * lax.rsqrt(jnp.mean(v * v, axis=-1, keepdims=True) + EPS) * g_ref[...]).astype(BF16)
    r_i = lax.broadcasted_iota(jnp.int32, (GM_CHUNK, GM_CHUNK), 0)
    c_i = lax.broadcasted_iota(jnp.int32, (GM_CHUNK, GM_CHUNK), 1)
    for g in range(GM_GROUPS):
        w = jnp.where(r_i >= c_i, w_ref[g], 0.0).astype(BF16)
        bcol = bt_ref[:, g:g + 1]
        cs = slice(g * GM_GROUP_W, (g + 1) * GM_GROUP_W)
        for n in range(rows // GM_CHUNK):
            rs = slice(n * GM_CHUNK, (n + 1) * GM_CHUNK)
            f = jnp.dot(w, vn[rs, cs], preferred_element_type=F32) + bcol
            o_ref[rs, cs] = (u_ref[rs, cs].astype(F32) * f).astype(o_ref.dtype)


def _sgu(uv, gain, w_s, b_s, seq):
    m = uv.shape[0]
    rows = _pick(seq, (512, 256, 128))
    return pl.pallas_call(
        functools.partial(_sgu_kernel, rows=rows),
        grid=(m // rows,),
        in_specs=[pl.BlockSpec((rows, GM_WIDTH), lambda i: (i, 0)),
                  pl.BlockSpec((rows, GM_WIDTH), lambda i: (i, 1)),
                  pl.BlockSpec((1, GM_WIDTH), lambda i: (0, 0)),
                  pl.BlockSpec((GM_GROUPS, GM_CHUNK, GM_CHUNK), lambda i: (0, 0, 0)),
                  pl.BlockSpec((GM_CHUNK, GM_GROUPS), lambda i: (0, 0))],
        out_specs=pl.BlockSpec((rows, GM_WIDTH), lambda i: (i, 0)),
        out_shape=jax.ShapeDtypeStruct((m, GM_WIDTH), BF16),
        compiler_params=_params(("arbitrary",), 6 * rows * GM_WIDTH * 2, 2 * rows * GM_WIDTH * 4),
        name="sgu",
    )(uv, uv, gain.reshape(1, GM_WIDTH), w_s, b_s.T)


def _w_spec(w, l, k, tn, j0=0):
    if w.ndim == 3:
        return pl.BlockSpec((None, k, tn), lambda i, j: (l, 0, j + j0))
    return pl.BlockSpec((k, tn), lambda i, j: (0, j + j0))


def _merge_kernel(ya_ref, yb_ref, wa_ref, wb_ref, ga_ref, gb_ref, o_ref):
    pa = jnp.dot(ya_ref[...], wa_ref[...].astype(BF16), preferred_element_type=F32)
    pb = jnp.dot(yb_ref[...], wb_ref[...].astype(BF16), preferred_element_type=F32)
    o_ref[...] = (ga_ref[...].astype(F32) * pa + gb_ref[...].astype(F32) * pb).astype(o_ref.dtype)


def _merge(ya, yb, wa, wb, l, gates):
    m, ka = ya.shape
    kb = yb.shape[1]
    d = wa.shape[-1]
    tm = _pick(m, (1024, 512, 256, 128))
    tn = _pick(d, (512, 256, 128))
    nj = d // tn
    wsz = wa.dtype.itemsize
    return pl.pallas_call(
        _merge_kernel,
        grid=(m // tm, nj),
        in_specs=[pl.BlockSpec((tm, ka), lambda i, j: (i, 0)),
                  pl.BlockSpec((tm, kb), lambda i, j: (i, 0)),
                  _w_spec(wa, l, ka, tn),
                  _w_spec(wb, l, kb, tn),
                  pl.BlockSpec((tm, tn), lambda i, j: (i, j)),
                  pl.BlockSpec((tm, tn), lambda i, j: (i, j + nj))],
        out_specs=pl.BlockSpec((tm, tn), lambda i, j: (i, j)),
        out_shape=jax.ShapeDtypeStruct((m, d), BF16),
        compiler_params=pltpu.CompilerParams(
            dimension_semantics=("arbitrary", "arbitrary"),
            vmem_limit_bytes=int(min(VMEM_CAP, 2 * tm * (ka + kb) * 2 + (ka + kb) * tn * (2 * wsz + 2)
                                     + 6 * tm * tn * 2 + 3 * tm * tn * 4 + (4 << 20)))),
        name="merge",
    )(ya, yb, wa, wb, gates, gates)


def _resid_kernel(a_ref, w_ref, x_ref, g_ref, o_ref):
    acc = jnp.dot(a_ref[...], w_ref[...].astype(BF16), preferred_element_type=F32)
    o_ref[...] = x_ref[...] + g_ref[...] * acc


def _resid(a, w, l, x2, mod3, g_idx, seq, *, tn_prefs, single_buffer_a, name):
    m, k = a.shape
    d = w.shape[-1]
    tm = _pick(seq, (1024, 512, 256, 128))
    per_b = seq // tm
    tn = _pick(d, tn_prefs)
    nj = d // tn
    a_bufs = 1 if single_buffer_a else 2
    wsz = w.dtype.itemsize
    a_spec = (pl.BlockSpec((tm, k), lambda i, j: (i, 0), pipeline_mode=pl.Buffered(1))
              if single_buffer_a else pl.BlockSpec((tm, k), lambda i, j: (i, 0)))
    return pl.pallas_call(
        _resid_kernel,
        grid=(m // tm, nj),
        in_specs=[a_spec,
                  _w_spec(w, l, k, tn),
                  pl.BlockSpec((tm, tn), lambda i, j: (i, j)),
                  pl.BlockSpec((None, 1, tn), lambda i, j: (i // per_b, 0, g_idx * nj + j))],
        out_specs=pl.BlockSpec((tm, tn), lambda i, j: (i, j)),
        out_shape=jax.ShapeDtypeStruct((m, d), F32),
        compiler_params=pltpu.CompilerParams(
            dimension_semantics=("arbitrary", "arbitrary"),
            vmem_limit_bytes=int(min(VMEM_CAP, a_bufs * tm * k * 2 + k * tn * (2 * wsz + 2)
                                     + 6 * tm * tn * 4 + (4 << 20)))),
        name=name,
    )(a, w, x2, mod3)


HALO = 8


def _up_kernel(h_ref, wg_ref, wv_ref, cwg_ref, cwv_ref, cbg_ref, cbv_ref, wd_ref, o_ref, wd_o_ref, *, sub):
    wd_o_ref[...] = wd_ref[...].astype(BF16)
    wg = wg_ref[...].astype(BF16)
    wv = wv_ref[...].astype(BF16)
    tn = wg.shape[1]

    def conv(a, halo, cw_ref, cb_ref):
        ext = jnp.concatenate([halo, a], axis=0)
        acc = cb_ref[...] + pltpu.roll(ext, 2, 0)[HALO:] * cw_ref[0:1, :]
        acc = acc + pltpu.roll(ext, 1, 0)[HALO:] * cw_ref[1:2, :]
        return acc + a * cw_ref[2:3, :]

    halo_g = halo_v = jnp.zeros((HALO, tn), F32)
    for s in range(h_ref.shape[0] // sub):
        rs = slice(s * sub, (s + 1) * sub)
        hs = h_ref[rs, :]
        ag = jnp.dot(hs, wg, preferred_element_type=F32)
        av = jnp.dot(hs, wv, preferred_element_type=F32)
        gate = conv(ag, halo_g, cwg_ref, cbg_ref)
        val = conv(av, halo_v, cwv_ref, cbv_ref)
        o_ref[rs, :] = (gate * jax.nn.sigmoid(gate) * val).astype(o_ref.dtype)
        halo_g, halo_v = ag[sub - HALO:], av[sub - HALO:]


BF16_SUBLANES = 16


def _up_conv_gate(h2, w_up, l, conv_w, conv_b, w_down, seq):
    m, k = h2.shape
    f = w_up.shape[-1] // 2
    d_out = w_down.shape[-1]
    tn = _pick(f, (256, 128))
    nj = f // tn
    n_steps = (m // seq) * nj
    assert f % (n_steps * BF16_SUBLANES) == 0
    slab = f // n_steps
    sub = _pick(seq, (512, 256, 128))
    wsz = w_up.dtype.itemsize
    return pl.pallas_call(
        functools.partial(_up_kernel, sub=sub),
        grid=(m // seq, nj),
        in_specs=[pl.BlockSpec((seq, k), lambda i, j: (i, 0), pipeline_mode=pl.Buffered(1)),
                  _w_spec(w_up, l, k, tn),
                  _w_spec(w_up, l, k, tn, nj),
                  pl.BlockSpec((None, CONV_W, tn), lambda i, j: (l, 0, j)),
                  pl.BlockSpec((None, CONV_W, tn), lambda i, j: (l, 0, j + nj)),
                  pl.BlockSpec((None, 1, tn), lambda i, j: (l, 0, j)),
                  pl.BlockSpec((None, 1, tn), lambda i, j: (l, 0, j + nj)),
                  pl.BlockSpec((None, slab, d_out), lambda i, j: (l, i * nj + j, 0))],
        out_specs=[pl.BlockSpec((seq, tn), lambda i, j: (i, j)),
                   pl.BlockSpec((slab, d_out), lambda i, j: (i * nj + j, 0))],
        out_shape=[jax.ShapeDtypeStruct((m, f), BF16),
                   jax.ShapeDtypeStruct((f, d_out), BF16)],
        compiler_params=pltpu.CompilerParams(
            dimension_semantics=("arbitrary", "arbitrary"),
            vmem_limit_bytes=int(min(VMEM_CAP, seq * k * 2 + 2 * k * tn * (2 * wsz + 2) + 2 * seq * tn * 2
                                     + 12 * sub * tn * 4 + 2 * slab * d_out * 6 + (4 << 20)))),
        name="up_conv_gate",
    )(h2, w_up, w_up, conv_w, conv_w, conv_b.reshape(conv_b.shape[0], 1, 2 * f),
      conv_b.reshape(conv_b.shape[0], 1, 2 * f), w_down)


def kernel(x, c, ada_w, ada_b, norm1_g, w_in, q_norm_g, k_norm_g, sgu_norm_g, sgu_w, sgu_b,
           w_branch_a, w_branch_b, w_out, norm2_g, w_up, conv_w, conv_b, w_down):
    batch, seq, d = x.shape
    m = batch * seq
    depth = ada_w.shape[0]
    assert seq % GM_CHUNK == 0 and d % V7X_LANES == 0

    x2 = x.reshape(m, d)
    bp = -(-batch // 8) * 8
    c_pad = jnp.pad(c, ((0, bp - batch), (0, 0)))

    o_q = 0
    o_k = o_q + ATT_WIDTH
    o_v = o_k + KV_WIDTH
    o_qi = o_v + KV_WIDTH
    o_ki = o_qi + IDX_HEADS * IDX_DIM
    o_wi = o_ki + IDX_DIM
    o_gu = o_wi + IDX_HEADS
    o_ga = o_gu + 2 * GM_WIDTH
    o_end = o_ga + 2 * d

    assert o_gu - o_ki <= V7X_LANES and o_end == w_in.shape[2]
    w_in_t = jnp.swapaxes(w_in, 1, 2)

    for l in range(depth):
        mod = _ada(c_pad, ada_w, ada_b, l)
        mod3 = mod.reshape(bp, 1, 6 * d)

        h = _norm_mod(x2, norm1_g[l], mod3, 1, 0, seq)
        q = _proj(h, w_in_t, l, o_q, ATT_WIDTH, out_dtype=BF16, epilogue="headnorm", gain=q_norm_g[l],
                  post_scale=HEAD_DIM ** -0.5 * LOG2E, name="proj_q")
        k = _proj(h, w_in_t, l, o_k, KV_WIDTH, out_dtype=BF16, epilogue="headnorm", gain=k_norm_g[l],
                  name="proj_k")
        v = _proj(h, w_in_t, l, o_v, KV_WIDTH, out_dtype=BF16, name="proj_v")
        qi = _proj(h, w_in_t, l, o_qi, IDX_HEADS * IDX_DIM, out_dtype=BF16, name="proj_qi")
        kw = _proj(h, w_in_t, l, o_ki, V7X_LANES, out_dtype=F32, name="proj_kw")
        uv = _proj(h, w_in_t, l, o_gu, 2 * GM_WIDTH, out_dtype=BF16, epilogue="gelu", name="proj_uv")
        gates = _proj(h, w_in_t, l, o_ga, 2 * d, out_dtype=BF16, epilogue="sigmoid", name="proj_gates")

        y_a = _attention(q, qi, kw, k, v, batch, seq)
        y_b = _sgu(uv, sgu_norm_g[l], sgu_w[l], sgu_b[l], seq)
        merged = _merge(y_a, y_b, w_branch_a, w_branch_b, l, gates)
        x2 = _resid(merged, w_out, l, x2, mod3, 2, seq,
                    tn_prefs=(512, 256, 128), single_buffer_a=False, name="out_proj_resid")

        h2 = _norm_mod(x2, norm2_g[l], mod3, 4, 3, seq)
        act, w_down_bf = _up_conv_gate(h2, w_up, l, conv_w, conv_b, w_down, seq)
        x2 = _resid(act, w_down_bf, l, x2, mod3, 5, seq,
                    tn_prefs=(256, 128), single_buffer_a=True, name="down_proj_resid")

    return x2.reshape(batch, seq, d)
```

```python
import functools

import numpy as np
import jax
import jax.numpy as jnp
from jax import lax
from jax.experimental import pallas as pl
from jax.experimental.pallas import tpu as pltpu

N_HEADS = 16
HEAD_DIM = 128
N_KV_HEADS = 4
ATT_WIDTH = N_HEADS * HEAD_DIM
KV_WIDTH = N_KV_HEADS * HEAD_DIM
IDX_HEADS = 32
IDX_DIM = 64
TOPK_MAX = 256
GM_WIDTH = 2048
GM_GROUPS = 8
GM_GROUP_W = GM_WIDTH // GM_GROUPS
GM_CHUNK = 128
CONV_W = 3
EPS = 1e-6
NEG_BIG = -1e30

V7X_LANES = 128
V7X_VMEM_BYTES = 64 * 1024 * 1024
VMEM_CAP = V7X_VMEM_BYTES - 8 * 1024 * 1024

BF16 = jnp.bfloat16
F32 = jnp.float32
INT_MIN = -(2 ** 31)


def _vmem_limit(*nbytes):
    return int(min(VMEM_CAP, 2 * sum(nbytes) + (4 << 20)))


def _params(semantics, *nbytes):
    return pltpu.CompilerParams(dimension_semantics=semantics, vmem_limit_bytes=_vmem_limit(*nbytes))


def _pick(n, prefs):
    for p in prefs:
        if n % p == 0:
            return p
    return n


def _ada_kernel(c_ref, w_ref, b_ref, o_ref):
    c = c_ref[...]
    cs = c * jax.nn.sigmoid(c)
    o_ref[...] = jnp.dot(cs.astype(BF16), w_ref[...].astype(BF16),
                         preferred_element_type=F32) + b_ref[...]


def _ada(c_pad, ada_w, ada_b, l):
    bp, d = c_pad.shape
    n = ada_w.shape[2]
    tn = _pick(n, (512, 256, 128))
    return pl.pallas_call(
        _ada_kernel,
        grid=(n // tn,),
        in_specs=[pl.BlockSpec((bp, d), lambda j: (0, 0)),
                  pl.BlockSpec((None, d, tn), lambda j: (l, 0, j)),
                  pl.BlockSpec((None, 1, tn), lambda j: (l, 0, j))],
        out_specs=pl.BlockSpec((bp, tn), lambda j: (0, j)),
        out_shape=jax.ShapeDtypeStruct((bp, n), F32),
        compiler_params=_params(("arbitrary",), 2 * d * tn * 4, d * tn * 2),
        name="ada_mod",
    )(c_pad, ada_w, ada_b.reshape(ada_b.shape[0], 1, n))


def _norm_mod_kernel(x_ref, g_ref, sc_ref, sh_ref, o_ref):
    x = x_ref[...]
    y = x * lax.rsqrt(jnp.mean(x * x, axis=-1, keepdims=True) + EPS) * g_ref[...]
    o_ref[...] = (y * (1.0 + sc_ref[...]) + sh_ref[...]).astype(o_ref.dtype)


def _norm_mod(x2, g, mod3, sc_idx, sh_idx, seq):
    m, d = x2.shape
    tm = _pick(seq, (512, 256, 128, 64, 8))
    per_b = seq // tm
    return pl.pallas_call(
        _norm_mod_kernel,
        grid=(m // tm,),
        in_specs=[pl.BlockSpec((tm, d), lambda i: (i, 0)),
                  pl.BlockSpec((1, d), lambda i: (0, 0)),
                  pl.BlockSpec((None, 1, d), lambda i: (i // per_b, 0, sc_idx)),
                  pl.BlockSpec((None, 1, d), lambda i: (i // per_b, 0, sh_idx))],
        out_specs=pl.BlockSpec((tm, d), lambda i: (i, 0)),
        out_shape=jax.ShapeDtypeStruct((m, d), BF16),
        compiler_params=_params(("arbitrary",), 2 * tm * d * 4, 2 * tm * d * 2),
        name="norm_mod",
    )(x2, g.reshape(1, d), mod3, mod3)


def _gelu_exact(x):
    return 0.5 * x * (1.0 + lax.erf(x * (2.0 ** -0.5)))


MXU_ACC_ROWS = 512


_NT = (((1,), (1,)), ((), ()))


def _proj_kernel(a_ref, w_ref, *rest, epilogue, post_scale):
    o_ref = rest[-1]
    tm, tn = o_ref.shape
    rsub = min(tm, MXU_ACC_ROWS)
    w = w_ref[...].astype(BF16)
    for rb in range(tm // rsub):
        rs = slice(rb * rsub, (rb + 1) * rsub)
        acc = lax.dot_general(a_ref[rs, :], w, _NT, preferred_element_type=F32)
        if epilogue == "plain":
            out = acc
        elif epilogue == "gelu":
            out = _gelu_exact(acc)
        elif epilogue == "sigmoid":
            out = jax.nn.sigmoid(acc)
        elif epilogue == "headnorm":
            gain = rest[0][...]
            parts = []
            for c in range(tn // HEAD_DIM):
                blk = acc[:, c * HEAD_DIM:(c + 1) * HEAD_DIM]
                ms = jnp.mean(blk * blk, axis=-1, keepdims=True)
                parts.append(blk * lax.rsqrt(ms + EPS) * (gain * post_scale))
            out = jnp.concatenate(parts, axis=1) if len(parts) > 1 else parts[0]
        else:
            raise ValueError(epilogue)
        o_ref[rs, :] = out.astype(o_ref.dtype)


def _proj(a, w_t, l, row0, n, *, out_dtype, epilogue="plain", gain=None, post_scale=1.0, name):
    m, k = a.shape
    assert row0 % 8 == 0 and w_t.shape[2] == k
    tm = _pick(m, (1024, 512, 256, 128))
    tn = _pick(n, (512, 256, 128))
    in_specs = [pl.BlockSpec((tm, k), lambda i, j: (i, 0)),
                pl.BlockSpec((None, pl.Element(tn), pl.Element(k)),
                             lambda i, j: (l, pl.multiple_of(row0 + j * tn, 8), 0))]
    args = [a, w_t]
    if epilogue == "headnorm":
        in_specs.append(pl.BlockSpec((1, HEAD_DIM), lambda i, j: (0, 0)))
        args.append(gain.reshape(1, HEAD_DIM))
    osz = jnp.dtype(out_dtype).itemsize
    return pl.pallas_call(
        functools.partial(_proj_kernel, epilogue=epilogue, post_scale=post_scale),
        grid=(m // tm, n // tn),
        in_specs=in_specs,
        out_specs=pl.BlockSpec((tm, tn), lambda i, j: (i, j)),
        out_shape=jax.ShapeDtypeStruct((m, n), out_dtype),
        compiler_params=pltpu.CompilerParams(
            dimension_semantics=("arbitrary", "arbitrary"),
            vmem_limit_bytes=int(min(VMEM_CAP, 2 * tm * k * 2 + tn * k * (2 * 4 + 2) + 2 * tm * tn * osz
                                     + 3 * tm * tn * 4 + (4 << 20)))),
        name=name,
    )(*args)


def _alibi_slopes(n):
    return [2.0 ** (-8.0 * (i + 1) / n) for i in range(n)]


F32_SUBLANES = 8


def _fold_rows(x, op):
    rows = x.shape[0]
    while rows > F32_SUBLANES and rows % (2 * F32_SUBLANES) == 0:
        rows //= 2
        x = op(x[:rows], x[rows:])
    return x


def _reduce_rows(x, op):
    return (jnp.max if op is jnp.maximum else jnp.sum)(_fold_rows(x, op), axis=0, keepdims=True)


LOG2E = 1.4426950408889634
ALIBI_PIECES = 4
POS_SPLIT = 256


def _alibi_tables(seq):
    assert seq <= POS_SPLIT * POS_SPLIT and 2 * ALIBI_PIECES <= V7X_LANES
    slope_tab = np.zeros((N_HEADS, V7X_LANES), np.float32)
    for h, s in enumerate(_alibi_slopes(N_HEADS)):
        rest = float(np.float32(s)) * LOG2E
        for p in range(ALIBI_PIECES):
            piece = float(np.asarray(rest, dtype=BF16))
            slope_tab[h, 2 * p:2 * p + 2] = piece
            rest -= piece
    pos = np.arange(seq)
    pos_tab = np.zeros((seq, V7X_LANES), np.float32)
    pos_tab[:, 0:2 * ALIBI_PIECES:2] = ((pos // POS_SPLIT) * POS_SPLIT)[:, None]
    pos_tab[:, 1:2 * ALIBI_PIECES:2] = (pos % POS_SPLIT)[:, None]
    return jnp.asarray(slope_tab), jnp.asarray(pos_tab, dtype=BF16)


def _attn_kernel(q_ref, qi_ref, kwq_ref, k_ref, v_ref, kwa_ref, slope_ref, pos_ref, y_ref,
                 keys_scr, bias_scr, wt_scr, sel_scr, vt_scr, qa_scr, ml_scr, acc_scr, *, tq, seq, topk):
    i = pl.program_id(1)
    n_chunks = i + 1
    t0 = i * tq
    rep = N_HEADS // N_KV_HEADS
    nt = (((1,), (1,)), ((), ()))
    row = lax.broadcasted_iota(jnp.int32, (tq, tq), 0)
    col = lax.broadcasted_iota(jnp.int32, (tq, tq), 1)
    kf = float(topk)

    w_fold = (IDX_HEADS ** -0.5) * (IDX_DIM ** -0.5)
    wt_scr[...] = (kwq_ref[...] * w_fold).T
    lane = lax.broadcasted_iota(jnp.int32, (tq, V7X_LANES), 1)

    def score_body(c, carry):
        off = pl.multiple_of(c * tq, tq)
        kraw = kwa_ref[pl.ds(off, tq), :]
        k_even = jnp.where(lane < IDX_DIM, kraw, 0.0)
        k_odd = jnp.where(lane >= IDX_DIM, pltpu.roll(kraw, IDX_DIM, 1), 0.0)
        lhs = jnp.concatenate([k_even, k_odd], axis=0).astype(BF16)
        acc = jnp.zeros((tq, tq), F32)
        for p in range(IDX_HEADS // 2):
            lg = lax.dot_general(lhs, qi_ref[:, p * V7X_LANES:(p + 1) * V7X_LANES], nt,
                                 preferred_element_type=F32)
            h0 = IDX_DIM + 2 * p
            acc = acc + wt_scr[h0:h0 + 1, :] * jnp.maximum(lg[:tq], 0.0)
            acc = acc + wt_scr[h0 + 1:h0 + 2, :] * jnp.maximum(lg[tq:], 0.0)
        bits = lax.bitcast_convert_type(acc, jnp.int32)
        key = bits ^ ((bits >> 31) & jnp.int32(0x7FFFFFFF))
        keys_scr[c] = jnp.where(off + row <= t0 + col, key, jnp.int32(INT_MIN))
        return carry

    lax.fori_loop(0, n_chunks, score_body, 0)

    def count(pred):
        def body(c, part):
            off = c * tq
            return part + _fold_rows(jnp.where(pred(keys_scr[c], off + row), 1.0, 0.0), jnp.add)
        part = lax.fori_loop(0, n_chunks, body, jnp.zeros((F32_SUBLANES, tq), F32))
        return jnp.sum(part, axis=0, keepdims=True)

    def select_topk():
        cnt_nonneg = count(lambda kb, s: kb >= 0)
        nonneg = cnt_nonneg >= kf
        prefix0 = jnp.where(nonneg, jnp.int32(0), jnp.int32(INT_MIN))
        cnt0 = jnp.where(nonneg, cnt_nonneg, (n_chunks * tq).astype(F32))

        def bit_body(j, carry):
            prefix, cnt_prefix = carry
            cand = prefix | lax.shift_left(jnp.int32(1), 30 - j)
            cnt = count(lambda kb, s: kb >= cand)
            keep = cnt >= kf
            return jnp.where(keep, cand, prefix), jnp.where(keep, cnt, cnt_prefix)

        thr, cnt_ge = lax.fori_loop(0, 31, bit_body, (prefix0, cnt0))
        sel_scr[0:1, :] = thr
        sel_scr[1:2, :] = jnp.full((1, tq), seq, jnp.int32)

        @pl.when(jnp.max(cnt_ge) > kf)
        def _():
            nbits = max(1, (seq - 1).bit_length())
            need = kf - count(lambda kb, s: kb > thr)

            def idx_body(j, m):
                cand = m | lax.shift_left(jnp.int32(1), nbits - 1 - j)
                cnt = count(lambda kb, s: (kb == thr) & (s < cand))
                return jnp.where(cnt < need, cand, m)

            sel_scr[1:2, :] = lax.fori_loop(0, nbits, idx_body, jnp.zeros((1, tq), jnp.int32))

    if tq <= topk:
        @pl.when(i == 0)
        def _():
            sel_scr[0:1, :] = jnp.full((1, tq), INT_MIN, jnp.int32)
            sel_scr[1:2, :] = jnp.full((1, tq), -1, jnp.int32)

        pl.when(i > 0)(select_topk)
    else:
        select_topk()

    thr = sel_scr[0:1, :]
    m_idx = sel_scr[1:2, :]

    def bias_body(c, carry):
        off = c * tq
        kb = keys_scr[c]
        s_idx = off + row
        take = jnp.where(kb > thr, 1.0, jnp.where((kb == thr) & (s_idx <= m_idx), 1.0, 0.0))
        take = jnp.where(s_idx <= t0 + col, take, 0.0)
        bias_scr[c] = jnp.where(take > 0.5, 0.0, NEG_BIG)
        return carry

    lax.fori_loop(0, n_chunks, bias_body, 0)

    @pl.when(i == 0)
    def _():
        for c in range(seq // tq):
            for g in range(N_KV_HEADS):
                blk = v_ref[c * tq:(c + 1) * tq, g * HEAD_DIM:(g + 1) * HEAD_DIM].astype(F32)
                vt_scr[c, g * HEAD_DIM:(g + 1) * HEAD_DIM, :] = blk.T.astype(BF16)

    for h in range(N_HEADS):
        g, r = divmod(h, rep)
        qa_scr[g, r * tq:(r + 1) * tq, :HEAD_DIM] = q_ref[:, h * HEAD_DIM:(h + 1) * HEAD_DIM]
        qa_scr[g, r * tq:(r + 1) * tq, HEAD_DIM:] = jnp.broadcast_to(
            slope_ref[h:h + 1, :], (tq, V7X_LANES)).astype(BF16)
    acc_scr[...] = jnp.zeros(acc_scr.shape, F32)
    for g in range(N_KV_HEADS):
        ml_scr[2 * g:2 * g + 1, :] = jnp.full((1, rep * tq), NEG_BIG, F32)
        ml_scr[2 * g + 1:2 * g + 2, :] = jnp.zeros((1, rep * tq), F32)

    def attn_body(c, carry):
        off = pl.multiple_of(c * tq, tq)
        pos_c = pos_ref[pl.ds(off, tq), :]
        bias = jnp.concatenate([bias_scr[c]] * rep, axis=1)
        def scores(g):
            kc = jnp.concatenate([k_ref[pl.ds(off, tq), g * HEAD_DIM:(g + 1) * HEAD_DIM], pos_c],
                                 axis=1)
            return lax.dot_general(kc, qa_scr[g], nt, preferred_element_type=F32) + bias

        lead = N_KV_HEADS - 1
        sts = [scores(g) for g in range(lead)]
        for g in range(N_KV_HEADS):
            st = sts[g]
            if g + lead < N_KV_HEADS:
                sts.append(scores(g + lead))
            vt = vt_scr[c, g * HEAD_DIM:(g + 1) * HEAD_DIM, :]
            m_prev = ml_scr[2 * g:2 * g + 1, :]
            m_new = jnp.maximum(m_prev, _reduce_rows(st, jnp.maximum))
            alpha = jnp.exp2(m_prev - m_new)
            pt = jnp.exp2(st - m_new)
            ml_scr[2 * g:2 * g + 1, :] = m_new
            ml_scr[2 * g + 1:2 * g + 2, :] = (alpha * ml_scr[2 * g + 1:2 * g + 2, :]
                                              + _reduce_rows(pt, jnp.add))
            acc_scr[g] = alpha * acc_scr[g] + jnp.dot(vt, pt.astype(BF16), preferred_element_type=F32)
        return carry

    lax.fori_loop(0, n_chunks, attn_body, 0)
    for g in range(N_KV_HEADS):
        o_t = acc_scr[g] / ml_scr[2 * g + 1:2 * g + 2, :]
        for r in range(rep):
            h = g * rep + r
            y_ref[:, h * HEAD_DIM:(h + 1) * HEAD_DIM] = o_t[:, r * tq:(r + 1) * tq].T.astype(y_ref.dtype)


def _attention(q, qi, kw, k, v, batch, seq):
    m = q.shape[0]
    topk = min(TOPK_MAX, seq // 4)
    tq = _pick(seq, (256, 128))
    nq = seq // tq
    kww = kw.shape[1]
    rep = N_HEADS // N_KV_HEADS
    assert HEAD_DIM == V7X_LANES and tq % V7X_LANES == 0
    assert kww == V7X_LANES == 2 * IDX_DIM and IDX_DIM + IDX_HEADS <= V7X_LANES
    return pl.pallas_call(
        functools.partial(_attn_kernel, tq=tq, seq=seq, topk=topk),
        grid=(batch, nq),
        in_specs=[pl.BlockSpec((tq, ATT_WIDTH), lambda b, i: (b * nq + i, 0)),
                  pl.BlockSpec((tq, IDX_HEADS * IDX_DIM), lambda b, i: (b * nq + i, 0)),
                  pl.BlockSpec((tq, kww), lambda b, i: (b * nq + i, 0)),
                  pl.BlockSpec((seq, KV_WIDTH), lambda b, i: (b, 0)),
                  pl.BlockSpec((seq, KV_WIDTH), lambda b, i: (b, 0)),
                  pl.BlockSpec((seq, kww), lambda b, i: (b, 0)),
                  pl.BlockSpec((N_HEADS, V7X_LANES), lambda b, i: (0, 0)),
                  pl.BlockSpec((seq, V7X_LANES), lambda b, i: (0, 0))],
        out_specs=pl.BlockSpec((tq, ATT_WIDTH), lambda b, i: (b * nq + i, 0)),
        out_shape=jax.ShapeDtypeStruct((m, ATT_WIDTH), BF16),
        scratch_shapes=[pltpu.VMEM((nq, tq, tq), jnp.int32),
                        pltpu.VMEM((nq, tq, tq), F32),
                        pltpu.VMEM((V7X_LANES, tq), F32),
                        pltpu.VMEM((8, tq), jnp.int32),
                        pltpu.VMEM((nq, KV_WIDTH, tq), BF16),
                        pltpu.VMEM((N_KV_HEADS, rep * tq, 2 * HEAD_DIM), BF16),
                        pltpu.VMEM((2 * N_KV_HEADS, rep * tq), F32),
                        pltpu.VMEM((N_KV_HEADS, HEAD_DIM, rep * tq), F32)],
        compiler_params=_params(("arbitrary", "arbitrary"),
                                6 * tq * ATT_WIDTH * 2, 4 * seq * KV_WIDTH * 2, 2 * (seq + tq) * kww * 4,
                                2 * seq * tq * 4, 8 * tq * tq * 4, seq * KV_WIDTH * 2, rep * HEAD_DIM * tq * 4),
        name="dsa_attention",
    )(q, qi, kw, k, v, kw, *_alibi_tables(seq))


def _sgu_kernel(u_ref, v_ref, g_ref, w_ref, bt_ref, o_ref, *, rows):
    v = v_ref[...].astype(F32)
    vn = (v * lax.rsqrt(jnp.mean(v * v, axis=-1, keepdims=True) + EPS) * g_ref[...]).astype(BF16)
    r_i = lax.broadcasted_iota(jnp.int32, (GM_CHUNK, GM_CHUNK), 0)
    c_i = lax.broadcasted_iota(jnp.int32, (GM_CHUNK, GM_CHUNK), 1)
    for g in range(GM_GROUPS):
        w = jnp.where(r_i >= c_i, w_ref[g], 0.0).astype(BF16)
        bcol = bt_ref[:, g:g + 1]
        cs = slice(g * GM_GROUP_W, (g + 1) * GM_GROUP_W)
        for n in range(rows // GM_CHUNK):
            rs = slice(n * GM_CHUNK, (n + 1) * GM_CHUNK)
            f = jnp.dot(w, vn[rs, cs], preferred_element_type=F32) + bcol
            o_ref[rs, cs] = (u_ref[rs, cs].astype(F32) * f).astype(o_ref.dtype)


def _sgu(uv, gain, w_s, b_s, seq):
    m = uv.shape[0]
    rows = _pick(seq, (512, 256, 128))
    return pl.pallas_call(
        functools.partial(_sgu_kernel, rows=rows),
        grid=(m // rows,),
        in_specs=[pl.BlockSpec((rows, GM_WIDTH), lambda i: (i, 0)),
                  pl.BlockSpec((rows, GM_WIDTH), lambda i: (i, 1)),
                  pl.BlockSpec((1, GM_WIDTH), lambda i: (0, 0)),
                  pl.BlockSpec((GM_GROUPS, GM_CHUNK, GM_CHUNK), lambda i: (0, 0, 0)),
                  pl.BlockSpec((GM_CHUNK, GM_GROUPS), lambda i: (0, 0))],
        out_specs=pl.BlockSpec((rows, GM_WIDTH), lambda i: (i, 0)),
        out_shape=jax.ShapeDtypeStruct((m, GM_WIDTH), BF16),
        compiler_params=_params(("arbitrary",), 6 * rows * GM_WIDTH * 2, 2 * rows * GM_WIDTH * 4),
        name="sgu",
    )(uv, uv, gain.reshape(1, GM_WIDTH), w_s, b_s.T)


def _w_spec(w, l, k, tn, j0=0):
    if w.ndim == 3:
        return pl.BlockSpec((None, k, tn), lambda i, j: (l, 0, j + j0))
    return pl.BlockSpec((k, tn), lambda i, j: (0, j + j0))


def _merge_kernel(ya_ref, yb_ref, wa_ref, wb_ref, ga_ref, gb_ref, o_ref):
    pa = jnp.dot(ya_ref[...], wa_ref[...].astype(BF16), preferred_element_type=F32)
    pb = jnp.dot(yb_ref[...], wb_ref[...].astype(BF16), preferred_element_type=F32)
    o_ref[...] = (ga_ref[...].astype(F32) * pa + gb_ref[...].astype(F32) * pb).astype(o_ref.dtype)


def _merge(ya, yb, wa, wb, l, gates):
    m, ka = ya.shape
    kb = yb.shape[1]
    d = wa.shape[-1]
    tm = _pick(m, (1024, 512, 256, 128))
    tn = _pick(d, (512, 256, 128))
    nj = d // tn
    wsz = wa.dtype.itemsize
    return pl.pallas_call(
        _merge_kernel,
        grid=(m // tm, nj),
        in_specs=[pl.BlockSpec((tm, ka), lambda i, j: (i, 0)),
                  pl.BlockSpec((tm, kb), lambda i, j: (i, 0)),
                  _w_spec(wa, l, ka, tn),
                  _w_spec(wb, l, kb, tn),
                  pl.BlockSpec((tm, tn), lambda i, j: (i, j)),
                  pl.BlockSpec((tm, tn), lambda i, j: (i, j + nj))],
        out_specs=pl.BlockSpec((tm, tn), lambda i, j: (i, j)),
        out_shape=jax.ShapeDtypeStruct((m, d), BF16),
        compiler_params=pltpu.CompilerParams(
            dimension_semantics=("arbitrary", "arbitrary"),
            vmem_limit_bytes=int(min(VMEM_CAP, 2 * tm * (ka + kb) * 2 + (ka + kb) * tn * (2 * wsz + 2)
                                     + 6 * tm * tn * 2 + 3 * tm * tn * 4 + (4 << 20)))),
        name="merge",
    )(ya, yb, wa, wb, gates, gates)


def _resid_kernel(a_ref, w_ref, x_ref, g_ref, o_ref):
    acc = jnp.dot(a_ref[...], w_ref[...].astype(BF16), preferred_element_type=F32)
    o_ref[...] = x_ref[...] + g_ref[...] * acc


def _resid(a, w, l, x2, mod3, g_idx, seq, *, tn_prefs, k_part=(0, 1), name):
    m = a.shape[0]
    kp_idx, kp_n = k_part
    k = a.shape[1] // kp_n
    d = w.shape[-1]
    assert a.shape[1] % kp_n == 0 and k % V7X_LANES == 0 and (kp_n == 1 or w.ndim == 2)
    tm = _pick(seq, (1024, 512, 256, 128))
    per_b = seq // tm
    tn = _pick(d, tn_prefs)
    nj = d // tn
    wsz = w.dtype.itemsize
    w_spec = _w_spec(w, l, k, tn) if w.ndim == 3 else pl.BlockSpec((k, tn), lambda i, j: (kp_idx, j))
    return pl.pallas_call(
        _resid_kernel,
        grid=(m // tm, nj),
        in_specs=[pl.BlockSpec((tm, k), lambda i, j: (i, kp_idx)),
                  w_spec,
                  pl.BlockSpec((tm, tn), lambda i, j: (i, j)),
                  pl.BlockSpec((None, 1, tn), lambda i, j: (i // per_b, 0, g_idx * nj + j))],
        out_specs=pl.BlockSpec((tm, tn), lambda i, j: (i, j)),
        out_shape=jax.ShapeDtypeStruct((m, d), F32),
        compiler_params=pltpu.CompilerParams(
            dimension_semantics=("arbitrary", "arbitrary"),
            vmem_limit_bytes=int(min(VMEM_CAP, 2 * tm * k * 2 + k * tn * (2 * wsz + 2)
                                     + 6 * tm * tn * 4 + (4 << 20)))),
        name=name,
    )(a, w, x2, mod3)


HALO = 8


def _up_kernel(h_ref, wg_ref, wv_ref, cwg_ref, cwv_ref, cbg_ref, cbv_ref, wd_ref, o_ref, wd_o_ref, *, sub):
    wd_o_ref[...] = wd_ref[...].astype(BF16)
    wg = wg_ref[...].astype(BF16)
    wv = wv_ref[...].astype(BF16)
    tn = wg.shape[1]

    def conv(a, halo, cw_ref, cb_ref):
        ext = jnp.concatenate([halo, a], axis=0)
        acc = cb_ref[...] + pltpu.roll(ext, 2, 0)[HALO:] * cw_ref[0:1, :]
        acc = acc + pltpu.roll(ext, 1, 0)[HALO:] * cw_ref[1:2, :]
        return acc + a * cw_ref[2:3, :]

    halo_g = halo_v = jnp.zeros((HALO, tn), F32)
    for s in range(h_ref.shape[0] // sub):
        rs = slice(s * sub, (s + 1) * sub)
        hs = h_ref[rs, :]
        ag = jnp.dot(hs, wg, preferred_element_type=F32)
        av = jnp.dot(hs, wv, preferred_element_type=F32)
        gate = conv(ag, halo_g, cwg_ref, cbg_ref)
        val = conv(av, halo_v, cwv_ref, cbv_ref)
        o_ref[rs, :] = (gate * jax.nn.sigmoid(gate) * val).astype(o_ref.dtype)
        halo_g, halo_v = ag[sub - HALO:], av[sub - HALO:]


BF16_SUBLANES = 16


def _up_conv_gate(h2, w_up, l, conv_w, conv_b, w_down, seq):
    m, k = h2.shape
    f = w_up.shape[-1] // 2
    d_out = w_down.shape[-1]
    tn = _pick(f, (256, 128))
    nj = f // tn
    n_steps = (m // seq) * nj
    assert f % (n_steps * BF16_SUBLANES) == 0
    slab = f // n_steps
    sub = _pick(seq, (512, 256, 128))
    wsz = w_up.dtype.itemsize
    return pl.pallas_call(
        functools.partial(_up_kernel, sub=sub),
        grid=(m // seq, nj),
        in_specs=[pl.BlockSpec((seq, k), lambda i, j: (i, 0), pipeline_mode=pl.Buffered(1)),
                  _w_spec(w_up, l, k, tn),
                  _w_spec(w_up, l, k, tn, nj),
                  pl.BlockSpec((None, CONV_W, tn), lambda i, j: (l, 0, j)),
                  pl.BlockSpec((None, CONV_W, tn), lambda i, j: (l, 0, j + nj)),
                  pl.BlockSpec((None, 1, tn), lambda i, j: (l, 0, j)),
                  pl.BlockSpec((None, 1, tn), lambda i, j: (l, 0, j + nj)),
                  pl.BlockSpec((None, slab, d_out), lambda i, j: (l, i * nj + j, 0))],
        out_specs=[pl.BlockSpec((seq, tn), lambda i, j: (i, j)),
                   pl.BlockSpec((slab, d_out), lambda i, j: (i * nj + j, 0))],
        out_shape=[jax.ShapeDtypeStruct((m, f), BF16),
                   jax.ShapeDtypeStruct((f, d_out), BF16)],
        compiler_params=pltpu.CompilerParams(
            dimension_semantics=("arbitrary", "arbitrary"),
            vmem_limit_bytes=int(min(VMEM_CAP, seq * k * 2 + 2 * k * tn * (2 * wsz + 2) + 2 * seq * tn * 2
                                     + 12 * sub * tn * 4 + 2 * slab * d_out * 6 + (4 << 20)))),
        name="up_conv_gate",
    )(h2, w_up, w_up, conv_w, conv_w, conv_b.reshape(conv_b.shape[0], 1, 2 * f),
      conv_b.reshape(conv_b.shape[0], 1, 2 * f), w_down)


def kernel(x, c, ada_w, ada_b, norm1_g, w_in, q_norm_g, k_norm_g, sgu_norm_g, sgu_w, sgu_b,
           w_branch_a, w_branch_b, w_out, norm2_g, w_up, conv_w, conv_b, w_down):
    batch, seq, d = x.shape
    m = batch * seq
    depth = ada_w.shape[0]
    assert seq % GM_CHUNK == 0 and d % V7X_LANES == 0

    x2 = x.reshape(m, d)
    bp = -(-batch // 8) * 8
    c_pad = jnp.pad(c, ((0, bp - batch), (0, 0)))

    o_q = 0
    o_k = o_q + ATT_WIDTH
    o_v = o_k + KV_WIDTH
    o_qi = o_v + KV_WIDTH
    o_ki = o_qi + IDX_HEADS * IDX_DIM
    o_wi = o_ki + IDX_DIM
    o_gu = o_wi + IDX_HEADS
    o_ga = o_gu + 2 * GM_WIDTH
    o_end = o_ga + 2 * d

    assert o_gu - o_ki <= V7X_LANES and o_end == w_in.shape[2]
    w_in_t = jnp.swapaxes(w_in, 1, 2)

    for l in range(depth):
        mod = _ada(c_pad, ada_w, ada_b, l)
        mod3 = mod.reshape(bp, 1, 6 * d)

        h = _norm_mod(x2, norm1_g[l], mod3, 1, 0, seq)
        q = _proj(h, w_in_t, l, o_q, ATT_WIDTH, out_dtype=BF16, epilogue="headnorm", gain=q_norm_g[l],
                  post_scale=HEAD_DIM ** -0.5 * LOG2E, name="proj_q")
        k = _proj(h, w_in_t, l, o_k, KV_WIDTH, out_dtype=BF16, epilogue="headnorm", gain=k_norm_g[l],
                  name="proj_k")
        v = _proj(h, w_in_t, l, o_v, KV_WIDTH, out_dtype=BF16, name="proj_v")
        qi = _proj(h, w_in_t, l, o_qi, IDX_HEADS * IDX_DIM, out_dtype=BF16, name="proj_qi")
        kw = _proj(h, w_in_t, l, o_ki, V7X_LANES, out_dtype=F32, name="proj_kw")
        uv = _proj(h, w_in_t, l, o_gu, 2 * GM_WIDTH, out_dtype=BF16, epilogue="gelu", name="proj_uv")
        gates = _proj(h, w_in_t, l, o_ga, 2 * d, out_dtype=BF16, epilogue="sigmoid", name="proj_gates")

        y_a = _attention(q, qi, kw, k, v, batch, seq)
        y_b = _sgu(uv, sgu_norm_g[l], sgu_w[l], sgu_b[l], seq)
        merged = _merge(y_a, y_b, w_branch_a, w_branch_b, l, gates)
        x2 = _resid(merged, w_out, l, x2, mod3, 2, seq,
                    tn_prefs=(512, 256, 128), name="out_proj_resid")

        h2 = _norm_mod(x2, norm2_g[l], mod3, 4, 3, seq)
        act, w_down_bf = _up_conv_gate(h2, w_up, l, conv_w, conv_b, w_down, seq)
        k_parts = 2 if act.shape[1] % (2 * V7X_LANES) == 0 else 1
        for kp in range(k_parts):
            x2 = _resid(act, w_down_bf, l, x2, mod3, 5, seq,
                        tn_prefs=(512, 256, 128), k_part=(kp, k_parts), name="down_proj_resid")

    return x2.reshape(batch, seq, d)
```

```python
import functools

import numpy as np
import jax
import jax.numpy as jnp
from jax import lax
from jax.experimental import pallas as pl
from jax.experimental.pallas import tpu as pltpu

N_HEADS = 16
HEAD_DIM = 128
N_KV_HEADS = 4
ATT_WIDTH = N_HEADS * HEAD_DIM
KV_WIDTH = N_KV_HEADS * HEAD_DIM
IDX_HEADS = 32
IDX_DIM = 64
TOPK_MAX = 256
GM_WIDTH = 2048
GM_GROUPS = 8
GM_GROUP_W = GM_WIDTH // GM_GROUPS
GM_CHUNK = 128
CONV_W = 3
EPS = 1e-6
NEG_BIG = -1e30

V7X_LANES = 128
V7X_VMEM_BYTES = 64 * 1024 * 1024
VMEM_CAP = V7X_VMEM_BYTES - 8 * 1024 * 1024

BF16 = jnp.bfloat16
F32 = jnp.float32
INT_MIN = -(2 ** 31)


def _vmem_limit(*nbytes):
    return int(min(VMEM_CAP, 2 * sum(nbytes) + (4 << 20)))


def _params(semantics, *nbytes):
    return pltpu.CompilerParams(dimension_semantics=semantics, vmem_limit_bytes=_vmem_limit(*nbytes))


def _pick(n, prefs):
    for p in prefs:
        if n % p == 0:
            return p
    return n


def _ada_kernel(c_ref, w_ref, b_ref, o_ref):
    c = c_ref[...]
    cs = c * jax.nn.sigmoid(c)
    o_ref[...] = jnp.dot(cs.astype(BF16), w_ref[...].astype(BF16),
                         preferred_element_type=F32) + b_ref[...]


def _ada(c_pad, ada_w, ada_b, l):
    bp, d = c_pad.shape
    n = ada_w.shape[2]
    tn = _pick(n, (512, 256, 128))
    return pl.pallas_call(
        _ada_kernel,
        grid=(n // tn,),
        in_specs=[pl.BlockSpec((bp, d), lambda j: (0, 0)),
                  pl.BlockSpec((None, d, tn), lambda j: (l, 0, j)),
                  pl.BlockSpec((None, 1, tn), lambda j: (l, 0, j))],
        out_specs=pl.BlockSpec((bp, tn), lambda j: (0, j)),
        out_shape=jax.ShapeDtypeStruct((bp, n), F32),
        compiler_params=_params(("arbitrary",), 2 * d * tn * 4, d * tn * 2),
        name="ada_mod",
    )(c_pad, ada_w, ada_b.reshape(ada_b.shape[0], 1, n))


def _norm_mod_kernel(x_ref, g_ref, sc_ref, sh_ref, o_ref):
    x = x_ref[...]
    y = x * lax.rsqrt(jnp.mean(x * x, axis=-1, keepdims=True) + EPS) * g_ref[...]
    o_ref[...] = (y * (1.0 + sc_ref[...]) + sh_ref[...]).astype(o_ref.dtype)


def _norm_mod(x2, g, mod3, sc_idx, sh_idx, seq):
    m, d = x2.shape
    tm = _pick(seq, (512, 256, 128, 64, 8))
    per_b = seq // tm
    return pl.pallas_call(
        _norm_mod_kernel,
        grid=(m // tm,),
        in_specs=[pl.BlockSpec((tm, d), lambda i: (i, 0)),
                  pl.BlockSpec((1, d), lambda i: (0, 0)),
                  pl.BlockSpec((None, 1, d), lambda i: (i // per_b, 0, sc_idx)),
                  pl.BlockSpec((None, 1, d), lambda i: (i // per_b, 0, sh_idx))],
        out_specs=pl.BlockSpec((tm, d), lambda i: (i, 0)),
        out_shape=jax.ShapeDtypeStruct((m, d), BF16),
        compiler_params=_params(("arbitrary",), 2 * tm * d * 4, 2 * tm * d * 2),
        name="norm_mod",
    )(x2, g.reshape(1, d), mod3, mod3)


def _gelu_exact(x):
    return 0.5 * x * (1.0 + lax.erf(x * (2.0 ** -0.5)))


MXU_ACC_ROWS = 512


_NT = (((1,), (1,)), ((), ()))


def _proj_kernel(a_ref, w_ref, *rest, epilogue, post_scale):
    o_ref = rest[-1]
    tm, tn = o_ref.shape
    rsub = min(tm, MXU_ACC_ROWS)
    w = w_ref[...].astype(BF16)
    for rb in range(tm // rsub):
        rs = slice(rb * rsub, (rb + 1) * rsub)
        acc = lax.dot_general(a_ref[rs, :], w, _NT, preferred_element_type=F32)
        if epilogue == "plain":
            out = acc
        elif epilogue == "gelu":
            out = _gelu_exact(acc)
        elif epilogue == "sigmoid":
            out = jax.nn.sigmoid(acc)
        elif epilogue == "headnorm":
            gain = rest[0][...]
            parts = []
            for c in range(tn // HEAD_DIM):
                blk = acc[:, c * HEAD_DIM:(c + 1) * HEAD_DIM]
                ms = jnp.mean(blk * blk, axis=-1, keepdims=True)
                parts.append(blk * lax.rsqrt(ms + EPS) * (gain * post_scale))
            out = jnp.concatenate(parts, axis=1) if len(parts) > 1 else parts[0]
        else:
            raise ValueError(epilogue)
        o_ref[rs, :] = out.astype(o_ref.dtype)


def _proj(a, w_t, l, row0, n, *, out_dtype, epilogue="plain", gain=None, post_scale=1.0, name):
    m, k = a.shape
    assert row0 % 8 == 0 and w_t.shape[2] == k
    tm = _pick(m, (1024, 512, 256, 128))
    tn = _pick(n, (512, 256, 128))
    in_specs = [pl.BlockSpec((tm, k), lambda i, j: (i, 0)),
                pl.BlockSpec((None, pl.Element(tn), pl.Element(k)),
                             lambda i, j: (l, pl.multiple_of(row0 + j * tn, 8), 0))]
    args = [a, w_t]
    if epilogue == "headnorm":
        in_specs.append(pl.BlockSpec((1, HEAD_DIM), lambda i, j: (0, 0)))
        args.append(gain.reshape(1, HEAD_DIM))
    osz = jnp.dtype(out_dtype).itemsize
    return pl.pallas_call(
        functools.partial(_proj_kernel, epilogue=epilogue, post_scale=post_scale),
        grid=(m // tm, n // tn),
        in_specs=in_specs,
        out_specs=pl.BlockSpec((tm, tn), lambda i, j: (i, j)),
        out_shape=jax.ShapeDtypeStruct((m, n), out_dtype),
        compiler_params=pltpu.CompilerParams(
            dimension_semantics=("arbitrary", "arbitrary"),
            vmem_limit_bytes=int(min(VMEM_CAP, 2 * tm * k * 2 + tn * k * (2 * 4 + 2) + 2 * tm * tn * osz
                                     + 3 * tm * tn * 4 + (4 << 20)))),
        name=name,
    )(*args)


def _alibi_slopes(n):
    return [2.0 ** (-8.0 * (i + 1) / n) for i in range(n)]


F32_SUBLANES = 8
BF16_SUBLANES = 16


def _fold_rows(x, op):
    rows = x.shape[0]
    while rows > F32_SUBLANES and rows % (2 * F32_SUBLANES) == 0:
        rows //= 2
        x = op(x[:rows], x[rows:])
    return x


def _reduce_rows(x, op):
    return (jnp.max if op is jnp.maximum else jnp.sum)(_fold_rows(x, op), axis=0, keepdims=True)


LOG2E = 1.4426950408889634
ALIBI_PIECES = 4
POS_SPLIT = 256


def _alibi_tables(seq):
    assert seq <= POS_SPLIT * POS_SPLIT and 2 * ALIBI_PIECES <= V7X_LANES
    slope_tab = np.zeros((N_HEADS, V7X_LANES), np.float32)
    for h, s in enumerate(_alibi_slopes(N_HEADS)):
        rest = float(np.float32(s)) * LOG2E
        for p in range(ALIBI_PIECES):
            piece = float(np.asarray(rest, dtype=BF16))
            slope_tab[h, 2 * p:2 * p + 2] = piece
            rest -= piece
    pos = np.arange(seq)
    pos_tab = np.zeros((seq, V7X_LANES), np.float32)
    pos_tab[:, 0:2 * ALIBI_PIECES:2] = ((pos // POS_SPLIT) * POS_SPLIT)[:, None]
    pos_tab[:, 1:2 * ALIBI_PIECES:2] = (pos % POS_SPLIT)[:, None]
    return jnp.asarray(slope_tab), jnp.asarray(pos_tab, dtype=BF16)


N_CAST = 3


def _attn_kernel(q_ref, qi_ref, kwq_ref, k_ref, v_ref, kwa_ref, slope_ref, pos_ref, *rest, tq, seq, topk):
    cast_in, (y_ref, *cast_out) = rest[:N_CAST], rest[N_CAST:2 * N_CAST + 1]
    keys_scr, bias_scr, wt_scr, sel_scr, vt_scr, qa_scr, ml_scr, acc_scr = rest[2 * N_CAST + 1:]
    for src, dst in zip(cast_in, cast_out):
        dst[...] = src[...].astype(BF16)
    i = pl.program_id(1)
    n_chunks = i + 1
    t0 = i * tq
    rep = N_HEADS // N_KV_HEADS
    nt = (((1,), (1,)), ((), ()))
    row = lax.broadcasted_iota(jnp.int32, (tq, tq), 0)
    col = lax.broadcasted_iota(jnp.int32, (tq, tq), 1)
    kf = float(topk)

    w_fold = (IDX_HEADS ** -0.5) * (IDX_DIM ** -0.5)
    wt_scr[...] = (kwq_ref[...] * w_fold).T
    lane = lax.broadcasted_iota(jnp.int32, (tq, V7X_LANES), 1)

    def score_body(c, carry):
        off = pl.multiple_of(c * tq, tq)
        kraw = kwa_ref[pl.ds(off, tq), :]
        k_even = jnp.where(lane < IDX_DIM, kraw, 0.0)
        k_odd = jnp.where(lane >= IDX_DIM, pltpu.roll(kraw, IDX_DIM, 1), 0.0)
        lhs = jnp.concatenate([k_even, k_odd], axis=0).astype(BF16)
        acc = jnp.zeros((tq, tq), F32)
        for p in range(IDX_HEADS // 2):
            lg = lax.dot_general(lhs, qi_ref[:, p * V7X_LANES:(p + 1) * V7X_LANES], nt,
                                 preferred_element_type=F32)
            h0 = IDX_DIM + 2 * p
            acc = acc + wt_scr[h0:h0 + 1, :] * jnp.maximum(lg[:tq], 0.0)
            acc = acc + wt_scr[h0 + 1:h0 + 2, :] * jnp.maximum(lg[tq:], 0.0)
        bits = lax.bitcast_convert_type(acc, jnp.int32)
        key = bits ^ ((bits >> 31) & jnp.int32(0x7FFFFFFF))
        keys_scr[c] = jnp.where(off + row <= t0 + col, key, jnp.int32(INT_MIN))
        return carry

    lax.fori_loop(0, n_chunks, score_body, 0)

    def count(pred):
        def body(c, part):
            off = c * tq
            return part + _fold_rows(jnp.where(pred(keys_scr[c], off + row), 1.0, 0.0), jnp.add)
        part = lax.fori_loop(0, n_chunks, body, jnp.zeros((F32_SUBLANES, tq), F32))
        return jnp.sum(part, axis=0, keepdims=True)

    def select_topk():
        cnt_nonneg = count(lambda kb, s: kb >= 0)
        nonneg = cnt_nonneg >= kf
        prefix0 = jnp.where(nonneg, jnp.int32(0), jnp.int32(INT_MIN))
        cnt0 = jnp.where(nonneg, cnt_nonneg, (n_chunks * tq).astype(F32))

        def bit_body(j, carry):
            prefix, cnt_prefix = carry
            cand = prefix | lax.shift_left(jnp.int32(1), 30 - j)
            cnt = count(lambda kb, s: kb >= cand)
            keep = cnt >= kf
            return jnp.where(keep, cand, prefix), jnp.where(keep, cnt, cnt_prefix)

        thr, cnt_ge = lax.fori_loop(0, 31, bit_body, (prefix0, cnt0))
        sel_scr[0:1, :] = thr
        sel_scr[1:2, :] = jnp.full((1, tq), seq, jnp.int32)

        @pl.when(jnp.max(cnt_ge) > kf)
        def _():
            nbits = max(1, (seq - 1).bit_length())
            need = kf - count(lambda kb, s: kb > thr)

            def idx_body(j, m):
                cand = m | lax.shift_left(jnp.int32(1), nbits - 1 - j)
                cnt = count(lambda kb, s: (kb == thr) & (s < cand))
                return jnp.where(cnt < need, cand, m)

            sel_scr[1:2, :] = lax.fori_loop(0, nbits, idx_body, jnp.zeros((1, tq), jnp.int32))

    if tq <= topk:
        @pl.when(i == 0)
        def _():
            sel_scr[0:1, :] = jnp.full((1, tq), INT_MIN, jnp.int32)
            sel_scr[1:2, :] = jnp.full((1, tq), -1, jnp.int32)

        pl.when(i > 0)(select_topk)
    else:
        select_topk()

    thr = sel_scr[0:1, :]
    m_idx = sel_scr[1:2, :]

    def bias_body(c, carry):
        off = c * tq
        kb = keys_scr[c]
        s_idx = off + row
        take = jnp.where(kb > thr, 1.0, jnp.where((kb == thr) & (s_idx <= m_idx), 1.0, 0.0))
        take = jnp.where(s_idx <= t0 + col, take, 0.0)
        bias_scr[c] = jnp.where(take > 0.5, 0.0, NEG_BIG)
        return carry

    lax.fori_loop(0, n_chunks, bias_body, 0)

    @pl.when(i == 0)
    def _():
        for c in range(seq // tq):
            for g in range(N_KV_HEADS):
                blk = v_ref[c * tq:(c + 1) * tq, g * HEAD_DIM:(g + 1) * HEAD_DIM].astype(F32)
                vt_scr[c, g * HEAD_DIM:(g + 1) * HEAD_DIM, :] = blk.T.astype(BF16)

    for h in range(N_HEADS):
        g, r = divmod(h, rep)
        qa_scr[g, r * tq:(r + 1) * tq, :HEAD_DIM] = q_ref[:, h * HEAD_DIM:(h + 1) * HEAD_DIM]
        qa_scr[g, r * tq:(r + 1) * tq, HEAD_DIM:] = jnp.broadcast_to(
            slope_ref[h:h + 1, :], (tq, V7X_LANES)).astype(BF16)
    acc_scr[...] = jnp.zeros(acc_scr.shape, F32)
    for g in range(N_KV_HEADS):
        ml_scr[2 * g:2 * g + 1, :] = jnp.full((1, rep * tq), NEG_BIG, F32)
        ml_scr[2 * g + 1:2 * g + 2, :] = jnp.zeros((1, rep * tq), F32)

    def attn_body(c, carry):
        off = pl.multiple_of(c * tq, tq)
        pos_c = pos_ref[pl.ds(off, tq), :]
        bias = jnp.concatenate([bias_scr[c]] * rep, axis=1)
        def scores(g):
            kc = jnp.concatenate([k_ref[pl.ds(off, tq), g * HEAD_DIM:(g + 1) * HEAD_DIM], pos_c],
                                 axis=1)
            return lax.dot_general(kc, qa_scr[g], nt, preferred_element_type=F32) + bias

        lead = N_KV_HEADS - 1
        sts = [scores(g) for g in range(lead)]
        for g in range(N_KV_HEADS):
            st = sts[g]
            if g + lead < N_KV_HEADS:
                sts.append(scores(g + lead))
            vt = vt_scr[c, g * HEAD_DIM:(g + 1) * HEAD_DIM, :]
            m_prev = ml_scr[2 * g:2 * g + 1, :]
            m_new = jnp.maximum(m_prev, _reduce_rows(st, jnp.maximum))
            alpha = jnp.exp2(m_prev - m_new)
            pt = jnp.exp2(st - m_new)
            ml_scr[2 * g:2 * g + 1, :] = m_new
            ml_scr[2 * g + 1:2 * g + 2, :] = (alpha * ml_scr[2 * g + 1:2 * g + 2, :]
                                              + _reduce_rows(pt, jnp.add))
            acc_scr[g] = alpha * acc_scr[g] + jnp.dot(vt, pt.astype(BF16), preferred_element_type=F32)
        return carry

    lax.fori_loop(0, n_chunks, attn_body, 0)
    for g in range(N_KV_HEADS):
        o_t = acc_scr[g] / ml_scr[2 * g + 1:2 * g + 2, :]
        for r in range(rep):
            h = g * rep + r
            y_ref[:, h * HEAD_DIM:(h + 1) * HEAD_DIM] = o_t[:, r * tq:(r + 1) * tq].T.astype(y_ref.dtype)


def _attention(q, qi, kw, k, v, batch, seq, cast_weights, l):
    m = q.shape[0]
    topk = min(TOPK_MAX, seq // 4)
    tq = _pick(seq, (256, 128))
    nq = seq // tq
    kww = kw.shape[1]
    rep = N_HEADS // N_KV_HEADS
    assert HEAD_DIM == V7X_LANES and tq % V7X_LANES == 0
    assert kww == V7X_LANES == 2 * IDX_DIM and IDX_DIM + IDX_HEADS <= V7X_LANES
    n_steps = batch * nq
    assert len(cast_weights) == N_CAST and all(w.shape[1] % (n_steps * BF16_SUBLANES) == 0 for w in cast_weights)
    slabs = [w.shape[1] // n_steps for w in cast_weights]
    cast_bytes = sum(2 * s * w.shape[2] * 6 for s, w in zip(slabs, cast_weights))
    outs = pl.pallas_call(
        functools.partial(_attn_kernel, tq=tq, seq=seq, topk=topk),
        grid=(batch, nq),
        in_specs=[pl.BlockSpec((tq, ATT_WIDTH), lambda b, i: (b * nq + i, 0)),
                  pl.BlockSpec((tq, IDX_HEADS * IDX_DIM), lambda b, i: (b * nq + i, 0)),
                  pl.BlockSpec((tq, kww), lambda b, i: (b * nq + i, 0)),
                  pl.BlockSpec((seq, KV_WIDTH), lambda b, i: (b, 0)),
                  pl.BlockSpec((seq, KV_WIDTH), lambda b, i: (b, 0)),
                  pl.BlockSpec((seq, kww), lambda b, i: (b, 0)),
                  pl.BlockSpec((N_HEADS, V7X_LANES), lambda b, i: (0, 0)),
                  pl.BlockSpec((seq, V7X_LANES), lambda b, i: (0, 0))]
        + [pl.BlockSpec((None, s, w.shape[2]), lambda b, i: (l, b * nq + i, 0))
           for s, w in zip(slabs, cast_weights)],
        out_specs=[pl.BlockSpec((tq, ATT_WIDTH), lambda b, i: (b * nq + i, 0))]
        + [pl.BlockSpec((s, w.shape[2]), lambda b, i: (b * nq + i, 0)) for s, w in zip(slabs, cast_weights)],
        out_shape=[jax.ShapeDtypeStruct((m, ATT_WIDTH), BF16)]
        + [jax.ShapeDtypeStruct(w.shape[1:], BF16) for w in cast_weights],
        scratch_shapes=[pltpu.VMEM((nq, tq, tq), jnp.int32),
                        pltpu.VMEM((nq, tq, tq), F32),
                        pltpu.VMEM((V7X_LANES, tq), F32),
                        pltpu.VMEM((8, tq), jnp.int32),
                        pltpu.VMEM((nq, KV_WIDTH, tq), BF16),
                        pltpu.VMEM((N_KV_HEADS, rep * tq, 2 * HEAD_DIM), BF16),
                        pltpu.VMEM((2 * N_KV_HEADS, rep * tq), F32),
                        pltpu.VMEM((N_KV_HEADS, HEAD_DIM, rep * tq), F32)],
        compiler_params=_params(("arbitrary", "arbitrary"),
                                6 * tq * ATT_WIDTH * 2, 4 * seq * KV_WIDTH * 2, 2 * (seq + tq) * kww * 4,
                                2 * seq * tq * 4, 8 * tq * tq * 4, seq * KV_WIDTH * 2, rep * HEAD_DIM * tq * 4,
                                cast_bytes // 2),
        name="dsa_attention",
    )(q, qi, kw, k, v, kw, *_alibi_tables(seq), *cast_weights)
    return outs[0], outs[1:]


def _sgu_kernel(u_ref, v_ref, g_ref, w_ref, bt_ref, o_ref, *, rows):
    v = v_ref[...].astype(F32)
    vn = (v * lax.rsqrt(jnp.mean(v * v, axis=-1, keepdims=True) + EPS) * g_ref[...]).astype(BF16)
    r_i = lax.broadcasted_iota(jnp.int32, (GM_CHUNK, GM_CHUNK), 0)
    c_i = lax.broadcasted_iota(jnp.int32, (GM_CHUNK, GM_CHUNK), 1)
    for g in range(GM_GROUPS):
        w = jnp.where(r_i >= c_i, w_ref[g], 0.0).astype(BF16)
        bcol = bt_ref[:, g:g + 1]
        cs = slice(g * GM_GROUP_W, (g + 1) * GM_GROUP_W)
        for n in range(rows // GM_CHUNK):
            rs = slice(n * GM_CHUNK, (n + 1) * GM_CHUNK)
            f = jnp.dot(w, vn[rs, cs], preferred_element_type=F32) + bcol
            o_ref[rs, cs] = (u_ref[rs, cs].astype(F32) * f).astype(o_ref.dtype)


def _sgu(uv, gain, w_s, b_s, seq):
    m = uv.shape[0]
    rows = _pick(seq, (512, 256, 128))
    return pl.pallas_call(
        functools.partial(_sgu_kernel, rows=rows),
        grid=(m // rows,),
        in_specs=[pl.BlockSpec((rows, GM_WIDTH), lambda i: (i, 0)),
                  pl.BlockSpec((rows, GM_WIDTH), lambda i: (i, 1)),
                  pl.BlockSpec((1, GM_WIDTH), lambda i: (0, 0)),
                  pl.BlockSpec((GM_GROUPS, GM_CHUNK, GM_CHUNK), lambda i: (0, 0, 0)),
                  pl.BlockSpec((GM_CHUNK, GM_GROUPS), lambda i: (0, 0))],
        out_specs=pl.BlockSpec((rows, GM_WIDTH), lambda i: (i, 0)),
        out_shape=jax.ShapeDtypeStruct((m, GM_WIDTH), BF16),
        compiler_params=_params(("arbitrary",), 6 * rows * GM_WIDTH * 2, 2 * rows * GM_WIDTH * 4),
        name="sgu",
    )(uv, uv, gain.reshape(1, GM_WIDTH), w_s, b_s.T)


def _w_spec(w, l, k, tn, j0=0):
    if w.ndim == 3:
        return pl.BlockSpec((None, k, tn), lambda i, j: (l, 0, j + j0))
    return pl.BlockSpec((k, tn), lambda i, j: (0, j + j0))


def _merge_kernel(ya_ref, yb_ref, wa_ref, wb_ref, ga_ref, gb_ref, o_ref):
    pa = jnp.dot(ya_ref[...], wa_ref[...].astype(BF16), preferred_element_type=F32)
    pb = jnp.dot(yb_ref[...], wb_ref[...].astype(BF16), preferred_element_type=F32)
    o_ref[...] = (ga_ref[...].astype(F32) * pa + gb_ref[...].astype(F32) * pb).astype(o_ref.dtype)


def _merge(ya, yb, wa, wb, l, gates):
    m, ka = ya.shape
    kb = yb.shape[1]
    d = wa.shape[-1]
    tm = _pick(m, (1024, 512, 256, 128))
    tn = _pick(d, (512, 256, 128))
    nj = d // tn
    wsz = wa.dtype.itemsize
    return pl.pallas_call(
        _merge_kernel,
        grid=(m // tm, nj),
        in_specs=[pl.BlockSpec((tm, ka), lambda i, j: (i, 0)),
                  pl.BlockSpec((tm, kb), lambda i, j: (i, 0)),
                  _w_spec(wa, l, ka, tn),
                  _w_spec(wb, l, kb, tn),
                  pl.BlockSpec((tm, tn), lambda i, j: (i, j)),
                  pl.BlockSpec((tm, tn), lambda i, j: (i, j + nj))],
        out_specs=pl.BlockSpec((tm, tn), lambda i, j: (i, j)),
        out_shape=jax.ShapeDtypeStruct((m, d), BF16),
        compiler_params=pltpu.CompilerParams(
            dimension_semantics=("arbitrary", "arbitrary"),
            vmem_limit_bytes=int(min(VMEM_CAP, 2 * tm * (ka + kb) * 2 + (ka + kb) * tn * (2 * wsz + 2)
                                     + 6 * tm * tn * 2 + 3 * tm * tn * 4 + (4 << 20)))),
        name="merge",
    )(ya, yb, wa, wb, gates, gates)


def _resid_kernel(a_ref, w_ref, x_ref, g_ref, o_ref):
    acc = jnp.dot(a_ref[...], w_ref[...].astype(BF16), preferred_element_type=F32)
    o_ref[...] = x_ref[...] + g_ref[...] * acc


def _resid(a, w, l, x2, mod3, g_idx, seq, *, tn_prefs, k_part=(0, 1), name):
    m = a.shape[0]
    kp_idx, kp_n = k_part
    k = a.shape[1] // kp_n
    d = w.shape[-1]
    assert a.shape[1] % kp_n == 0 and k % V7X_LANES == 0 and (kp_n == 1 or w.ndim == 2)
    tm = _pick(seq, (1024, 512, 256, 128))
    per_b = seq // tm
    tn = _pick(d, tn_prefs)
    nj = d // tn
    wsz = w.dtype.itemsize
    w_spec = _w_spec(w, l, k, tn) if w.ndim == 3 else pl.BlockSpec((k, tn), lambda i, j: (kp_idx, j))
    return pl.pallas_call(
        _resid_kernel,
        grid=(m // tm, nj),
        in_specs=[pl.BlockSpec((tm, k), lambda i, j: (i, kp_idx)),
                  w_spec,
                  pl.BlockSpec((tm, tn), lambda i, j: (i, j)),
                  pl.BlockSpec((None, 1, tn), lambda i, j: (i // per_b, 0, g_idx * nj + j))],
        out_specs=pl.BlockSpec((tm, tn), lambda i, j: (i, j)),
        out_shape=jax.ShapeDtypeStruct((m, d), F32),
        compiler_params=pltpu.CompilerParams(
            dimension_semantics=("arbitrary", "arbitrary"),
            vmem_limit_bytes=int(min(VMEM_CAP, 2 * tm * k * 2 + k * tn * (2 * wsz + 2)
                                     + 6 * tm * tn * 4 + (4 << 20)))),
        name=name,
    )(a, w, x2, mod3)


HALO = 8


def _up_kernel(h_ref, wg_ref, wv_ref, cwg_ref, cwv_ref, cbg_ref, cbv_ref, wd_ref, o_ref, wd_o_ref, *, sub):
    wd_o_ref[...] = wd_ref[...].astype(BF16)
    wg = wg_ref[...].astype(BF16)
    wv = wv_ref[...].astype(BF16)
    tn = wg.shape[1]

    def conv(a, halo, cw_ref, cb_ref):
        ext = jnp.concatenate([halo, a], axis=0)
        acc = cb_ref[...] + pltpu.roll(ext, 2, 0)[HALO:] * cw_ref[0:1, :]
        acc = acc + pltpu.roll(ext, 1, 0)[HALO:] * cw_ref[1:2, :]
        return acc + a * cw_ref[2:3, :]

    halo_g = halo_v = jnp.zeros((HALO, tn), F32)
    for s in range(h_ref.shape[0] // sub):
        rs = slice(s * sub, (s + 1) * sub)
        hs = h_ref[rs, :]
        ag = jnp.dot(hs, wg, preferred_element_type=F32)
        av = jnp.dot(hs, wv, preferred_element_type=F32)
        gate = conv(ag, halo_g, cwg_ref, cbg_ref)
        val = conv(av, halo_v, cwv_ref, cbv_ref)
        o_ref[rs, :] = (gate * jax.nn.sigmoid(gate) * val).astype(o_ref.dtype)
        halo_g, halo_v = ag[sub - HALO:], av[sub - HALO:]


def _up_conv_gate(h2, w_up, l, conv_w, conv_b, w_down, seq):
    m, k = h2.shape
    f = w_up.shape[-1] // 2
    d_out = w_down.shape[-1]
    tn = _pick(f, (256, 128))
    nj = f // tn
    n_steps = (m // seq) * nj
    assert f % (n_steps * BF16_SUBLANES) == 0
    slab = f // n_steps
    sub = _pick(seq, (512, 256, 128))
    wsz = w_up.dtype.itemsize
    return pl.pallas_call(
        functools.partial(_up_kernel, sub=sub),
        grid=(m // seq, nj),
        in_specs=[pl.BlockSpec((seq, k), lambda i, j: (i, 0), pipeline_mode=pl.Buffered(1)),
                  _w_spec(w_up, l, k, tn),
                  _w_spec(w_up, l, k, tn, nj),
                  pl.BlockSpec((None, CONV_W, tn), lambda i, j: (l, 0, j)),
                  pl.BlockSpec((None, CONV_W, tn), lambda i, j: (l, 0, j + nj)),
                  pl.BlockSpec((None, 1, tn), lambda i, j: (l, 0, j)),
                  pl.BlockSpec((None, 1, tn), lambda i, j: (l, 0, j + nj)),
                  pl.BlockSpec((None, slab, d_out), lambda i, j: (l, i * nj + j, 0))],
        out_specs=[pl.BlockSpec((seq, tn), lambda i, j: (i, j)),
                   pl.BlockSpec((slab, d_out), lambda i, j: (i * nj + j, 0))],
        out_shape=[jax.ShapeDtypeStruct((m, f), BF16),
                   jax.ShapeDtypeStruct((f, d_out), BF16)],
        compiler_params=pltpu.CompilerParams(
            dimension_semantics=("arbitrary", "arbitrary"),
            vmem_limit_bytes=int(min(VMEM_CAP, seq * k * 2 + 2 * k * tn * (2 * wsz + 2) + 2 * seq * tn * 2
                                     + 12 * sub * tn * 4 + 2 * slab * d_out * 6 + (4 << 20)))),
        name="up_conv_gate",
    )(h2, w_up, w_up, conv_w, conv_w, conv_b.reshape(conv_b.shape[0], 1, 2 * f),
      conv_b.reshape(conv_b.shape[0], 1, 2 * f), w_down)


def kernel(x, c, ada_w, ada_b, norm1_g, w_in, q_norm_g, k_norm_g, sgu_norm_g, sgu_w, sgu_b,
           w_branch_a, w_branch_b, w_out, norm2_g, w_up, conv_w, conv_b, w_down):
    batch, seq, d = x.shape
    m = batch * seq
    depth = ada_w.shape[0]
    assert seq % GM_CHUNK == 0 and d % V7X_LANES == 0

    x2 = x.reshape(m, d)
    bp = -(-batch // 8) * 8
    c_pad = jnp.pad(c, ((0, bp - batch), (0, 0)))

    o_q = 0
    o_k = o_q + ATT_WIDTH
    o_v = o_k + KV_WIDTH
    o_qi = o_v + KV_WIDTH
    o_ki = o_qi + IDX_HEADS * IDX_DIM
    o_wi = o_ki + IDX_DIM
    o_gu = o_wi + IDX_HEADS
    o_ga = o_gu + 2 * GM_WIDTH
    o_end = o_ga + 2 * d

    assert o_gu - o_ki <= V7X_LANES and o_end == w_in.shape[2]
    w_in_t = jnp.swapaxes(w_in, 1, 2)

    for l in range(depth):
        mod = _ada(c_pad, ada_w, ada_b, l)
        mod3 = mod.reshape(bp, 1, 6 * d)

        h = _norm_mod(x2, norm1_g[l], mod3, 1, 0, seq)
        q = _proj(h, w_in_t, l, o_q, ATT_WIDTH, out_dtype=BF16, epilogue="headnorm", gain=q_norm_g[l],
                  post_scale=HEAD_DIM ** -0.5 * LOG2E, name="proj_q")
        k = _proj(h, w_in_t, l, o_k, KV_WIDTH, out_dtype=BF16, epilogue="headnorm", gain=k_norm_g[l],
                  name="proj_k")
        v = _proj(h, w_in_t, l, o_v, KV_WIDTH, out_dtype=BF16, name="proj_v")
        qi = _proj(h, w_in_t, l, o_qi, IDX_HEADS * IDX_DIM, out_dtype=BF16, name="proj_qi")
        kw = _proj(h, w_in_t, l, o_ki, V7X_LANES, out_dtype=F32, name="proj_kw")
        uv = _proj(h, w_in_t, l, o_gu, 2 * GM_WIDTH, out_dtype=BF16, epilogue="gelu", name="proj_uv")
        gates = _proj(h, w_in_t, l, o_ga, 2 * d, out_dtype=BF16, epilogue="sigmoid", name="proj_gates")

        y_a, (w_out_bf, w_a_bf, w_b_bf) = _attention(q, qi, kw, k, v, batch, seq,
                                                     (w_out, w_branch_a, w_branch_b), l)
        y_b = _sgu(uv, sgu_norm_g[l], sgu_w[l], sgu_b[l], seq)
        merged = _merge(y_a, y_b, w_a_bf, w_b_bf, l, gates)
        x2 = _resid(merged, w_out_bf, l, x2, mod3, 2, seq,
                    tn_prefs=(512, 256, 128), name="out_proj_resid")

        h2 = _norm_mod(x2, norm2_g[l], mod3, 4, 3, seq)
        act, w_down_bf = _up_conv_gate(h2, w_up, l, conv_w, conv_b, w_down, seq)
        k_parts = 2 if act.shape[1] % (2 * V7X_LANES) == 0 else 1
        for kp in range(k_parts):
            x2 = _resid(act, w_down_bf, l, x2, mod3, 5, seq,
                        tn_prefs=(512, 256, 128), k_part=(kp, k_parts), name="down_proj_resid")

    return x2.reshape(batch, seq, d)
```

```python
import functools

import numpy as np
import jax
import jax.numpy as jnp
from jax import lax
from jax.experimental import pallas as pl
from jax.experimental.pallas import tpu as pltpu

N_HEADS = 16
HEAD_DIM = 128
N_KV_HEADS = 4
ATT_WIDTH = N_HEADS * HEAD_DIM
KV_WIDTH = N_KV_HEADS * HEAD_DIM
IDX_HEADS = 32
IDX_DIM = 64
TOPK_MAX = 256
GM_WIDTH = 2048
GM_GROUPS = 8
GM_GROUP_W = GM_WIDTH // GM_GROUPS
GM_CHUNK = 128
CONV_W = 3
EPS = 1e-6
NEG_BIG = -1e30

V7X_LANES = 128
V7X_VMEM_BYTES = 64 * 1024 * 1024
VMEM_CAP = V7X_VMEM_BYTES - 8 * 1024 * 1024

BF16 = jnp.bfloat16
F32 = jnp.float32
INT_MIN = -(2 ** 31)


def _vmem_limit(*nbytes):
    return int(min(VMEM_CAP, 2 * sum(nbytes) + (4 << 20)))


def _params(semantics, *nbytes):
    return pltpu.CompilerParams(dimension_semantics=semantics, vmem_limit_bytes=_vmem_limit(*nbytes))


def _pick(n, prefs):
    for p in prefs:
        if n % p == 0:
            return p
    return n


def _ada_kernel(c_ref, w_ref, b_ref, o_ref):
    c = c_ref[...]
    cs = c * jax.nn.sigmoid(c)
    o_ref[...] = jnp.dot(cs.astype(BF16), w_ref[...].astype(BF16),
                         preferred_element_type=F32) + b_ref[...]


def _ada(c_pad, ada_w, ada_b, l):
    bp, d = c_pad.shape
    n = ada_w.shape[2]
    tn = _pick(n, (512, 256, 128))
    return pl.pallas_call(
        _ada_kernel,
        grid=(n // tn,),
        in_specs=[pl.BlockSpec((bp, d), lambda j: (0, 0)),
                  pl.BlockSpec((None, d, tn), lambda j: (l, 0, j)),
                  pl.BlockSpec((None, 1, tn), lambda j: (l, 0, j))],
        out_specs=pl.BlockSpec((bp, tn), lambda j: (0, j)),
        out_shape=jax.ShapeDtypeStruct((bp, n), F32),
        compiler_params=_params(("arbitrary",), 2 * d * tn * 4, d * tn * 2),
        name="ada_mod",
    )(c_pad, ada_w, ada_b.reshape(ada_b.shape[0], 1, n))


def _norm_mod_kernel(x_ref, g_ref, sc_ref, sh_ref, o_ref):
    x = x_ref[...]
    y = x * lax.rsqrt(jnp.mean(x * x, axis=-1, keepdims=True) + EPS) * g_ref[...]
    o_ref[...] = (y * (1.0 + sc_ref[...]) + sh_ref[...]).astype(o_ref.dtype)


def _norm_mod(x2, g, mod3, sc_idx, sh_idx, seq):
    m, d = x2.shape
    tm = _pick(seq, (512, 256, 128, 64, 8))
    per_b = seq // tm
    return pl.pallas_call(
        _norm_mod_kernel,
        grid=(m // tm,),
        in_specs=[pl.BlockSpec((tm, d), lambda i: (i, 0)),
                  pl.BlockSpec((1, d), lambda i: (0, 0)),
                  pl.BlockSpec((None, 1, d), lambda i: (i // per_b, 0, sc_idx)),
                  pl.BlockSpec((None, 1, d), lambda i: (i // per_b, 0, sh_idx))],
        out_specs=pl.BlockSpec((tm, d), lambda i: (i, 0)),
        out_shape=jax.ShapeDtypeStruct((m, d), BF16),
        compiler_params=_params(("arbitrary",), 2 * tm * d * 4, 2 * tm * d * 2),
        name="norm_mod",
    )(x2, g.reshape(1, d), mod3, mod3)


def _gelu_exact(x):
    return 0.5 * x * (1.0 + lax.erf(x * (2.0 ** -0.5)))


MXU_ACC_ROWS = 512


_NT = (((1,), (1,)), ((), ()))


def _proj_kernel(a_ref, w_ref, *rest, epilogue, post_scale):
    o_ref = rest[-1]
    tm, tn = o_ref.shape
    rsub = min(tm, MXU_ACC_ROWS)
    w = w_ref[...].astype(BF16)
    for rb in range(tm // rsub):
        rs = slice(rb * rsub, (rb + 1) * rsub)
        acc = lax.dot_general(a_ref[rs, :], w, _NT, preferred_element_type=F32)
        if epilogue == "plain":
            out = acc
        elif epilogue == "gelu":
            out = _gelu_exact(acc)
        elif epilogue == "sigmoid":
            out = jax.nn.sigmoid(acc)
        elif epilogue == "headnorm":
            gain = rest[0][...]
            parts = []
            for c in range(tn // HEAD_DIM):
                blk = acc[:, c * HEAD_DIM:(c + 1) * HEAD_DIM]
                ms = jnp.mean(blk * blk, axis=-1, keepdims=True)
                parts.append(blk * lax.rsqrt(ms + EPS) * (gain * post_scale))
            out = jnp.concatenate(parts, axis=1) if len(parts) > 1 else parts[0]
        else:
            raise ValueError(epilogue)
        o_ref[rs, :] = out.astype(o_ref.dtype)


def _proj(a, w_t, l, row0, n, *, out_dtype, epilogue="plain", gain=None, post_scale=1.0, name):
    m, k = a.shape
    assert row0 % 8 == 0 and w_t.shape[2] == k
    tm = _pick(m, (1024, 512, 256, 128))
    tn = _pick(n, (512, 256, 128))
    in_specs = [pl.BlockSpec((tm, k), lambda i, j: (i, 0)),
                pl.BlockSpec((None, pl.Element(tn), pl.Element(k)),
                             lambda i, j: (l, pl.multiple_of(row0 + j * tn, 8), 0))]
    args = [a, w_t]
    if epilogue == "headnorm":
        in_specs.append(pl.BlockSpec((1, HEAD_DIM), lambda i, j: (0, 0)))
        args.append(gain.reshape(1, HEAD_DIM))
    osz = jnp.dtype(out_dtype).itemsize
    return pl.pallas_call(
        functools.partial(_proj_kernel, epilogue=epilogue, post_scale=post_scale),
        grid=(m // tm, n // tn),
        in_specs=in_specs,
        out_specs=pl.BlockSpec((tm, tn), lambda i, j: (i, j)),
        out_shape=jax.ShapeDtypeStruct((m, n), out_dtype),
        compiler_params=pltpu.CompilerParams(
            dimension_semantics=("arbitrary", "arbitrary"),
            vmem_limit_bytes=int(min(VMEM_CAP, 2 * tm * k * 2 + tn * k * (2 * 4 + 2) + 2 * tm * tn * osz
                                     + 3 * tm * tn * 4 + (4 << 20)))),
        name=name,
    )(*args)


def _alibi_slopes(n):
    return [2.0 ** (-8.0 * (i + 1) / n) for i in range(n)]


F32_SUBLANES = 8
BF16_SUBLANES = 16


def _fold_rows(x, op):
    rows = x.shape[0]
    while rows > F32_SUBLANES and rows % (2 * F32_SUBLANES) == 0:
        rows //= 2
        x = op(x[:rows], x[rows:])
    return x


def _reduce_rows(x, op):
    return (jnp.max if op is jnp.maximum else jnp.sum)(_fold_rows(x, op), axis=0, keepdims=True)


LOG2E = 1.4426950408889634
ALIBI_PIECES = 4
POS_SPLIT = 256


def _alibi_tables(seq):
    assert seq <= POS_SPLIT * POS_SPLIT and 2 * ALIBI_PIECES <= V7X_LANES
    slope_tab = np.zeros((N_HEADS, V7X_LANES), np.float32)
    for h, s in enumerate(_alibi_slopes(N_HEADS)):
        rest = float(np.float32(s)) * LOG2E
        for p in range(ALIBI_PIECES):
            piece = float(np.asarray(rest, dtype=BF16))
            slope_tab[h, 2 * p:2 * p + 2] = piece
            rest -= piece
    pos = np.arange(seq)
    pos_tab = np.zeros((seq, V7X_LANES), np.float32)
    pos_tab[:, 0:2 * ALIBI_PIECES:2] = ((pos // POS_SPLIT) * POS_SPLIT)[:, None]
    pos_tab[:, 1:2 * ALIBI_PIECES:2] = (pos % POS_SPLIT)[:, None]
    return jnp.asarray(slope_tab), jnp.asarray(pos_tab, dtype=BF16)


N_CAST = 3


def _attn_kernel(q_ref, qi_ref, kwq_ref, k_ref, v_ref, kwa_ref, slope_ref, pos_ref, *rest, tq, seq, topk):
    cast_in, (y_ref, *cast_out) = rest[:N_CAST], rest[N_CAST:2 * N_CAST + 1]
    keys_scr, bias_scr, wt_scr, sel_scr, vt_scr, qa_scr, ml_scr, acc_scr = rest[2 * N_CAST + 1:]
    for src, dst in zip(cast_in, cast_out):
        dst[...] = src[...].astype(BF16)
    i = pl.program_id(1)
    n_chunks = i + 1
    t0 = i * tq
    rep = N_HEADS // N_KV_HEADS
    nt = (((1,), (1,)), ((), ()))
    row = lax.broadcasted_iota(jnp.int32, (tq, tq), 0)
    col = lax.broadcasted_iota(jnp.int32, (tq, tq), 1)
    kf = float(topk)

    w_fold = (IDX_HEADS ** -0.5) * (IDX_DIM ** -0.5)
    wt_scr[...] = (kwq_ref[...] * w_fold).T
    lane = lax.broadcasted_iota(jnp.int32, (tq, V7X_LANES), 1)

    def score_body(c, carry):
        off = pl.multiple_of(c * tq, tq)
        kraw = kwa_ref[pl.ds(off, tq), :]
        k_even = jnp.where(lane < IDX_DIM, kraw, 0.0)
        k_odd = jnp.where(lane >= IDX_DIM, pltpu.roll(kraw, IDX_DIM, 1), 0.0)
        lhs = jnp.concatenate([k_even, k_odd], axis=0).astype(BF16)
        acc = jnp.zeros((tq, tq), F32)
        for p in range(IDX_HEADS // 2):
            lg = lax.dot_general(lhs, qi_ref[:, p * V7X_LANES:(p + 1) * V7X_LANES], nt,
                                 preferred_element_type=F32)
            h0 = IDX_DIM + 2 * p
            acc = acc + wt_scr[h0:h0 + 1, :] * jnp.maximum(lg[:tq], 0.0)
            acc = acc + wt_scr[h0 + 1:h0 + 2, :] * jnp.maximum(lg[tq:], 0.0)
        bits = lax.bitcast_convert_type(acc, jnp.int32)
        key = bits ^ ((bits >> 31) & jnp.int32(0x7FFFFFFF))
        keys_scr[c] = jnp.where(off + row <= t0 + col, key, jnp.int32(INT_MIN))
        return carry

    lax.fori_loop(0, n_chunks, score_body, 0)

    def count(pred):
        def body(c, part):
            off = c * tq
            return part + _fold_rows(jnp.where(pred(keys_scr[c], off + row), 1.0, 0.0), jnp.add)
        part = lax.fori_loop(0, n_chunks, body, jnp.zeros((F32_SUBLANES, tq), F32))
        return jnp.sum(part, axis=0, keepdims=True)

    def select_topk():
        cnt_nonneg = count(lambda kb, s: kb >= 0)
        nonneg = cnt_nonneg >= kf
        prefix0 = jnp.where(nonneg, jnp.int32(0), jnp.int32(INT_MIN))
        cnt0 = jnp.where(nonneg, cnt_nonneg, (n_chunks * tq).astype(F32))

        def bit_body(j, carry):
            prefix, cnt_prefix = carry
            cand = prefix | lax.shift_left(jnp.int32(1), 30 - j)
            cnt = count(lambda kb, s: kb >= cand)
            keep = cnt >= kf
            return jnp.where(keep, cand, prefix), jnp.where(keep, cnt, cnt_prefix)

        thr, cnt_ge = lax.fori_loop(0, 31, bit_body, (prefix0, cnt0))
        sel_scr[0:1, :] = thr
        sel_scr[1:2, :] = jnp.full((1, tq), seq, jnp.int32)

        @pl.when(jnp.max(cnt_ge) > kf)
        def _():
            nbits = max(1, (seq - 1).bit_length())
            need = kf - count(lambda kb, s: kb > thr)

            def idx_body(j, m):
                cand = m | lax.shift_left(jnp.int32(1), nbits - 1 - j)
                cnt = count(lambda kb, s: (kb == thr) & (s < cand))
                return jnp.where(cnt < need, cand, m)

            sel_scr[1:2, :] = lax.fori_loop(0, nbits, idx_body, jnp.zeros((1, tq), jnp.int32))

    if tq <= topk:
        @pl.when(i == 0)
        def _():
            sel_scr[0:1, :] = jnp.full((1, tq), INT_MIN, jnp.int32)
            sel_scr[1:2, :] = jnp.full((1, tq), -1, jnp.int32)

        pl.when(i > 0)(select_topk)
    else:
        select_topk()

    thr = sel_scr[0:1, :]
    m_idx = sel_scr[1:2, :]

    def bias_body(c, carry):
        off = c * tq
        kb = keys_scr[c]
        s_idx = off + row
        take = jnp.where(kb > thr, 1.0, jnp.where((kb == thr) & (s_idx <= m_idx), 1.0, 0.0))
        take = jnp.where(s_idx <= t0 + col, take, 0.0)
        bias_scr[c] = jnp.where(take > 0.5, 0.0, NEG_BIG)
        return carry

    lax.fori_loop(0, n_chunks, bias_body, 0)

    @pl.when(i == 0)
    def _():
        for c in range(seq // tq):
            for g in range(N_KV_HEADS):
                blk = v_ref[c * tq:(c + 1) * tq, g * HEAD_DIM:(g + 1) * HEAD_DIM].astype(F32)
                vt_scr[c, g * HEAD_DIM:(g + 1) * HEAD_DIM, :] = blk.T.astype(BF16)

    for h in range(N_HEADS):
        g, r = divmod(h, rep)
        qa_scr[g, r * tq:(r + 1) * tq, :HEAD_DIM] = q_ref[:, h * HEAD_DIM:(h + 1) * HEAD_DIM]
        qa_scr[g, r * tq:(r + 1) * tq, HEAD_DIM:] = jnp.broadcast_to(
            slope_ref[h:h + 1, :], (tq, V7X_LANES)).astype(BF16)
    acc_scr[...] = jnp.zeros(acc_scr.shape, F32)
    for g in range(N_KV_HEADS):
        ml_scr[2 * g:2 * g + 1, :] = jnp.full((1, rep * tq), NEG_BIG, F32)
        ml_scr[2 * g + 1:2 * g + 2, :] = jnp.zeros((1, rep * tq), F32)

    def attn_body(c, carry):
        off = pl.multiple_of(c * tq, tq)
        pos_c = pos_ref[pl.ds(off, tq), :]
        bias = jnp.concatenate([bias_scr[c]] * rep, axis=1)
        def scores(g):
            kc = jnp.concatenate([k_ref[pl.ds(off, tq), g * HEAD_DIM:(g + 1) * HEAD_DIM], pos_c],
                                 axis=1)
            return lax.dot_general(kc, qa_scr[g], nt, preferred_element_type=F32) + bias

        lead = N_KV_HEADS - 1
        sts = [scores(g) for g in range(lead)]
        for g in range(N_KV_HEADS):
            st = sts[g]
            if g + lead < N_KV_HEADS:
                sts.append(scores(g + lead))
            vt = vt_scr[c, g * HEAD_DIM:(g + 1) * HEAD_DIM, :]
            m_prev = ml_scr[2 * g:2 * g + 1, :]
            m_new = jnp.maximum(m_prev, _reduce_rows(st, jnp.maximum))
            alpha = jnp.exp2(m_prev - m_new)
            pt = jnp.exp2(st - m_new)
            ml_scr[2 * g:2 * g + 1, :] = m_new
            ml_scr[2 * g + 1:2 * g + 2, :] = (alpha * ml_scr[2 * g + 1:2 * g + 2, :]
                                              + _reduce_rows(pt, jnp.add))
            acc_scr[g] = alpha * acc_scr[g] + jnp.dot(vt, pt.astype(BF16), preferred_element_type=F32)
        return carry

    lax.fori_loop(0, n_chunks, attn_body, 0)
    for g in range(N_KV_HEADS):
        o_t = acc_scr[g] / ml_scr[2 * g + 1:2 * g + 2, :]
        for r in range(rep):
            h = g * rep + r
            y_ref[:, h * HEAD_DIM:(h + 1) * HEAD_DIM] = o_t[:, r * tq:(r + 1) * tq].T.astype(y_ref.dtype)


def _attention(q, qi, kw, k, v, batch, seq, cast_weights, l):
    m = q.shape[0]
    topk = min(TOPK_MAX, seq // 4)
    tq = _pick(seq, (256, 128))
    nq = seq // tq
    kww = kw.shape[1]
    rep = N_HEADS // N_KV_HEADS
    assert HEAD_DIM == V7X_LANES and tq % V7X_LANES == 0
    assert kww == V7X_LANES == 2 * IDX_DIM and IDX_DIM + IDX_HEADS <= V7X_LANES
    n_steps = batch * nq
    assert len(cast_weights) == N_CAST and all(w.shape[1] % (n_steps * BF16_SUBLANES) == 0 for w in cast_weights)
    slabs = [w.shape[1] // n_steps for w in cast_weights]
    cast_bytes = sum(2 * s * w.shape[2] * 6 for s, w in zip(slabs, cast_weights))
    outs = pl.pallas_call(
        functools.partial(_attn_kernel, tq=tq, seq=seq, topk=topk),
        grid=(batch, nq),
        in_specs=[pl.BlockSpec((tq, ATT_WIDTH), lambda b, i: (b * nq + i, 0)),
                  pl.BlockSpec((tq, IDX_HEADS * IDX_DIM), lambda b, i: (b * nq + i, 0)),
                  pl.BlockSpec((tq, kww), lambda b, i: (b * nq + i, 0)),
                  pl.BlockSpec((seq, KV_WIDTH), lambda b, i: (b, 0)),
                  pl.BlockSpec((seq, KV_WIDTH), lambda b, i: (b, 0)),
                  pl.BlockSpec((seq, kww), lambda b, i: (b, 0)),
                  pl.BlockSpec((N_HEADS, V7X_LANES), lambda b, i: (0, 0)),
                  pl.BlockSpec((seq, V7X_LANES), lambda b, i: (0, 0))]
        + [pl.BlockSpec((None, s, w.shape[2]), lambda b, i: (l, b * nq + i, 0))
           for s, w in zip(slabs, cast_weights)],
        out_specs=[pl.BlockSpec((tq, ATT_WIDTH), lambda b, i: (b * nq + i, 0))]
        + [pl.BlockSpec((s, w.shape[2]), lambda b, i: (b * nq + i, 0)) for s, w in zip(slabs, cast_weights)],
        out_shape=[jax.ShapeDtypeStruct((m, ATT_WIDTH), BF16)]
        + [jax.ShapeDtypeStruct(w.shape[1:], BF16) for w in cast_weights],
        scratch_shapes=[pltpu.VMEM((nq, tq, tq), jnp.int32),
                        pltpu.VMEM((nq, tq, tq), F32),
                        pltpu.VMEM((V7X_LANES, tq), F32),
                        pltpu.VMEM((8, tq), jnp.int32),
                        pltpu.VMEM((nq, KV_WIDTH, tq), BF16),
                        pltpu.VMEM((N_KV_HEADS, rep * tq, 2 * HEAD_DIM), BF16),
                        pltpu.VMEM((2 * N_KV_HEADS, rep * tq), F32),
                        pltpu.VMEM((N_KV_HEADS, HEAD_DIM, rep * tq), F32)],
        compiler_params=_params(("arbitrary", "arbitrary"),
                                6 * tq * ATT_WIDTH * 2, 4 * seq * KV_WIDTH * 2, 2 * (seq + tq) * kww * 4,
                                2 * seq * tq * 4, 8 * tq * tq * 4, seq * KV_WIDTH * 2, rep * HEAD_DIM * tq * 4,
                                cast_bytes // 2),
        name="dsa_attention",
    )(q, qi, kw, k, v, kw, *_alibi_tables(seq), *cast_weights)
    return outs[0], outs[1:]


def _sgu_kernel(u_ref, v_ref, g_ref, w_ref, bt_ref, o_ref, *, rows):
    v = v_ref[...].astype(F32)
    vn = (v * lax.rsqrt(jnp.mean(v * v, axis=-1, keepdims=True) + EPS) * g_ref[...]).astype(BF16)
    r_i = lax.broadcasted_iota(jnp.int32, (GM_CHUNK, GM_CHUNK), 0)
    c_i = lax.broadcasted_iota(jnp.int32, (GM_CHUNK, GM_CHUNK), 1)
    for g in range(GM_GROUPS):
        w = jnp.where(r_i >= c_i, w_ref[g], 0.0).astype(BF16)
        bcol = bt_ref[:, g:g + 1]
        cs = slice(g * GM_GROUP_W, (g + 1) * GM_GROUP_W)
        for n in range(rows // GM_CHUNK):
            rs = slice(n * GM_CHUNK, (n + 1) * GM_CHUNK)
            f = jnp.dot(w, vn[rs, cs], preferred_element_type=F32) + bcol
            o_ref[rs, cs] = (u_ref[rs, cs].astype(F32) * f).astype(o_ref.dtype)


def _sgu(uv, gain, w_s, b_s, seq):
    m = uv.shape[0]
    rows = _pick(seq, (512, 256, 128))
    return pl.pallas_call(
        functools.partial(_sgu_kernel, rows=rows),
        grid=(m // rows,),
        in_specs=[pl.BlockSpec((rows, GM_WIDTH), lambda i: (i, 0)),
                  pl.BlockSpec((rows, GM_WIDTH), lambda i: (i, 1)),
                  pl.BlockSpec((1, GM_WIDTH), lambda i: (0, 0)),
                  pl.BlockSpec((GM_GROUPS, GM_CHUNK, GM_CHUNK), lambda i: (0, 0, 0)),
                  pl.BlockSpec((GM_CHUNK, GM_GROUPS), lambda i: (0, 0))],
        out_specs=pl.BlockSpec((rows, GM_WIDTH), lambda i: (i, 0)),
        out_shape=jax.ShapeDtypeStruct((m, GM_WIDTH), BF16),
        compiler_params=_params(("arbitrary",), 6 * rows * GM_WIDTH * 2, 2 * rows * GM_WIDTH * 4),
        name="sgu",
    )(uv, uv, gain.reshape(1, GM_WIDTH), w_s, b_s.T)


def _w_spec(w, l, k, tn, j0=0):
    if w.ndim == 3:
        return pl.BlockSpec((None, k, tn), lambda i, j: (l, 0, j + j0))
    return pl.BlockSpec((k, tn), lambda i, j: (0, j + j0))


def _merge_kernel(ya_ref, yb_ref, wa_ref, wb_ref, ga_ref, gb_ref, o_ref):
    wa = wa_ref[...].astype(BF16)
    wb = wb_ref[...].astype(BF16)
    tm = o_ref.shape[0]
    rsub = min(tm, MXU_ACC_ROWS)
    for rb in range(tm // rsub):
        rs = slice(rb * rsub, (rb + 1) * rsub)
        pa = jnp.dot(ya_ref[rs, :], wa, preferred_element_type=F32)
        pb = jnp.dot(yb_ref[rs, :], wb, preferred_element_type=F32)
        ga = jax.nn.sigmoid(ga_ref[rs, :].astype(F32))
        gb = jax.nn.sigmoid(gb_ref[rs, :].astype(F32))
        o_ref[rs, :] = (ga * pa + gb * pb).astype(o_ref.dtype)


def _merge(ya, yb, wa, wb, l, gates):
    m, ka = ya.shape
    kb = yb.shape[1]
    d = wa.shape[-1]
    tm = _pick(m, (1024, 512, 256, 128))
    tn = _pick(d, (512, 256, 128))
    nj = d // tn
    wsz = wa.dtype.itemsize
    return pl.pallas_call(
        _merge_kernel,
        grid=(m // tm, nj),
        in_specs=[pl.BlockSpec((tm, ka), lambda i, j: (i, 0)),
                  pl.BlockSpec((tm, kb), lambda i, j: (i, 0)),
                  _w_spec(wa, l, ka, tn),
                  _w_spec(wb, l, kb, tn),
                  pl.BlockSpec((tm, tn), lambda i, j: (i, j)),
                  pl.BlockSpec((tm, tn), lambda i, j: (i, j + nj))],
        out_specs=pl.BlockSpec((tm, tn), lambda i, j: (i, j)),
        out_shape=jax.ShapeDtypeStruct((m, d), BF16),
        compiler_params=pltpu.CompilerParams(
            dimension_semantics=("arbitrary", "arbitrary"),
            vmem_limit_bytes=int(min(VMEM_CAP, 2 * tm * (ka + kb) * 2 + (ka + kb) * tn * (2 * wsz + 2)
                                     + 6 * tm * tn * 2 + 3 * tm * tn * 4 + (4 << 20)))),
        name="merge",
    )(ya, yb, wa, wb, gates, gates)


def _resid_kernel(a_ref, w_ref, x_ref, g_ref, o_ref):
    acc = jnp.dot(a_ref[...], w_ref[...].astype(BF16), preferred_element_type=F32)
    o_ref[...] = x_ref[...] + g_ref[...] * acc


def _resid(a, w, l, x2, mod3, g_idx, seq, *, tn_prefs, k_part=(0, 1), name):
    m = a.shape[0]
    kp_idx, kp_n = k_part
    k = a.shape[1] // kp_n
    d = w.shape[-1]
    assert a.shape[1] % kp_n == 0 and k % V7X_LANES == 0 and (kp_n == 1 or w.ndim == 2)
    tm = _pick(seq, (1024, 512, 256, 128))
    per_b = seq // tm
    tn = _pick(d, tn_prefs)
    nj = d // tn
    wsz = w.dtype.itemsize
    w_spec = _w_spec(w, l, k, tn) if w.ndim == 3 else pl.BlockSpec((k, tn), lambda i, j: (kp_idx, j))
    return pl.pallas_call(
        _resid_kernel,
        grid=(m // tm, nj),
        in_specs=[pl.BlockSpec((tm, k), lambda i, j: (i, kp_idx)),
                  w_spec,
                  pl.BlockSpec((tm, tn), lambda i, j: (i, j)),
                  pl.BlockSpec((None, 1, tn), lambda i, j: (i // per_b, 0, g_idx * nj + j))],
        out_specs=pl.BlockSpec((tm, tn), lambda i, j: (i, j)),
        out_shape=jax.ShapeDtypeStruct((m, d), F32),
        compiler_params=pltpu.CompilerParams(
            dimension_semantics=("arbitrary", "arbitrary"),
            vmem_limit_bytes=int(min(VMEM_CAP, 2 * tm * k * 2 + k * tn * (2 * wsz + 2)
                                     + 6 * tm * tn * 4 + (4 << 20)))),
        name=name,
    )(a, w, x2, mod3)


HALO = 8


def _up_kernel(h_ref, wg_ref, wv_ref, cwg_ref, cwv_ref, cbg_ref, cbv_ref, wd_ref, o_ref, wd_o_ref, *, sub):
    wd_o_ref[...] = wd_ref[...].astype(BF16)
    wg = wg_ref[...].astype(BF16)
    wv = wv_ref[...].astype(BF16)
    tn = wg.shape[1]

    def conv(a, halo, cw_ref, cb_ref):
        ext = jnp.concatenate([halo, a], axis=0)
        acc = cb_ref[...] + pltpu.roll(ext, 2, 0)[HALO:] * cw_ref[0:1, :]
        acc = acc + pltpu.roll(ext, 1, 0)[HALO:] * cw_ref[1:2, :]
        return acc + a * cw_ref[2:3, :]

    halo_g = halo_v = jnp.zeros((HALO, tn), F32)
    for s in range(h_ref.shape[0] // sub):
        rs = slice(s * sub, (s + 1) * sub)
        hs = h_ref[rs, :]
        ag = jnp.dot(hs, wg, preferred_element_type=F32)
        av = jnp.dot(hs, wv, preferred_element_type=F32)
        gate = conv(ag, halo_g, cwg_ref, cbg_ref)
        val = conv(av, halo_v, cwv_ref, cbv_ref)
        o_ref[rs, :] = (gate * jax.nn.sigmoid(gate) * val).astype(o_ref.dtype)
        halo_g, halo_v = ag[sub - HALO:], av[sub - HALO:]


def _up_conv_gate(h2, w_up, l, conv_w, conv_b, w_down, seq):
    m, k = h2.shape
    f = w_up.shape[-1] // 2
    d_out = w_down.shape[-1]
    tn = _pick(f, (256, 128))
    nj = f // tn
    n_steps = (m // seq) * nj
    assert f % (n_steps * BF16_SUBLANES) == 0
    slab = f // n_steps
    sub = _pick(seq, (512, 256, 128))
    wsz = w_up.dtype.itemsize
    return pl.pallas_call(
        functools.partial(_up_kernel, sub=sub),
        grid=(m // seq, nj),
        in_specs=[pl.BlockSpec((seq, k), lambda i, j: (i, 0), pipeline_mode=pl.Buffered(1)),
                  _w_spec(w_up, l, k, tn),
                  _w_spec(w_up, l, k, tn, nj),
                  pl.BlockSpec((None, CONV_W, tn), lambda i, j: (l, 0, j)),
                  pl.BlockSpec((None, CONV_W, tn), lambda i, j: (l, 0, j + nj)),
                  pl.BlockSpec((None, 1, tn), lambda i, j: (l, 0, j)),
                  pl.BlockSpec((None, 1, tn), lambda i, j: (l, 0, j + nj)),
                  pl.BlockSpec((None, slab, d_out), lambda i, j: (l, i * nj + j, 0))],
        out_specs=[pl.BlockSpec((seq, tn), lambda i, j: (i, j)),
                   pl.BlockSpec((slab, d_out), lambda i, j: (i * nj + j, 0))],
        out_shape=[jax.ShapeDtypeStruct((m, f), BF16),
                   jax.ShapeDtypeStruct((f, d_out), BF16)],
        compiler_params=pltpu.CompilerParams(
            dimension_semantics=("arbitrary", "arbitrary"),
            vmem_limit_bytes=int(min(VMEM_CAP, seq * k * 2 + 2 * k * tn * (2 * wsz + 2) + 2 * seq * tn * 2
                                     + 12 * sub * tn * 4 + 2 * slab * d_out * 6 + (4 << 20)))),
        name="up_conv_gate",
    )(h2, w_up, w_up, conv_w, conv_w, conv_b.reshape(conv_b.shape[0], 1, 2 * f),
      conv_b.reshape(conv_b.shape[0], 1, 2 * f), w_down)


def kernel(x, c, ada_w, ada_b, norm1_g, w_in, q_norm_g, k_norm_g, sgu_norm_g, sgu_w, sgu_b,
           w_branch_a, w_branch_b, w_out, norm2_g, w_up, conv_w, conv_b, w_down):
    batch, seq, d = x.shape
    m = batch * seq
    depth = ada_w.shape[0]
    assert seq % GM_CHUNK == 0 and d % V7X_LANES == 0

    x2 = x.reshape(m, d)
    bp = -(-batch // 8) * 8
    c_pad = jnp.pad(c, ((0, bp - batch), (0, 0)))

    o_q = 0
    o_k = o_q + ATT_WIDTH
    o_v = o_k + KV_WIDTH
    o_qi = o_v + KV_WIDTH
    o_ki = o_qi + IDX_HEADS * IDX_DIM
    o_wi = o_ki + IDX_DIM
    o_gu = o_wi + IDX_HEADS
    o_ga = o_gu + 2 * GM_WIDTH
    o_end = o_ga + 2 * d

    assert o_gu - o_ki <= V7X_LANES and o_end == w_in.shape[2]
    w_in_t = jnp.swapaxes(w_in, 1, 2)

    for l in range(depth):
        mod = _ada(c_pad, ada_w, ada_b, l)
        mod3 = mod.reshape(bp, 1, 6 * d)

        h = _norm_mod(x2, norm1_g[l], mod3, 1, 0, seq)
        q = _proj(h, w_in_t, l, o_q, ATT_WIDTH, out_dtype=BF16, epilogue="headnorm", gain=q_norm_g[l],
                  post_scale=HEAD_DIM ** -0.5 * LOG2E, name="proj_q")
        k = _proj(h, w_in_t, l, o_k, KV_WIDTH, out_dtype=BF16, epilogue="headnorm", gain=k_norm_g[l],
                  name="proj_k")
        v = _proj(h, w_in_t, l, o_v, KV_WIDTH, out_dtype=BF16, name="proj_v")
        qi = _proj(h, w_in_t, l, o_qi, IDX_HEADS * IDX_DIM, out_dtype=BF16, name="proj_qi")
        kw = _proj(h, w_in_t, l, o_ki, V7X_LANES, out_dtype=F32, name="proj_kw")
        uv = _proj(h, w_in_t, l, o_gu, 2 * GM_WIDTH, out_dtype=BF16, epilogue="gelu", name="proj_uv")
        gates = _proj(h, w_in_t, l, o_ga, 2 * d, out_dtype=BF16, name="proj_gates")

        y_a, (w_out_bf, w_a_bf, w_b_bf) = _attention(q, qi, kw, k, v, batch, seq,
                                                     (w_out, w_branch_a, w_branch_b), l)
        y_b = _sgu(uv, sgu_norm_g[l], sgu_w[l], sgu_b[l], seq)
        merged = _merge(y_a, y_b, w_a_bf, w_b_bf, l, gates)
        x2 = _resid(merged, w_out_bf, l, x2, mod3, 2, seq,
                    tn_prefs=(512, 256, 128), name="out_proj_resid")

        h2 = _norm_mod(x2, norm2_g[l], mod3, 4, 3, seq)
        act, w_down_bf = _up_conv_gate(h2, w_up, l, conv_w, conv_b, w_down, seq)
        k_parts = 2 if act.shape[1] % (2 * V7X_LANES) == 0 else 1
        for kp in range(k_parts):
            x2 = _resid(act, w_down_bf, l, x2, mod3, 5, seq,
                        tn_prefs=(512, 256, 128), k_part=(kp, k_parts), name="down_proj_resid")

    return x2.reshape(batch, seq, d)
```

```python
import functools

import numpy as np
import jax
import jax.numpy as jnp
from jax import lax
from jax.experimental import pallas as pl
from jax.experimental.pallas import tpu as pltpu

N_HEADS = 16
HEAD_DIM = 128
N_KV_HEADS = 4
ATT_WIDTH = N_HEADS * HEAD_DIM
KV_WIDTH = N_KV_HEADS * HEAD_DIM
IDX_HEADS = 32
IDX_DIM = 64
TOPK_MAX = 256
GM_WIDTH = 2048
GM_GROUPS = 8
GM_GROUP_W = GM_WIDTH // GM_GROUPS
GM_CHUNK = 128
CONV_W = 3
EPS = 1e-6
NEG_BIG = -1e30

V7X_LANES = 128
V7X_VMEM_BYTES = 64 * 1024 * 1024
VMEM_CAP = V7X_VMEM_BYTES - 8 * 1024 * 1024

BF16 = jnp.bfloat16
F32 = jnp.float32
INT_MIN = -(2 ** 31)


def _vmem_limit(*nbytes):
    return int(min(VMEM_CAP, 2 * sum(nbytes) + (4 << 20)))


def _params(semantics, *nbytes):
    return pltpu.CompilerParams(dimension_semantics=semantics, vmem_limit_bytes=_vmem_limit(*nbytes))


def _pick(n, prefs):
    for p in prefs:
        if n % p == 0:
            return p
    return n


def _ada_kernel(c_ref, w_ref, b_ref, o_ref):
    c = c_ref[...]
    cs = c * jax.nn.sigmoid(c)
    o_ref[...] = jnp.dot(cs.astype(BF16), w_ref[...].astype(BF16),
                         preferred_element_type=F32) + b_ref[...]


def _ada(c_pad, ada_w, ada_b, l):
    bp, d = c_pad.shape
    n = ada_w.shape[2]
    tn = _pick(n, (512, 256, 128))
    return pl.pallas_call(
        _ada_kernel,
        grid=(n // tn,),
        in_specs=[pl.BlockSpec((bp, d), lambda j: (0, 0)),
                  pl.BlockSpec((None, d, tn), lambda j: (l, 0, j)),
                  pl.BlockSpec((None, 1, tn), lambda j: (l, 0, j))],
        out_specs=pl.BlockSpec((bp, tn), lambda j: (0, j)),
        out_shape=jax.ShapeDtypeStruct((bp, n), F32),
        compiler_params=_params(("arbitrary",), 2 * d * tn * 4, d * tn * 2),
        name="ada_mod",
    )(c_pad, ada_w, ada_b.reshape(ada_b.shape[0], 1, n))


def _norm_mod_kernel(x_ref, g_ref, sc_ref, sh_ref, o_ref):
    x = x_ref[...]
    y = x * lax.rsqrt(jnp.mean(x * x, axis=-1, keepdims=True) + EPS) * g_ref[...]
    o_ref[...] = (y * (1.0 + sc_ref[...]) + sh_ref[...]).astype(o_ref.dtype)


def _norm_mod(x2, g, mod3, sc_idx, sh_idx, seq):
    m, d = x2.shape
    tm = _pick(seq, (512, 256, 128, 64, 8))
    per_b = seq // tm
    return pl.pallas_call(
        _norm_mod_kernel,
        grid=(m // tm,),
        in_specs=[pl.BlockSpec((tm, d), lambda i: (i, 0)),
                  pl.BlockSpec((1, d), lambda i: (0, 0)),
                  pl.BlockSpec((None, 1, d), lambda i: (i // per_b, 0, sc_idx)),
                  pl.BlockSpec((None, 1, d), lambda i: (i // per_b, 0, sh_idx))],
        out_specs=pl.BlockSpec((tm, d), lambda i: (i, 0)),
        out_shape=jax.ShapeDtypeStruct((m, d), BF16),
        compiler_params=_params(("arbitrary",), 2 * tm * d * 4, 2 * tm * d * 2),
        name="norm_mod",
    )(x2, g.reshape(1, d), mod3, mod3)


def _gelu_exact(x):
    return 0.5 * x * (1.0 + lax.erf(x * (2.0 ** -0.5)))


MXU_ACC_ROWS = 512


_NT = (((1,), (1,)), ((), ()))


def _proj_kernel(a_ref, w_ref, *rest, epilogue, post_scale):
    o_ref = rest[-1]
    tm, tn = o_ref.shape
    rsub = min(tm, MXU_ACC_ROWS)
    w = w_ref[...].astype(BF16)
    for rb in range(tm // rsub):
        rs = slice(rb * rsub, (rb + 1) * rsub)
        acc = lax.dot_general(a_ref[rs, :], w, _NT, preferred_element_type=F32)
        if epilogue == "plain":
            out = acc
        elif epilogue == "gelu":
            out = _gelu_exact(acc)
        elif epilogue == "sigmoid":
            out = jax.nn.sigmoid(acc)
        elif epilogue == "headnorm":
            gain = rest[0][...]
            parts = []
            for c in range(tn // HEAD_DIM):
                blk = acc[:, c * HEAD_DIM:(c + 1) * HEAD_DIM]
                ms = jnp.mean(blk * blk, axis=-1, keepdims=True)
                parts.append(blk * lax.rsqrt(ms + EPS) * (gain * post_scale))
            out = jnp.concatenate(parts, axis=1) if len(parts) > 1 else parts[0]
        else:
            raise ValueError(epilogue)
        o_ref[rs, :] = out.astype(o_ref.dtype)


def _proj(a, w_t, l, row0, n, *, out_dtype, epilogue="plain", gain=None, post_scale=1.0, name):
    m, k = a.shape
    assert row0 % 8 == 0 and w_t.shape[2] == k
    tm = _pick(m, (1024, 512, 256, 128))
    tn = _pick(n, (512, 256, 128))
    in_specs = [pl.BlockSpec((tm, k), lambda i, j: (i, 0)),
                pl.BlockSpec((None, pl.Element(tn), pl.Element(k)),
                             lambda i, j: (l, pl.multiple_of(row0 + j * tn, 8), 0))]
    args = [a, w_t]
    if epilogue == "headnorm":
        in_specs.append(pl.BlockSpec((1, HEAD_DIM), lambda i, j: (0, 0)))
        args.append(gain.reshape(1, HEAD_DIM))
    osz = jnp.dtype(out_dtype).itemsize
    return pl.pallas_call(
        functools.partial(_proj_kernel, epilogue=epilogue, post_scale=post_scale),
        grid=(m // tm, n // tn),
        in_specs=in_specs,
        out_specs=pl.BlockSpec((tm, tn), lambda i, j: (i, j)),
        out_shape=jax.ShapeDtypeStruct((m, n), out_dtype),
        compiler_params=pltpu.CompilerParams(
            dimension_semantics=("arbitrary", "arbitrary"),
            vmem_limit_bytes=int(min(VMEM_CAP, 2 * tm * k * 2 + tn * k * (2 * 4 + 2) + 2 * tm * tn * osz
                                     + 3 * tm * tn * 4 + (4 << 20)))),
        name=name,
    )(*args)


def _alibi_slopes(n):
    return [2.0 ** (-8.0 * (i + 1) / n) for i in range(n)]


F32_SUBLANES = 8
BF16_SUBLANES = 16


def _fold_rows(x, op):
    rows = x.shape[0]
    while rows > F32_SUBLANES and rows % (2 * F32_SUBLANES) == 0:
        rows //= 2
        x = op(x[:rows], x[rows:])
    return x


def _reduce_rows(x, op):
    return (jnp.max if op is jnp.maximum else jnp.sum)(_fold_rows(x, op), axis=0, keepdims=True)


KEY_NEG_INF = INT_MIN + 0x7FFFFF


def _key_to_float(key):
    bits = key ^ ((key >> 31) & jnp.int32(0x7FFFFFFF))
    return lax.bitcast_convert_type(bits, F32)


LOG2E = 1.4426950408889634
ALIBI_PIECES = 4
POS_SPLIT = 256


def _alibi_tables(seq):
    assert seq <= POS_SPLIT * POS_SPLIT and 2 * ALIBI_PIECES <= V7X_LANES
    slope_tab = np.zeros((N_HEADS, V7X_LANES), np.float32)
    for h, s in enumerate(_alibi_slopes(N_HEADS)):
        rest = float(np.float32(s)) * LOG2E
        for p in range(ALIBI_PIECES):
            piece = float(np.asarray(rest, dtype=BF16))
            slope_tab[h, 2 * p:2 * p + 2] = piece
            rest -= piece
    pos = np.arange(seq)
    pos_tab = np.zeros((seq, V7X_LANES), np.float32)
    pos_tab[:, 0:2 * ALIBI_PIECES:2] = ((pos // POS_SPLIT) * POS_SPLIT)[:, None]
    pos_tab[:, 1:2 * ALIBI_PIECES:2] = (pos % POS_SPLIT)[:, None]
    return jnp.asarray(slope_tab), jnp.asarray(pos_tab, dtype=BF16)


N_CAST = 3


def _attn_kernel(q_ref, qi_ref, kwq_ref, k_ref, v_ref, kwa_ref, slope_ref, pos_ref, *rest, tq, seq, topk):
    cast_in, (y_ref, *cast_out) = rest[:N_CAST], rest[N_CAST:2 * N_CAST + 1]
    score_scr, bias_scr, wt_scr, sel_scr, vt_scr, qa_scr, ml_scr, acc_scr = rest[2 * N_CAST + 1:]
    for src, dst in zip(cast_in, cast_out):
        dst[...] = src[...].astype(BF16)
    i = pl.program_id(1)
    n_chunks = i + 1
    t0 = i * tq
    rep = N_HEADS // N_KV_HEADS
    nt = (((1,), (1,)), ((), ()))
    row = lax.broadcasted_iota(jnp.int32, (tq, tq), 0)
    col = lax.broadcasted_iota(jnp.int32, (tq, tq), 1)
    kf = float(topk)

    w_fold = (IDX_HEADS ** -0.5) * (IDX_DIM ** -0.5)
    wt_scr[...] = (kwq_ref[...] * w_fold).T
    lane = lax.broadcasted_iota(jnp.int32, (tq, V7X_LANES), 1)

    def score_body(c, carry):
        off = pl.multiple_of(c * tq, tq)
        kraw = kwa_ref[pl.ds(off, tq), :]
        k_even = jnp.where(lane < IDX_DIM, kraw, 0.0)
        k_odd = jnp.where(lane >= IDX_DIM, pltpu.roll(kraw, IDX_DIM, 1), 0.0)
        lhs = jnp.concatenate([k_even, k_odd], axis=0).astype(BF16)
        acc = jnp.zeros((tq, tq), F32)
        for p in range(IDX_HEADS // 2):
            lg = lax.dot_general(lhs, qi_ref[:, p * V7X_LANES:(p + 1) * V7X_LANES], nt,
                                 preferred_element_type=F32)
            h0 = IDX_DIM + 2 * p
            acc = acc + wt_scr[h0:h0 + 1, :] * jnp.maximum(lg[:tq], 0.0)
            acc = acc + wt_scr[h0 + 1:h0 + 2, :] * jnp.maximum(lg[tq:], 0.0)
        score_scr[c] = jnp.where(off + row <= t0 + col, acc, -jnp.inf)
        return carry

    lax.fori_loop(0, n_chunks, score_body, 0)

    def count(pred):
        def body(c, part):
            off = c * tq
            return part + _fold_rows(jnp.where(pred(score_scr[c], off + row), 1.0, 0.0), jnp.add)
        part = lax.fori_loop(0, n_chunks, body, jnp.zeros((F32_SUBLANES, tq), F32))
        return jnp.sum(part, axis=0, keepdims=True)

    def select_topk():
        cnt_nonneg = count(lambda sc, s: sc >= 0.0)
        nonneg = cnt_nonneg >= kf
        prefix0 = jnp.where(nonneg, jnp.int32(0), jnp.int32(INT_MIN))
        cnt0 = jnp.where(nonneg, cnt_nonneg, (n_chunks * tq).astype(F32))

        def bit_body(j, carry):
            prefix, cnt_prefix = carry
            cand = prefix | lax.shift_left(jnp.int32(1), 30 - j)
            cand_f = _key_to_float(cand)
            cnt = count(lambda sc, s: sc >= cand_f)
            keep = cnt >= kf
            return jnp.where(keep, cand, prefix), jnp.where(keep, cnt, cnt_prefix)

        thr_key, cnt_ge = lax.fori_loop(0, 31, bit_body, (prefix0, cnt0))
        thr_key = jnp.maximum(thr_key, jnp.int32(KEY_NEG_INF))
        thr = _key_to_float(thr_key)
        sel_scr[0:1, :] = thr_key
        sel_scr[1:2, :] = jnp.full((1, tq), seq, jnp.int32)

        @pl.when(jnp.max(cnt_ge) > kf)
        def _():
            nbits = max(1, (seq - 1).bit_length())
            need = kf - count(lambda sc, s: sc > thr)

            def idx_body(j, m):
                cand = m | lax.shift_left(jnp.int32(1), nbits - 1 - j)
                cnt = count(lambda sc, s: (sc == thr) & (s < cand))
                return jnp.where(cnt < need, cand, m)

            sel_scr[1:2, :] = lax.fori_loop(0, nbits, idx_body, jnp.zeros((1, tq), jnp.int32))

    if tq <= topk:
        @pl.when(i == 0)
        def _():
            sel_scr[0:1, :] = jnp.full((1, tq), KEY_NEG_INF, jnp.int32)
            sel_scr[1:2, :] = jnp.full((1, tq), -1, jnp.int32)

        pl.when(i > 0)(select_topk)
    else:
        select_topk()

    thr = _key_to_float(sel_scr[0:1, :])
    m_idx = sel_scr[1:2, :]

    def bias_body(c, carry):
        off = c * tq
        kb = score_scr[c]
        s_idx = off + row
        take = jnp.where(kb > thr, 1.0, jnp.where((kb == thr) & (s_idx <= m_idx), 1.0, 0.0))
        take = jnp.where(s_idx <= t0 + col, take, 0.0)
        bias_scr[c] = jnp.where(take > 0.5, 0.0, NEG_BIG)
        return carry

    lax.fori_loop(0, n_chunks, bias_body, 0)

    @pl.when(i == 0)
    def _():
        for c in range(seq // tq):
            for g in range(N_KV_HEADS):
                blk = v_ref[c * tq:(c + 1) * tq, g * HEAD_DIM:(g + 1) * HEAD_DIM].astype(F32)
                vt_scr[c, g * HEAD_DIM:(g + 1) * HEAD_DIM, :] = blk.T.astype(BF16)

    for h in range(N_HEADS):
        g, r = divmod(h, rep)
        qa_scr[g, r * tq:(r + 1) * tq, :HEAD_DIM] = q_ref[:, h * HEAD_DIM:(h + 1) * HEAD_DIM]
        qa_scr[g, r * tq:(r + 1) * tq, HEAD_DIM:] = jnp.broadcast_to(
            slope_ref[h:h + 1, :], (tq, V7X_LANES)).astype(BF16)
    acc_scr[...] = jnp.zeros(acc_scr.shape, F32)
    for g in range(N_KV_HEADS):
        ml_scr[2 * g:2 * g + 1, :] = jnp.full((1, rep * tq), NEG_BIG, F32)
        ml_scr[2 * g + 1:2 * g + 2, :] = jnp.zeros((1, rep * tq), F32)

    def attn_body(c, carry):
        off = pl.multiple_of(c * tq, tq)
        pos_c = pos_ref[pl.ds(off, tq), :]
        bias = jnp.concatenate([bias_scr[c]] * rep, axis=1)
        def scores(g):
            kc = jnp.concatenate([k_ref[pl.ds(off, tq), g * HEAD_DIM:(g + 1) * HEAD_DIM], pos_c],
                                 axis=1)
            return lax.dot_general(kc, qa_scr[g], nt, preferred_element_type=F32) + bias

        lead = N_KV_HEADS - 1
        sts = [scores(g) for g in range(lead)]
        for g in range(N_KV_HEADS):
            st = sts[g]
            if g + lead < N_KV_HEADS:
                sts.append(scores(g + lead))
            vt = vt_scr[c, g * HEAD_DIM:(g + 1) * HEAD_DIM, :]
            m_prev = ml_scr[2 * g:2 * g + 1, :]
            m_new = jnp.maximum(m_prev, _reduce_rows(st, jnp.maximum))
            alpha = jnp.exp2(m_prev - m_new)
            pt = jnp.exp2(st - m_new)
            ml_scr[2 * g:2 * g + 1, :] = m_new
            ml_scr[2 * g + 1:2 * g + 2, :] = (alpha * ml_scr[2 * g + 1:2 * g + 2, :]
                                              + _reduce_rows(pt, jnp.add))
            acc_scr[g] = alpha * acc_scr[g] + jnp.dot(vt, pt.astype(BF16), preferred_element_type=F32)
        return carry

    lax.fori_loop(0, n_chunks, attn_body, 0)
    for g in range(N_KV_HEADS):
        o_t = acc_scr[g] / ml_scr[2 * g + 1:2 * g + 2, :]
        for r in range(rep):
            h = g * rep + r
            y_ref[:, h * HEAD_DIM:(h + 1) * HEAD_DIM] = o_t[:, r * tq:(r + 1) * tq].T.astype(y_ref.dtype)


def _attention(q, qi, kw, k, v, batch, seq, cast_weights, l):
    m = q.shape[0]
    topk = min(TOPK_MAX, seq // 4)
    tq = _pick(seq, (256, 128))
    nq = seq // tq
    kww = kw.shape[1]
    rep = N_HEADS // N_KV_HEADS
    assert HEAD_DIM == V7X_LANES and tq % V7X_LANES == 0
    assert kww == V7X_LANES == 2 * IDX_DIM and IDX_DIM + IDX_HEADS <= V7X_LANES
    n_steps = batch * nq
    assert len(cast_weights) == N_CAST and all(w.shape[1] % (n_steps * BF16_SUBLANES) == 0 for w in cast_weights)
    slabs = [w.shape[1] // n_steps for w in cast_weights]
    cast_bytes = sum(2 * s * w.shape[2] * 6 for s, w in zip(slabs, cast_weights))
    outs = pl.pallas_call(
        functools.partial(_attn_kernel, tq=tq, seq=seq, topk=topk),
        grid=(batch, nq),
        in_specs=[pl.BlockSpec((tq, ATT_WIDTH), lambda b, i: (b * nq + i, 0)),
                  pl.BlockSpec((tq, IDX_HEADS * IDX_DIM), lambda b, i: (b * nq + i, 0)),
                  pl.BlockSpec((tq, kww), lambda b, i: (b * nq + i, 0)),
                  pl.BlockSpec((seq, KV_WIDTH), lambda b, i: (b, 0)),
                  pl.BlockSpec((seq, KV_WIDTH), lambda b, i: (b, 0)),
                  pl.BlockSpec((seq, kww), lambda b, i: (b, 0)),
                  pl.BlockSpec((N_HEADS, V7X_LANES), lambda b, i: (0, 0)),
                  pl.BlockSpec((seq, V7X_LANES), lambda b, i: (0, 0))]
        + [pl.BlockSpec((None, s, w.shape[2]), lambda b, i: (l, b * nq + i, 0))
           for s, w in zip(slabs, cast_weights)],
        out_specs=[pl.BlockSpec((tq, ATT_WIDTH), lambda b, i: (b * nq + i, 0))]
        + [pl.BlockSpec((s, w.shape[2]), lambda b, i: (b * nq + i, 0)) for s, w in zip(slabs, cast_weights)],
        out_shape=[jax.ShapeDtypeStruct((m, ATT_WIDTH), BF16)]
        + [jax.ShapeDtypeStruct(w.shape[1:], BF16) for w in cast_weights],
        scratch_shapes=[pltpu.VMEM((nq, tq, tq), F32),
                        pltpu.VMEM((nq, tq, tq), F32),
                        pltpu.VMEM((V7X_LANES, tq), F32),
                        pltpu.VMEM((8, tq), jnp.int32),
                        pltpu.VMEM((nq, KV_WIDTH, tq), BF16),
                        pltpu.VMEM((N_KV_HEADS, rep * tq, 2 * HEAD_DIM), BF16),
                        pltpu.VMEM((2 * N_KV_HEADS, rep * tq), F32),
                        pltpu.VMEM((N_KV_HEADS, HEAD_DIM, rep * tq), F32)],
        compiler_params=_params(("arbitrary", "arbitrary"),
                                6 * tq * ATT_WIDTH * 2, 4 * seq * KV_WIDTH * 2, 2 * (seq + tq) * kww * 4,
                                2 * seq * tq * 4, 8 * tq * tq * 4, seq * KV_WIDTH * 2, rep * HEAD_DIM * tq * 4,
                                cast_bytes // 2),
        name="dsa_attention",
    )(q, qi, kw, k, v, kw, *_alibi_tables(seq), *cast_weights)
    return outs[0], outs[1:]


def _sgu_kernel(u_ref, v_ref, g_ref, w_ref, bt_ref, o_ref, *, rows):
    v = v_ref[...].astype(F32)
    vn = (v * lax.rsqrt(jnp.mean(v * v, axis=-1, keepdims=True) + EPS) * g_ref[...]).astype(BF16)
    r_i = lax.broadcasted_iota(jnp.int32, (GM_CHUNK, GM_CHUNK), 0)
    c_i = lax.broadcasted_iota(jnp.int32, (GM_CHUNK, GM_CHUNK), 1)
    for g in range(GM_GROUPS):
        w = jnp.where(r_i >= c_i, w_ref[g], 0.0).astype(BF16)
        bcol = bt_ref[:, g:g + 1]
        cs = slice(g * GM_GROUP_W, (g + 1) * GM_GROUP_W)
        for n in range(rows // GM_CHUNK):
            rs = slice(n * GM_CHUNK, (n + 1) * GM_CHUNK)
            f = jnp.dot(w, vn[rs, cs], preferred_element_type=F32) + bcol
            o_ref[rs, cs] = (u_ref[rs, cs].astype(F32) * f).astype(o_ref.dtype)


def _sgu(uv, gain, w_s, b_s, seq):
    m = uv.shape[0]
    rows = _pick(seq, (512, 256, 128))
    return pl.pallas_call(
        functools.partial(_sgu_kernel, rows=rows),
        grid=(m // rows,),
        in_specs=[pl.BlockSpec((rows, GM_WIDTH), lambda i: (i, 0)),
                  pl.BlockSpec((rows, GM_WIDTH), lambda i: (i, 1)),
                  pl.BlockSpec((1, GM_WIDTH), lambda i: (0, 0)),
                  pl.BlockSpec((GM_GROUPS, GM_CHUNK, GM_CHUNK), lambda i: (0, 0, 0)),
                  pl.BlockSpec((GM_CHUNK, GM_GROUPS), lambda i: (0, 0))],
        out_specs=pl.BlockSpec((rows, GM_WIDTH), lambda i: (i, 0)),
        out_shape=jax.ShapeDtypeStruct((m, GM_WIDTH), BF16),
        compiler_params=_params(("arbitrary",), 6 * rows * GM_WIDTH * 2, 2 * rows * GM_WIDTH * 4),
        name="sgu",
    )(uv, uv, gain.reshape(1, GM_WIDTH), w_s, b_s.T)


def _w_spec(w, l, k, tn, j0=0):
    if w.ndim == 3:
        return pl.BlockSpec((None, k, tn), lambda i, j: (l, 0, j + j0))
    return pl.BlockSpec((k, tn), lambda i, j: (0, j + j0))


def _merge_kernel(ya_ref, yb_ref, wa_ref, wb_ref, ga_ref, gb_ref, o_ref):
    wa = wa_ref[...].astype(BF16)
    wb = wb_ref[...].astype(BF16)
    tm = o_ref.shape[0]
    rsub = min(tm, MXU_ACC_ROWS)
    for rb in range(tm // rsub):
        rs = slice(rb * rsub, (rb + 1) * rsub)
        pa = jnp.dot(ya_ref[rs, :], wa, preferred_element_type=F32)
        pb = jnp.dot(yb_ref[rs, :], wb, preferred_element_type=F32)
        ga = jax.nn.sigmoid(ga_ref[rs, :].astype(F32))
        gb = jax.nn.sigmoid(gb_ref[rs, :].astype(F32))
        o_ref[rs, :] = (ga * pa + gb * pb).astype(o_ref.dtype)


def _merge(ya, yb, wa, wb, l, gates):
    m, ka = ya.shape
    kb = yb.shape[1]
    d = wa.shape[-1]
    tm = _pick(m, (1024, 512, 256, 128))
    tn = _pick(d, (512, 256, 128))
    nj = d // tn
    wsz = wa.dtype.itemsize
    return pl.pallas_call(
        _merge_kernel,
        grid=(m // tm, nj),
        in_specs=[pl.BlockSpec((tm, ka), lambda i, j: (i, 0)),
                  pl.BlockSpec((tm, kb), lambda i, j: (i, 0)),
                  _w_spec(wa, l, ka, tn),
                  _w_spec(wb, l, kb, tn),
                  pl.BlockSpec((tm, tn), lambda i, j: (i, j)),
                  pl.BlockSpec((tm, tn), lambda i, j: (i, j + nj))],
        out_specs=pl.BlockSpec((tm, tn), lambda i, j: (i, j)),
        out_shape=jax.ShapeDtypeStruct((m, d), BF16),
        compiler_params=pltpu.CompilerParams(
            dimension_semantics=("arbitrary", "arbitrary"),
            vmem_limit_bytes=int(min(VMEM_CAP, 2 * tm * (ka + kb) * 2 + (ka + kb) * tn * (2 * wsz + 2)
                                     + 6 * tm * tn * 2 + 3 * tm * tn * 4 + (4 << 20)))),
        name="merge",
    )(ya, yb, wa, wb, gates, gates)


def _resid_kernel(a_ref, w_ref, x_ref, g_ref, o_ref):
    acc = jnp.dot(a_ref[...], w_ref[...].astype(BF16), preferred_element_type=F32)
    o_ref[...] = x_ref[...] + g_ref[...] * acc


def _resid(a, w, l, x2, mod3, g_idx, seq, *, tn_prefs, k_part=(0, 1), name):
    m = a.shape[0]
    kp_idx, kp_n = k_part
    k = a.shape[1] // kp_n
    d = w.shape[-1]
    assert a.shape[1] % kp_n == 0 and k % V7X_LANES == 0 and (kp_n == 1 or w.ndim == 2)
    tm = _pick(seq, (1024, 512, 256, 128))
    per_b = seq // tm
    tn = _pick(d, tn_prefs)
    nj = d // tn
    wsz = w.dtype.itemsize
    w_spec = _w_spec(w, l, k, tn) if w.ndim == 3 else pl.BlockSpec((k, tn), lambda i, j: (kp_idx, j))
    return pl.pallas_call(
        _resid_kernel,
        grid=(m // tm, nj),
        in_specs=[pl.BlockSpec((tm, k), lambda i, j: (i, kp_idx)),
                  w_spec,
                  pl.BlockSpec((tm, tn), lambda i, j: (i, j)),
                  pl.BlockSpec((None, 1, tn), lambda i, j: (i // per_b, 0, g_idx * nj + j))],
        out_specs=pl.BlockSpec((tm, tn), lambda i, j: (i, j)),
        out_shape=jax.ShapeDtypeStruct((m, d), F32),
        compiler_params=pltpu.CompilerParams(
            dimension_semantics=("arbitrary", "arbitrary"),
            vmem_limit_bytes=int(min(VMEM_CAP, 2 * tm * k * 2 + k * tn * (2 * wsz + 2)
                                     + 6 * tm * tn * 4 + (4 << 20)))),
        name=name,
    )(a, w, x2, mod3)


HALO = 8


def _up_kernel(h_ref, wg_ref, wv_ref, cwg_ref, cwv_ref, cbg_ref, cbv_ref, wd_ref, o_ref, wd_o_ref, *, sub):
    wd_o_ref[...] = wd_ref[...].astype(BF16)
    wg = wg_ref[...].astype(BF16)
    wv = wv_ref[...].astype(BF16)
    tn = wg.shape[1]

    def conv(a, halo, cw_ref, cb_ref):
        ext = jnp.concatenate([halo, a], axis=0)
        acc = cb_ref[...] + pltpu.roll(ext, 2, 0)[HALO:] * cw_ref[0:1, :]
        acc = acc + pltpu.roll(ext, 1, 0)[HALO:] * cw_ref[1:2, :]
        return acc + a * cw_ref[2:3, :]

    halo_g = halo_v = jnp.zeros((HALO, tn), F32)
    for s in range(h_ref.shape[0] // sub):
        rs = slice(s * sub, (s + 1) * sub)
        hs = h_ref[rs, :]
        ag = jnp.dot(hs, wg, preferred_element_type=F32)
        av = jnp.dot(hs, wv, preferred_element_type=F32)
        gate = conv(ag, halo_g, cwg_ref, cbg_ref)
        val = conv(av, halo_v, cwv_ref, cbv_ref)
        gb = gate.astype(BF16)
        o_ref[rs, :] = (gb * jax.nn.sigmoid(gb) * val.astype(BF16)).astype(o_ref.dtype)
        halo_g, halo_v = ag[sub - HALO:], av[sub - HALO:]


def _up_conv_gate(h2, w_up, l, conv_w, conv_b, w_down, seq):
    m, k = h2.shape
    f = w_up.shape[-1] // 2
    d_out = w_down.shape[-1]
    tn = _pick(f, (256, 128))
    nj = f // tn
    n_steps = (m // seq) * nj
    assert f % (n_steps * BF16_SUBLANES) == 0
    slab = f // n_steps
    sub = _pick(seq, (512, 256, 128))
    wsz = w_up.dtype.itemsize
    return pl.pallas_call(
        functools.partial(_up_kernel, sub=sub),
        grid=(m // seq, nj),
        in_specs=[pl.BlockSpec((seq, k), lambda i, j: (i, 0), pipeline_mode=pl.Buffered(1)),
                  _w_spec(w_up, l, k, tn),
                  _w_spec(w_up, l, k, tn, nj),
                  pl.BlockSpec((None, CONV_W, tn), lambda i, j: (l, 0, j)),
                  pl.BlockSpec((None, CONV_W, tn), lambda i, j: (l, 0, j + nj)),
                  pl.BlockSpec((None, 1, tn), lambda i, j: (l, 0, j)),
                  pl.BlockSpec((None, 1, tn), lambda i, j: (l, 0, j + nj)),
                  pl.BlockSpec((None, slab, d_out), lambda i, j: (l, i * nj + j, 0))],
        out_specs=[pl.BlockSpec((seq, tn), lambda i, j: (i, j)),
                   pl.BlockSpec((slab, d_out), lambda i, j: (i * nj + j, 0))],
        out_shape=[jax.ShapeDtypeStruct((m, f), BF16),
                   jax.ShapeDtypeStruct((f, d_out), BF16)],
        compiler_params=pltpu.CompilerParams(
            dimension_semantics=("arbitrary", "arbitrary"),
            vmem_limit_bytes=int(min(VMEM_CAP, seq * k * 2 + 2 * k * tn * (2 * wsz + 2) + 2 * seq * tn * 2
                                     + 12 * sub * tn * 4 + 2 * slab * d_out * 6 + (4 << 20)))),
        name="up_conv_gate",
    )(h2, w_up, w_up, conv_w, conv_w, conv_b.reshape(conv_b.shape[0], 1, 2 * f),
      conv_b.reshape(conv_b.shape[0], 1, 2 * f), w_down)


def kernel(x, c, ada_w, ada_b, norm1_g, w_in, q_norm_g, k_norm_g, sgu_norm_g, sgu_w, sgu_b,
           w_branch_a, w_branch_b, w_out, norm2_g, w_up, conv_w, conv_b, w_down):
    batch, seq, d = x.shape
    m = batch * seq
    depth = ada_w.shape[0]
    assert seq % GM_CHUNK == 0 and d % V7X_LANES == 0

    x2 = x.reshape(m, d)
    bp = -(-batch // 8) * 8
    c_pad = jnp.pad(c, ((0, bp - batch), (0, 0)))

    o_q = 0
    o_k = o_q + ATT_WIDTH
    o_v = o_k + KV_WIDTH
    o_qi = o_v + KV_WIDTH
    o_ki = o_qi + IDX_HEADS * IDX_DIM
    o_wi = o_ki + IDX_DIM
    o_gu = o_wi + IDX_HEADS
    o_ga = o_gu + 2 * GM_WIDTH
    o_end = o_ga + 2 * d

    assert o_gu - o_ki <= V7X_LANES and o_end == w_in.shape[2]
    w_in_t = jnp.swapaxes(w_in, 1, 2)

    for l in range(depth):
        mod = _ada(c_pad, ada_w, ada_b, l)
        mod3 = mod.reshape(bp, 1, 6 * d)

        h = _norm_mod(x2, norm1_g[l], mod3, 1, 0, seq)
        q = _proj(h, w_in_t, l, o_q, ATT_WIDTH, out_dtype=BF16, epilogue="headnorm", gain=q_norm_g[l],
                  post_scale=HEAD_DIM ** -0.5 * LOG2E, name="proj_q")
        k = _proj(h, w_in_t, l, o_k, KV_WIDTH, out_dtype=BF16, epilogue="headnorm", gain=k_norm_g[l],
                  name="proj_k")
        v = _proj(h, w_in_t, l, o_v, KV_WIDTH, out_dtype=BF16, name="proj_v")
        qi = _proj(h, w_in_t, l, o_qi, IDX_HEADS * IDX_DIM, out_dtype=BF16, name="proj_qi")
        kw = _proj(h, w_in_t, l, o_ki, V7X_LANES, out_dtype=F32, name="proj_kw")
        uv = _proj(h, w_in_t, l, o_gu, 2 * GM_WIDTH, out_dtype=BF16, epilogue="gelu", name="proj_uv")
        gates = _proj(h, w_in_t, l, o_ga, 2 * d, out_dtype=BF16, name="proj_gates")

        y_a, (w_out_bf, w_a_bf, w_b_bf) = _attention(q, qi, kw, k, v, batch, seq,
                                                     (w_out, w_branch_a, w_branch_b), l)
        y_b = _sgu(uv, sgu_norm_g[l], sgu_w[l], sgu_b[l], seq)
        merged = _merge(y_a, y_b, w_a_bf, w_b_bf, l, gates)
        x2 = _resid(merged, w_out_bf, l, x2, mod3, 2, seq,
                    tn_prefs=(512, 256, 128), name="out_proj_resid")

        h2 = _norm_mod(x2, norm2_g[l], mod3, 4, 3, seq)
        act, w_down_bf = _up_conv_gate(h2, w_up, l, conv_w, conv_b, w_down, seq)
        k_parts = 2 if act.shape[1] % (2 * V7X_LANES) == 0 else 1
        for kp in range(k_parts):
            x2 = _resid(act, w_down_bf, l, x2, mod3, 5, seq,
                        tn_prefs=(512, 256, 128), k_part=(kp, k_parts), name="down_proj_resid")

    return x2.reshape(batch, seq, d)
```

```python
import functools

import numpy as np
import jax
import jax.numpy as jnp
from jax import lax
from jax.experimental import pallas as pl
from jax.experimental.pallas import tpu as pltpu

N_HEADS = 16
HEAD_DIM = 128
N_KV_HEADS = 4
ATT_WIDTH = N_HEADS * HEAD_DIM
KV_WIDTH = N_KV_HEADS * HEAD_DIM
IDX_HEADS = 32
IDX_DIM = 64
TOPK_MAX = 256
GM_WIDTH = 2048
GM_GROUPS = 8
GM_GROUP_W = GM_WIDTH // GM_GROUPS
GM_CHUNK = 128
CONV_W = 3
EPS = 1e-6
NEG_BIG = -1e30

V7X_LANES = 128
V7X_VMEM_BYTES = 64 * 1024 * 1024
VMEM_CAP = V7X_VMEM_BYTES - 8 * 1024 * 1024

BF16 = jnp.bfloat16
F32 = jnp.float32
INT_MIN = -(2 ** 31)


def _vmem_limit(*nbytes):
    return int(min(VMEM_CAP, 2 * sum(nbytes) + (4 << 20)))


def _params(semantics, *nbytes):
    return pltpu.CompilerParams(dimension_semantics=semantics, vmem_limit_bytes=_vmem_limit(*nbytes))


def _pick(n, prefs):
    for p in prefs:
        if n % p == 0:
            return p
    return n


def _ada_kernel(c_ref, w_ref, b_ref, o_ref):
    c = c_ref[...]
    cs = c * jax.nn.sigmoid(c)
    o_ref[...] = jnp.dot(cs.astype(BF16), w_ref[...].astype(BF16),
                         preferred_element_type=F32) + b_ref[...]


def _ada(c_pad, ada_w, ada_b, l):
    bp, d = c_pad.shape
    n = ada_w.shape[2]
    tn = _pick(n, (512, 256, 128))
    return pl.pallas_call(
        _ada_kernel,
        grid=(n // tn,),
        in_specs=[pl.BlockSpec((bp, d), lambda j: (0, 0)),
                  pl.BlockSpec((None, d, tn), lambda j: (l, 0, j)),
                  pl.BlockSpec((None, 1, tn), lambda j: (l, 0, j))],
        out_specs=pl.BlockSpec((bp, tn), lambda j: (0, j)),
        out_shape=jax.ShapeDtypeStruct((bp, n), F32),
        compiler_params=_params(("arbitrary",), 2 * d * tn * 4, d * tn * 2),
        name="ada_mod",
    )(c_pad, ada_w, ada_b.reshape(ada_b.shape[0], 1, n))


def _norm_mod_kernel(x_ref, g_ref, sc_ref, sh_ref, o_ref):
    x = x_ref[...]
    y = x * lax.rsqrt(jnp.mean(x * x, axis=-1, keepdims=True) + EPS) * g_ref[...]
    o_ref[...] = (y * (1.0 + sc_ref[...]) + sh_ref[...]).astype(o_ref.dtype)


def _norm_mod(x2, g, mod3, sc_idx, sh_idx, seq):
    m, d = x2.shape
    tm = _pick(seq, (512, 256, 128, 64, 8))
    per_b = seq // tm
    return pl.pallas_call(
        _norm_mod_kernel,
        grid=(m // tm,),
        in_specs=[pl.BlockSpec((tm, d), lambda i: (i, 0)),
                  pl.BlockSpec((1, d), lambda i: (0, 0)),
                  pl.BlockSpec((None, 1, d), lambda i: (i // per_b, 0, sc_idx)),
                  pl.BlockSpec((None, 1, d), lambda i: (i // per_b, 0, sh_idx))],
        out_specs=pl.BlockSpec((tm, d), lambda i: (i, 0)),
        out_shape=jax.ShapeDtypeStruct((m, d), BF16),
        compiler_params=_params(("arbitrary",), 2 * tm * d * 4, 2 * tm * d * 2),
        name="norm_mod",
    )(x2, g.reshape(1, d), mod3, mod3)


def _gelu_exact(x):
    return 0.5 * x * (1.0 + lax.erf(x * (2.0 ** -0.5)))


MXU_ACC_ROWS = 512


_NT = (((1,), (1,)), ((), ()))


def _proj_kernel(a_ref, w_ref, *rest, epilogue, post_scale):
    o_ref = rest[-1]
    tm, tn = o_ref.shape
    rsub = min(tm, MXU_ACC_ROWS)
    w = w_ref[...].astype(BF16)
    for rb in range(tm // rsub):
        rs = slice(rb * rsub, (rb + 1) * rsub)
        acc = lax.dot_general(a_ref[rs, :], w, _NT, preferred_element_type=F32)
        if epilogue == "plain":
            out = acc
        elif epilogue == "gelu":
            out = _gelu_exact(acc)
        elif epilogue == "sigmoid":
            out = jax.nn.sigmoid(acc)
        elif epilogue in ("headnorm", "headnorm_first_tile"):
            gain = rest[0][...]
            parts = []
            for c in range(tn // HEAD_DIM):
                blk = acc[:, c * HEAD_DIM:(c + 1) * HEAD_DIM]
                ms = jnp.mean(blk * blk, axis=-1, keepdims=True)
                parts.append(blk * lax.rsqrt(ms + EPS) * (gain * post_scale))
            out = jnp.concatenate(parts, axis=1) if len(parts) > 1 else parts[0]
            if epilogue == "headnorm_first_tile":
                out = jnp.where(pl.program_id(1) == 0, out, acc)
        else:
            raise ValueError(epilogue)
        o_ref[rs, :] = out.astype(o_ref.dtype)


def _proj(a, w_t, l, row0, n, *, out_dtype, epilogue="plain", gain=None, post_scale=1.0, name):
    m, k = a.shape
    assert row0 % 8 == 0 and w_t.shape[2] == k
    tm = _pick(m, (1024, 512, 256, 128))
    tn = _pick(n, (512, 256, 128))
    in_specs = [pl.BlockSpec((tm, k), lambda i, j: (i, 0)),
                pl.BlockSpec((None, pl.Element(tn), pl.Element(k)),
                             lambda i, j: (l, pl.multiple_of(row0 + j * tn, 8), 0))]
    args = [a, w_t]
    if epilogue in ("headnorm", "headnorm_first_tile"):
        in_specs.append(pl.BlockSpec((1, HEAD_DIM), lambda i, j: (0, 0)))
        args.append(gain.reshape(1, HEAD_DIM))
    osz = jnp.dtype(out_dtype).itemsize
    return pl.pallas_call(
        functools.partial(_proj_kernel, epilogue=epilogue, post_scale=post_scale),
        grid=(m // tm, n // tn),
        in_specs=in_specs,
        out_specs=pl.BlockSpec((tm, tn), lambda i, j: (i, j)),
        out_shape=jax.ShapeDtypeStruct((m, n), out_dtype),
        compiler_params=pltpu.CompilerParams(
            dimension_semantics=("arbitrary", "arbitrary"),
            vmem_limit_bytes=int(min(VMEM_CAP, 2 * tm * k * 2 + tn * k * (2 * 4 + 2) + 2 * tm * tn * osz
                                     + 3 * tm * tn * 4 + (4 << 20)))),
        name=name,
    )(*args)


def _alibi_slopes(n):
    return [2.0 ** (-8.0 * (i + 1) / n) for i in range(n)]


F32_SUBLANES = 8
BF16_SUBLANES = 16


def _fold_rows(x, op):
    rows = x.shape[0]
    while rows > F32_SUBLANES and rows % (2 * F32_SUBLANES) == 0:
        rows //= 2
        x = op(x[:rows], x[rows:])
    return x


def _reduce_rows(x, op):
    return (jnp.max if op is jnp.maximum else jnp.sum)(_fold_rows(x, op), axis=0, keepdims=True)


KEY_NEG_INF = INT_MIN + 0x7FFFFF


def _key_to_float(key):
    bits = key ^ ((key >> 31) & jnp.int32(0x7FFFFFFF))
    return lax.bitcast_convert_type(bits, F32)


LOG2E = 1.4426950408889634
ALIBI_PIECES = 4
POS_SPLIT = 256


def _alibi_tables(seq):
    assert seq <= POS_SPLIT * POS_SPLIT and 2 * ALIBI_PIECES <= V7X_LANES
    slope_tab = np.zeros((N_HEADS, V7X_LANES), np.float32)
    for h, s in enumerate(_alibi_slopes(N_HEADS)):
        rest = float(np.float32(s)) * LOG2E
        for p in range(ALIBI_PIECES):
            piece = float(np.asarray(rest, dtype=BF16))
            slope_tab[h, 2 * p:2 * p + 2] = piece
            rest -= piece
    pos = np.arange(seq)
    pos_tab = np.zeros((seq, V7X_LANES), np.float32)
    pos_tab[:, 0:2 * ALIBI_PIECES:2] = ((pos // POS_SPLIT) * POS_SPLIT)[:, None]
    pos_tab[:, 1:2 * ALIBI_PIECES:2] = (pos % POS_SPLIT)[:, None]
    return jnp.asarray(slope_tab), jnp.asarray(pos_tab, dtype=BF16)


N_CAST = 3


def _attn_kernel(q_ref, qi_ref, kwq_ref, k_ref, v_ref, kwa_ref, slope_ref, pos_ref, *rest, tq, seq, topk):
    cast_in, (y_ref, *cast_out) = rest[:N_CAST], rest[N_CAST:2 * N_CAST + 1]
    score_scr, bias_scr, wt_scr, sel_scr, vt_scr, qa_scr, ml_scr, acc_scr = rest[2 * N_CAST + 1:]
    for src, dst in zip(cast_in, cast_out):
        dst[...] = src[...].astype(BF16)
    i = pl.program_id(1)
    n_chunks = i + 1
    t0 = i * tq
    rep = N_HEADS // N_KV_HEADS
    nt = (((1,), (1,)), ((), ()))
    row = lax.broadcasted_iota(jnp.int32, (tq, tq), 0)
    col = lax.broadcasted_iota(jnp.int32, (tq, tq), 1)
    kf = float(topk)

    w_fold = (IDX_HEADS ** -0.5) * (IDX_DIM ** -0.5)
    wt_scr[...] = (kwq_ref[...] * w_fold).T
    lane = lax.broadcasted_iota(jnp.int32, (tq, V7X_LANES), 1)

    def score_body(c, carry):
        off = pl.multiple_of(c * tq, tq)
        kraw = kwa_ref[pl.ds(off, tq), :]
        k_even = jnp.where(lane < IDX_DIM, kraw, 0.0)
        k_odd = jnp.where(lane >= IDX_DIM, pltpu.roll(kraw, IDX_DIM, 1), 0.0)
        lhs = jnp.concatenate([k_even, k_odd], axis=0).astype(BF16)
        acc = jnp.zeros((tq, tq), F32)
        for p in range(IDX_HEADS // 2):
            lg = lax.dot_general(lhs, qi_ref[:, p * V7X_LANES:(p + 1) * V7X_LANES], nt,
                                 preferred_element_type=F32)
            h0 = IDX_DIM + 2 * p
            acc = acc + wt_scr[h0:h0 + 1, :] * jnp.maximum(lg[:tq], 0.0)
            acc = acc + wt_scr[h0 + 1:h0 + 2, :] * jnp.maximum(lg[tq:], 0.0)
        score_scr[c] = jnp.where(off + row <= t0 + col, acc, -jnp.inf)
        return carry

    lax.fori_loop(0, n_chunks, score_body, 0)

    def count(pred):
        def body(c, part):
            off = c * tq
            return part + _fold_rows(jnp.where(pred(score_scr[c], off + row), 1.0, 0.0), jnp.add)
        part = lax.fori_loop(0, n_chunks, body, jnp.zeros((F32_SUBLANES, tq), F32))
        return jnp.sum(part, axis=0, keepdims=True)

    def select_topk():
        cnt_nonneg = count(lambda sc, s: sc >= 0.0)
        nonneg = cnt_nonneg >= kf
        prefix0 = jnp.where(nonneg, jnp.int32(0), jnp.int32(INT_MIN))
        cnt0 = jnp.where(nonneg, cnt_nonneg, (n_chunks * tq).astype(F32))

        def bit_body(j, carry):
            prefix, cnt_prefix = carry
            cand = prefix | lax.shift_left(jnp.int32(1), 30 - j)
            cand_f = _key_to_float(cand)
            cnt = count(lambda sc, s: sc >= cand_f)
            keep = cnt >= kf
            return jnp.where(keep, cand, prefix), jnp.where(keep, cnt, cnt_prefix)

        thr_key, cnt_ge = lax.fori_loop(0, 31, bit_body, (prefix0, cnt0))
        thr_key = jnp.maximum(thr_key, jnp.int32(KEY_NEG_INF))
        thr = _key_to_float(thr_key)
        sel_scr[0:1, :] = thr_key
        sel_scr[1:2, :] = jnp.full((1, tq), seq, jnp.int32)

        @pl.when(jnp.max(cnt_ge) > kf)
        def _():
            nbits = max(1, (seq - 1).bit_length())
            need = kf - count(lambda sc, s: sc > thr)

            def idx_body(j, m):
                cand = m | lax.shift_left(jnp.int32(1), nbits - 1 - j)
                cnt = count(lambda sc, s: (sc == thr) & (s < cand))
                return jnp.where(cnt < need, cand, m)

            sel_scr[1:2, :] = lax.fori_loop(0, nbits, idx_body, jnp.zeros((1, tq), jnp.int32))

    if tq <= topk:
        @pl.when(i == 0)
        def _():
            sel_scr[0:1, :] = jnp.full((1, tq), KEY_NEG_INF, jnp.int32)
            sel_scr[1:2, :] = jnp.full((1, tq), -1, jnp.int32)

        pl.when(i > 0)(select_topk)
    else:
        select_topk()

    thr = _key_to_float(sel_scr[0:1, :])
    m_idx = sel_scr[1:2, :]

    def bias_body(c, carry):
        off = c * tq
        kb = score_scr[c]
        s_idx = off + row
        take = jnp.where(kb > thr, 1.0, jnp.where((kb == thr) & (s_idx <= m_idx), 1.0, 0.0))
        take = jnp.where(s_idx <= t0 + col, take, 0.0)
        bias_scr[c] = jnp.where(take > 0.5, 0.0, NEG_BIG)
        return carry

    lax.fori_loop(0, n_chunks, bias_body, 0)

    @pl.when(i == 0)
    def _():
        for c in range(seq // tq):
            for g in range(N_KV_HEADS):
                blk = v_ref[c * tq:(c + 1) * tq, g * HEAD_DIM:(g + 1) * HEAD_DIM].astype(F32)
                vt_scr[c, g * HEAD_DIM:(g + 1) * HEAD_DIM, :] = blk.T.astype(BF16)

    for h in range(N_HEADS):
        g, r = divmod(h, rep)
        qa_scr[g, r * tq:(r + 1) * tq, :HEAD_DIM] = q_ref[:, h * HEAD_DIM:(h + 1) * HEAD_DIM]
        qa_scr[g, r * tq:(r + 1) * tq, HEAD_DIM:] = jnp.broadcast_to(
            slope_ref[h:h + 1, :], (tq, V7X_LANES)).astype(BF16)
    acc_scr[...] = jnp.zeros(acc_scr.shape, F32)
    for g in range(N_KV_HEADS):
        ml_scr[2 * g:2 * g + 1, :] = jnp.full((1, rep * tq), NEG_BIG, F32)
        ml_scr[2 * g + 1:2 * g + 2, :] = jnp.zeros((1, rep * tq), F32)

    def attn_body(c, carry):
        off = pl.multiple_of(c * tq, tq)
        pos_c = pos_ref[pl.ds(off, tq), :]
        bias = jnp.concatenate([bias_scr[c]] * rep, axis=1)
        def scores(g):
            kc = jnp.concatenate([k_ref[pl.ds(off, tq), g * HEAD_DIM:(g + 1) * HEAD_DIM], pos_c],
                                 axis=1)
            return lax.dot_general(kc, qa_scr[g], nt, preferred_element_type=F32) + bias

        lead = N_KV_HEADS - 1
        sts = [scores(g) for g in range(lead)]
        for g in range(N_KV_HEADS):
            st = sts[g]
            if g + lead < N_KV_HEADS:
                sts.append(scores(g + lead))
            vt = vt_scr[c, g * HEAD_DIM:(g + 1) * HEAD_DIM, :]
            m_prev = ml_scr[2 * g:2 * g + 1, :]
            m_new = jnp.maximum(m_prev, _reduce_rows(st, jnp.maximum))
            alpha = jnp.exp2(m_prev - m_new)
            pt = jnp.exp2(st - m_new)
            ml_scr[2 * g:2 * g + 1, :] = m_new
            ml_scr[2 * g + 1:2 * g + 2, :] = (alpha * ml_scr[2 * g + 1:2 * g + 2, :]
                                              + _reduce_rows(pt, jnp.add))
            acc_scr[g] = alpha * acc_scr[g] + jnp.dot(vt, pt.astype(BF16), preferred_element_type=F32)
        return carry

    lax.fori_loop(0, n_chunks, attn_body, 0)
    for g in range(N_KV_HEADS):
        o_t = acc_scr[g] / ml_scr[2 * g + 1:2 * g + 2, :]
        for r in range(rep):
            h = g * rep + r
            y_ref[:, h * HEAD_DIM:(h + 1) * HEAD_DIM] = o_t[:, r * tq:(r + 1) * tq].T.astype(y_ref.dtype)


def _attention(q, qi, kw, kv, batch, seq, cast_weights, l):
    assert kv.shape[1] == 2 * KV_WIDTH
    m = q.shape[0]
    topk = min(TOPK_MAX, seq // 4)
    tq = _pick(seq, (256, 128))
    nq = seq // tq
    kww = kw.shape[1]
    rep = N_HEADS // N_KV_HEADS
    assert HEAD_DIM == V7X_LANES and tq % V7X_LANES == 0
    assert kww == V7X_LANES == 2 * IDX_DIM and IDX_DIM + IDX_HEADS <= V7X_LANES
    n_steps = batch * nq
    assert len(cast_weights) == N_CAST and all(w.shape[1] % (n_steps * BF16_SUBLANES) == 0 for w in cast_weights)
    slabs = [w.shape[1] // n_steps for w in cast_weights]
    cast_bytes = sum(2 * s * w.shape[2] * 6 for s, w in zip(slabs, cast_weights))
    outs = pl.pallas_call(
        functools.partial(_attn_kernel, tq=tq, seq=seq, topk=topk),
        grid=(batch, nq),
        in_specs=[pl.BlockSpec((tq, ATT_WIDTH), lambda b, i: (b * nq + i, 0)),
                  pl.BlockSpec((tq, IDX_HEADS * IDX_DIM), lambda b, i: (b * nq + i, 0)),
                  pl.BlockSpec((tq, kww), lambda b, i: (b * nq + i, 0)),
                  pl.BlockSpec((seq, KV_WIDTH), lambda b, i: (b, 0)),
                  pl.BlockSpec((seq, KV_WIDTH), lambda b, i: (b, 1)),
                  pl.BlockSpec((seq, kww), lambda b, i: (b, 0)),
                  pl.BlockSpec((N_HEADS, V7X_LANES), lambda b, i: (0, 0)),
                  pl.BlockSpec((seq, V7X_LANES), lambda b, i: (0, 0))]
        + [pl.BlockSpec((None, s, w.shape[2]), lambda b, i: (l, b * nq + i, 0))
           for s, w in zip(slabs, cast_weights)],
        out_specs=[pl.BlockSpec((tq, ATT_WIDTH), lambda b, i: (b * nq + i, 0))]
        + [pl.BlockSpec((s, w.shape[2]), lambda b, i: (b * nq + i, 0)) for s, w in zip(slabs, cast_weights)],
        out_shape=[jax.ShapeDtypeStruct((m, ATT_WIDTH), BF16)]
        + [jax.ShapeDtypeStruct(w.shape[1:], BF16) for w in cast_weights],
        scratch_shapes=[pltpu.VMEM((nq, tq, tq), F32),
                        pltpu.VMEM((nq, tq, tq), F32),
                        pltpu.VMEM((V7X_LANES, tq), F32),
                        pltpu.VMEM((8, tq), jnp.int32),
                        pltpu.VMEM((nq, KV_WIDTH, tq), BF16),
                        pltpu.VMEM((N_KV_HEADS, rep * tq, 2 * HEAD_DIM), BF16),
                        pltpu.VMEM((2 * N_KV_HEADS, rep * tq), F32),
                        pltpu.VMEM((N_KV_HEADS, HEAD_DIM, rep * tq), F32)],
        compiler_params=_params(("arbitrary", "arbitrary"),
                                6 * tq * ATT_WIDTH * 2, 4 * seq * KV_WIDTH * 2, 2 * (seq + tq) * kww * 4,
                                2 * seq * tq * 4, 8 * tq * tq * 4, seq * KV_WIDTH * 2, rep * HEAD_DIM * tq * 4,
                                cast_bytes // 2),
        name="dsa_attention",
    )(q, qi, kw, kv, kv, kw, *_alibi_tables(seq), *cast_weights)
    return outs[0], outs[1:]


def _sgu_kernel(u_ref, v_ref, g_ref, w_ref, bt_ref, o_ref, *, rows):
    v = v_ref[...].astype(F32)
    vn = (v * lax.rsqrt(jnp.mean(v * v, axis=-1, keepdims=True) + EPS) * g_ref[...]).astype(BF16)
    r_i = lax.broadcasted_iota(jnp.int32, (GM_CHUNK, GM_CHUNK), 0)
    c_i = lax.broadcasted_iota(jnp.int32, (GM_CHUNK, GM_CHUNK), 1)
    for g in range(GM_GROUPS):
        w = jnp.where(r_i >= c_i, w_ref[g], 0.0).astype(BF16)
        bcol = bt_ref[:, g:g + 1]
        cs = slice(g * GM_GROUP_W, (g + 1) * GM_GROUP_W)
        for n in range(rows // GM_CHUNK):
            rs = slice(n * GM_CHUNK, (n + 1) * GM_CHUNK)
            f = jnp.dot(w, vn[rs, cs], preferred_element_type=F32) + bcol
            o_ref[rs, cs] = (u_ref[rs, cs].astype(F32) * f).astype(o_ref.dtype)


def _sgu(uv, gain, w_s, b_s, seq):
    m = uv.shape[0]
    rows = _pick(seq, (512, 256, 128))
    return pl.pallas_call(
        functools.partial(_sgu_kernel, rows=rows),
        grid=(m // rows,),
        in_specs=[pl.BlockSpec((rows, GM_WIDTH), lambda i: (i, 0)),
                  pl.BlockSpec((rows, GM_WIDTH), lambda i: (i, 1)),
                  pl.BlockSpec((1, GM_WIDTH), lambda i: (0, 0)),
                  pl.BlockSpec((GM_GROUPS, GM_CHUNK, GM_CHUNK), lambda i: (0, 0, 0)),
                  pl.BlockSpec((GM_CHUNK, GM_GROUPS), lambda i: (0, 0))],
        out_specs=pl.BlockSpec((rows, GM_WIDTH), lambda i: (i, 0)),
        out_shape=jax.ShapeDtypeStruct((m, GM_WIDTH), BF16),
        compiler_params=_params(("arbitrary",), 6 * rows * GM_WIDTH * 2, 2 * rows * GM_WIDTH * 4),
        name="sgu",
    )(uv, uv, gain.reshape(1, GM_WIDTH), w_s, b_s.T)


def _w_spec(w, l, k, tn, j0=0):
    if w.ndim == 3:
        return pl.BlockSpec((None, k, tn), lambda i, j: (l, 0, j + j0))
    return pl.BlockSpec((k, tn), lambda i, j: (0, j + j0))


def _merge_kernel(ya_ref, yb_ref, wa_ref, wb_ref, ga_ref, gb_ref, o_ref):
    wa = wa_ref[...].astype(BF16)
    wb = wb_ref[...].astype(BF16)
    tm = o_ref.shape[0]
    rsub = min(tm, MXU_ACC_ROWS)
    for rb in range(tm // rsub):
        rs = slice(rb * rsub, (rb + 1) * rsub)
        pa = jnp.dot(ya_ref[rs, :], wa, preferred_element_type=F32)
        pb = jnp.dot(yb_ref[rs, :], wb, preferred_element_type=F32)
        ga = jax.nn.sigmoid(ga_ref[rs, :].astype(F32))
        gb = jax.nn.sigmoid(gb_ref[rs, :].astype(F32))
        o_ref[rs, :] = (ga * pa + gb * pb).astype(o_ref.dtype)


def _merge(ya, yb, wa, wb, l, gates):
    m, ka = ya.shape
    kb = yb.shape[1]
    d = wa.shape[-1]
    tm = _pick(m, (1024, 512, 256, 128))
    tn = _pick(d, (512, 256, 128))
    nj = d // tn
    wsz = wa.dtype.itemsize
    return pl.pallas_call(
        _merge_kernel,
        grid=(m // tm, nj),
        in_specs=[pl.BlockSpec((tm, ka), lambda i, j: (i, 0)),
                  pl.BlockSpec((tm, kb), lambda i, j: (i, 0)),
                  _w_spec(wa, l, ka, tn),
                  _w_spec(wb, l, kb, tn),
                  pl.BlockSpec((tm, tn), lambda i, j: (i, j)),
                  pl.BlockSpec((tm, tn), lambda i, j: (i, j + nj))],
        out_specs=pl.BlockSpec((tm, tn), lambda i, j: (i, j)),
        out_shape=jax.ShapeDtypeStruct((m, d), BF16),
        compiler_params=pltpu.CompilerParams(
            dimension_semantics=("arbitrary", "arbitrary"),
            vmem_limit_bytes=int(min(VMEM_CAP, 2 * tm * (ka + kb) * 2 + (ka + kb) * tn * (2 * wsz + 2)
                                     + 6 * tm * tn * 2 + 3 * tm * tn * 4 + (4 << 20)))),
        name="merge",
    )(ya, yb, wa, wb, gates, gates)


def _resid_kernel(a_ref, w_ref, x_ref, g_ref, o_ref):
    acc = jnp.dot(a_ref[...], w_ref[...].astype(BF16), preferred_element_type=F32)
    o_ref[...] = x_ref[...] + g_ref[...] * acc


def _resid(a, w, l, x2, mod3, g_idx, seq, *, tn_prefs, k_part=(0, 1), name):
    m = a.shape[0]
    kp_idx, kp_n = k_part
    k = a.shape[1] // kp_n
    d = w.shape[-1]
    assert a.shape[1] % kp_n == 0 and k % V7X_LANES == 0 and (kp_n == 1 or w.ndim == 2)
    tm = _pick(seq, (1024, 512, 256, 128))
    per_b = seq // tm
    tn = _pick(d, tn_prefs)
    nj = d // tn
    wsz = w.dtype.itemsize
    w_spec = _w_spec(w, l, k, tn) if w.ndim == 3 else pl.BlockSpec((k, tn), lambda i, j: (kp_idx, j))
    return pl.pallas_call(
        _resid_kernel,
        grid=(m // tm, nj),
        in_specs=[pl.BlockSpec((tm, k), lambda i, j: (i, kp_idx)),
                  w_spec,
                  pl.BlockSpec((tm, tn), lambda i, j: (i, j)),
                  pl.BlockSpec((None, 1, tn), lambda i, j: (i // per_b, 0, g_idx * nj + j))],
        out_specs=pl.BlockSpec((tm, tn), lambda i, j: (i, j)),
        out_shape=jax.ShapeDtypeStruct((m, d), F32),
        compiler_params=pltpu.CompilerParams(
            dimension_semantics=("arbitrary", "arbitrary"),
            vmem_limit_bytes=int(min(VMEM_CAP, 2 * tm * k * 2 + k * tn * (2 * wsz + 2)
                                     + 6 * tm * tn * 4 + (4 << 20)))),
        name=name,
    )(a, w, x2, mod3)


HALO = 8


def _up_kernel(h_ref, wg_ref, wv_ref, cwg_ref, cwv_ref, cbg_ref, cbv_ref, wd_ref, o_ref, wd_o_ref, *, sub):
    wd_o_ref[...] = wd_ref[...].astype(BF16)
    wg = wg_ref[...].astype(BF16)
    wv = wv_ref[...].astype(BF16)
    tn = wg.shape[1]

    def conv(a, halo, cw_ref, cb_ref):
        ext = jnp.concatenate([halo, a], axis=0)
        acc = cb_ref[...] + pltpu.roll(ext, 2, 0)[HALO:] * cw_ref[0:1, :]
        acc = acc + pltpu.roll(ext, 1, 0)[HALO:] * cw_ref[1:2, :]
        return acc + a * cw_ref[2:3, :]

    halo_g = halo_v = jnp.zeros((HALO, tn), F32)
    for s in range(h_ref.shape[0] // sub):
        rs = slice(s * sub, (s + 1) * sub)
        hs = h_ref[rs, :]
        ag = jnp.dot(hs, wg, preferred_element_type=F32)
        av = jnp.dot(hs, wv, preferred_element_type=F32)
        gate = conv(ag, halo_g, cwg_ref, cbg_ref)
        val = conv(av, halo_v, cwv_ref, cbv_ref)
        gb = gate.astype(BF16)
        o_ref[rs, :] = (gb * jax.nn.sigmoid(gb) * val.astype(BF16)).astype(o_ref.dtype)
        halo_g, halo_v = ag[sub - HALO:], av[sub - HALO:]


def _up_conv_gate(h2, w_up, l, conv_w, conv_b, w_down, seq):
    m, k = h2.shape
    f = w_up.shape[-1] // 2
    d_out = w_down.shape[-1]
    tn = _pick(f, (256, 128))
    nj = f // tn
    n_steps = (m // seq) * nj
    assert f % (n_steps * BF16_SUBLANES) == 0
    slab = f // n_steps
    sub = _pick(seq, (512, 256, 128))
    wsz = w_up.dtype.itemsize
    return pl.pallas_call(
        functools.partial(_up_kernel, sub=sub),
        grid=(m // seq, nj),
        in_specs=[pl.BlockSpec((seq, k), lambda i, j: (i, 0), pipeline_mode=pl.Buffered(1)),
                  _w_spec(w_up, l, k, tn),
                  _w_spec(w_up, l, k, tn, nj),
                  pl.BlockSpec((None, CONV_W, tn), lambda i, j: (l, 0, j)),
                  pl.BlockSpec((None, CONV_W, tn), lambda i, j: (l, 0, j + nj)),
                  pl.BlockSpec((None, 1, tn), lambda i, j: (l, 0, j)),
                  pl.BlockSpec((None, 1, tn), lambda i, j: (l, 0, j + nj)),
                  pl.BlockSpec((None, slab, d_out), lambda i, j: (l, i * nj + j, 0))],
        out_specs=[pl.BlockSpec((seq, tn), lambda i, j: (i, j)),
                   pl.BlockSpec((slab, d_out), lambda i, j: (i * nj + j, 0))],
        out_shape=[jax.ShapeDtypeStruct((m, f), BF16),
                   jax.ShapeDtypeStruct((f, d_out), BF16)],
        compiler_params=pltpu.CompilerParams(
            dimension_semantics=("arbitrary", "arbitrary"),
            vmem_limit_bytes=int(min(VMEM_CAP, seq * k * 2 + 2 * k * tn * (2 * wsz + 2) + 2 * seq * tn * 2
                                     + 12 * sub * tn * 4 + 2 * slab * d_out * 6 + (4 << 20)))),
        name="up_conv_gate",
    )(h2, w_up, w_up, conv_w, conv_w, conv_b.reshape(conv_b.shape[0], 1, 2 * f),
      conv_b.reshape(conv_b.shape[0], 1, 2 * f), w_down)


def kernel(x, c, ada_w, ada_b, norm1_g, w_in, q_norm_g, k_norm_g, sgu_norm_g, sgu_w, sgu_b,
           w_branch_a, w_branch_b, w_out, norm2_g, w_up, conv_w, conv_b, w_down):
    batch, seq, d = x.shape
    m = batch * seq
    depth = ada_w.shape[0]
    assert seq % GM_CHUNK == 0 and d % V7X_LANES == 0

    x2 = x.reshape(m, d)
    bp = -(-batch // 8) * 8
    c_pad = jnp.pad(c, ((0, bp - batch), (0, 0)))

    o_q = 0
    o_k = o_q + ATT_WIDTH
    o_v = o_k + KV_WIDTH
    o_qi = o_v + KV_WIDTH
    o_ki = o_qi + IDX_HEADS * IDX_DIM
    o_wi = o_ki + IDX_DIM
    o_gu = o_wi + IDX_HEADS
    o_ga = o_gu + 2 * GM_WIDTH
    o_end = o_ga + 2 * d

    assert o_gu - o_ki <= V7X_LANES and o_end == w_in.shape[2]
    w_in_t = jnp.swapaxes(w_in, 1, 2)

    for l in range(depth):
        mod = _ada(c_pad, ada_w, ada_b, l)
        mod3 = mod.reshape(bp, 1, 6 * d)

        h = _norm_mod(x2, norm1_g[l], mod3, 1, 0, seq)
        q = _proj(h, w_in_t, l, o_q, ATT_WIDTH, out_dtype=BF16, epilogue="headnorm", gain=q_norm_g[l],
                  post_scale=HEAD_DIM ** -0.5 * LOG2E, name="proj_q")
        assert o_v == o_k + KV_WIDTH
        kv = _proj(h, w_in_t, l, o_k, 2 * KV_WIDTH, out_dtype=BF16, epilogue="headnorm_first_tile",
                   gain=k_norm_g[l], name="proj_kv")
        qi = _proj(h, w_in_t, l, o_qi, IDX_HEADS * IDX_DIM, out_dtype=BF16, name="proj_qi")
        kw = _proj(h, w_in_t, l, o_ki, V7X_LANES, out_dtype=F32, name="proj_kw")
        uv = _proj(h, w_in_t, l, o_gu, 2 * GM_WIDTH, out_dtype=BF16, epilogue="gelu", name="proj_uv")
        gates = _proj(h, w_in_t, l, o_ga, 2 * d, out_dtype=BF16, name="proj_gates")

        y_a, (w_out_bf, w_a_bf, w_b_bf) = _attention(q, qi, kw, kv, batch, seq,
                                                     (w_out, w_branch_a, w_branch_b), l)
        y_b = _sgu(uv, sgu_norm_g[l], sgu_w[l], sgu_b[l], seq)
        merged = _merge(y_a, y_b, w_a_bf, w_b_bf, l, gates)
        x2 = _resid(merged, w_out_bf, l, x2, mod3, 2, seq,
                    tn_prefs=(512, 256, 128), name="out_proj_resid")

        h2 = _norm_mod(x2, norm2_g[l], mod3, 4, 3, seq)
        act, w_down_bf = _up_conv_gate(h2, w_up, l, conv_w, conv_b, w_down, seq)
        k_parts = 2 if act.shape[1] % (2 * V7X_LANES) == 0 else 1
        for kp in range(k_parts):
            x2 = _resid(act, w_down_bf, l, x2, mod3, 5, seq,
                        tn_prefs=(512, 256, 128), k_part=(kp, k_parts), name="down_proj_resid")

    return x2.reshape(batch, seq, d)
```
